```python
import jax
import jax.numpy as jnp
from jax import lax
import numpy as np

D_MODEL = 1024
BATCH = 2
SEQ = 8192
DEPTH = 2

GRID_W = 64
CTX_LEN = 256
HEAD_DIM = 64
A_HEADS = 8
A_KV_HEADS = 2
A_WINDOW = 128
A_BLOCK = 128
B_HEADS = 8
NA_ROWS = 8
NA_COLS = 16
QA_W = A_HEADS * HEAD_DIM
KVA_W = A_KV_HEADS * HEAD_DIM
QKVB_W = B_HEADS * HEAD_DIM
EVEN_SPLITS = (QA_W, QA_W + KVA_W, QA_W + 2 * KVA_W, QA_W + 2 * KVA_W + QKVB_W, QA_W + 2 * KVA_W + 2 * QKVB_W)
EVEN_IN = QA_W + 2 * KVA_W + 3 * QKVB_W
EVEN_OUT = (A_HEADS + B_HEADS) * HEAD_DIM
C_HEADS = 16
C_Q_RANK = 384
C_KV_RANK = 256
C_NOPE = 64
C_ROPE = 32
C_V = 64
C_IN = C_Q_RANK + C_KV_RANK + C_ROPE
C_OUT = C_HEADS * C_V
Q_BLOCK = 128
D_FF = 4 * D_MODEL
N_EVEN = (DEPTH + 1) // 2
N_ODD = DEPTH // 2
ROPE_THETA = 10000.0
NORM_EPS = 1e-6
NEG_INF = -1e30

kernel_name = 'hybrid_prefix_dit_block'


def rms_norm(x, g):
    xf = x.astype(jnp.float32)
    y = xf * lax.rsqrt(jnp.mean(xf * xf, axis=-1, keepdims=True) + NORM_EPS)
    return (y * g.astype(jnp.float32)).astype(x.dtype)


def modulate(h, shift, scale):
    return h * (1 + scale) + shift


def adaln(cond, w, b):
    m = jax.nn.silu(cond) @ w + b
    return jnp.split(m[:, None, :], 6, axis=-1)


def rope_1d(x, pos):
    half = x.shape[-1] // 2
    inv = ROPE_THETA ** (-jnp.arange(half, dtype=jnp.float32) / half)
    ang = pos.astype(jnp.float32)[:, None] * inv[None, :]
    cos, sin = jnp.cos(ang)[:, None, :], jnp.sin(ang)[:, None, :]
    x1, x2 = x[..., :half], x[..., half:]
    return jnp.concatenate([x1 * cos - x2 * sin, x1 * sin + x2 * cos], axis=-1).astype(x.dtype)


def rope_2d(x, row, col):
    half = x.shape[-1] // 2
    return jnp.concatenate([rope_1d(x[..., :half], row), rope_1d(x[..., half:], col)], axis=-1)


def squared_relu_mlp(h, w1, w2):
    return jnp.square(jax.nn.relu(h @ w1)) @ w2


def ctx_self_attention(q, k, v, scale, sink=None):
    bsz, n, hq, dq = q.shape
    g = k.shape[2]
    r = hq // g
    qg = q.reshape(bsz, n, g, r, dq)
    s = jnp.einsum('bqgrd,bkgd->bgrqk', qg, k, preferred_element_type=jnp.float32) * scale
    if sink is not None:
        sk = jnp.broadcast_to(sink.astype(jnp.float32).reshape(g, r)[None, :, :, None, None], (bsz, g, r, n, 1))
        s = jnp.concatenate([s, sk], axis=-1)
    p = jax.nn.softmax(s, axis=-1)[..., :n].astype(v.dtype)
    o = jnp.einsum('bgrqk,bkgd->bqgrd', p, v)
    return o.reshape(bsz, n, hq * v.shape[-1])


def window_attention(q, k, v, kc, vc, sink):
    bsz, seq, hq, d = q.shape
    g = k.shape[2]
    r = hq // g
    nb = seq // A_BLOCK
    qb = q.reshape(bsz, nb, A_BLOCK, g, r, d)

    def band(t):
        tb = jnp.pad(t.reshape(bsz, nb, A_BLOCK, g, d), ((0, 0), (1, 1), (0, 0), (0, 0), (0, 0)))
        return jnp.concatenate([tb[:, :-2], tb[:, 1:-1], tb[:, 2:]], axis=2)

    kb, vb = band(k), band(v)
    qpos = jnp.arange(seq).reshape(nb, A_BLOCK)
    kpos = (jnp.arange(nb)[:, None] - 1) * A_BLOCK + jnp.arange(3 * A_BLOCK)[None, :]
    valid = ((kpos[:, None, :] >= 0) & (kpos[:, None, :] < seq)
             & (jnp.abs(qpos[:, :, None] - kpos[:, None, :]) <= A_WINDOW))
    scale = d ** -0.5
    s_loc = jnp.einsum('bnqgrd,bnkgd->bngrqk', qb, kb, preferred_element_type=jnp.float32) * scale
    s_loc = jnp.where(valid[None, :, None, None], s_loc, NEG_INF)
    s_ctx = jnp.einsum('bnqgrd,bcgd->bngrqc', qb, kc, preferred_element_type=jnp.float32) * scale
    s_sink = jnp.broadcast_to(sink.astype(jnp.float32).reshape(g, r)[None, None, :, :, None, None],
                              (bsz, nb, g, r, A_BLOCK, 1))
    p = jax.nn.softmax(jnp.concatenate([s_loc, s_ctx, s_sink], axis=-1), axis=-1).astype(v.dtype)
    nloc = 3 * A_BLOCK
    nctx = kc.shape[1]
    o = (jnp.einsum('bngrqk,bnkgd->bnqgrd', p[..., :nloc], vb)
         + jnp.einsum('bngrqc,bcgd->bnqgrd', p[..., nloc:nloc + nctx], vc))
    return o.reshape(bsz, seq, hq * d)


def neighbourhood_attention(q, k, v, kc, vc, rpb, rows_n):
    bsz, _, h, d = q.shape
    kh = min(NA_ROWS, rows_n)
    r = jnp.arange(rows_n)
    row_idx = jnp.clip(r - kh // 2, 0, rows_n - kh)[:, None] + jnp.arange(kh)[None, :]
    cq = jnp.arange(GRID_W)
    c0 = jnp.clip(cq - NA_COLS // 2, 0, GRID_W - NA_COLS)
    col_ok = (cq[None, :] >= c0[:, None]) & (cq[None, :] < c0[:, None] + NA_COLS)
    dri = row_idx - r[:, None] + NA_ROWS - 1
    dci = jnp.clip(cq[None, :] - cq[:, None], 1 - NA_COLS, NA_COLS - 1) + NA_COLS - 1
    bias = rpb.astype(jnp.float32)[:, dri[:, None, :, None], dci[None, :, None, :]]
    bias = jnp.moveaxis(bias, 0, 1)
    qg = q.reshape(bsz, rows_n, GRID_W, h, d)
    kg = k.reshape(bsz, rows_n, GRID_W, h, d)[:, row_idx]
    vg = v.reshape(bsz, rows_n, GRID_W, h, d)[:, row_idx]
    scale = d ** -0.5
    s_loc = jnp.einsum('brqhd,brikhd->brhqik', qg, kg, preferred_element_type=jnp.float32) * scale + bias[None]
    s_loc = jnp.where(col_ok[:, None, :], s_loc, NEG_INF).reshape(bsz, rows_n, h, GRID_W, kh * GRID_W)
    s_ctx = jnp.einsum('brqhd,bchd->brhqc', qg, kc, preferred_element_type=jnp.float32) * scale
    p = jax.nn.softmax(jnp.concatenate([s_loc, s_ctx], axis=-1), axis=-1).astype(v.dtype)
    nloc = kh * GRID_W
    p_loc = p[..., :nloc].reshape(bsz, rows_n, h, GRID_W, kh, GRID_W)
    o = jnp.einsum('brhqik,brikhd->brqhd', p_loc, vg) + jnp.einsum('brhqc,bchd->brqhd', p[..., nloc:], vc)
    return o.reshape(bsz, rows_n * GRID_W, h * d)


def blockwise_dense_attention(q, k, v, kc, vc, scale):
    bsz, seq, h, dq = q.shape
    nb = seq // Q_BLOCK
    kall = jnp.concatenate([kc, k], axis=1)
    vall = jnp.concatenate([vc, v], axis=1)
    qb = jnp.moveaxis(q.reshape(bsz, nb, Q_BLOCK, h, dq), 1, 0)

    def one_block(qblk):
        s = jnp.einsum('bqhd,bkhd->bhqk', qblk, kall, preferred_element_type=jnp.float32) * scale
        p = jax.nn.softmax(s, axis=-1).astype(vall.dtype)
        return jnp.einsum('bhqk,bkhd->bqhd', p, vall)

    o = lax.map(one_block, qb)
    return jnp.moveaxis(o, 0, 1).reshape(bsz, seq, h * v.shape[-1])


def even_mixer(h_lat, h_ctx, row, col, rows_n, w_in, w_out, a_qn, a_kn, a_sink, b_qn, b_kn, b_rpb, need_ctx_out):
    def heads(hh):
        bsz, n, _ = hh.shape
        qa, ka, va, qb, kb, vb = jnp.split(hh @ w_in, EVEN_SPLITS, axis=-1)
        qa = rms_norm(qa.reshape(bsz, n, A_HEADS, HEAD_DIM), a_qn)
        ka = rms_norm(ka.reshape(bsz, n, A_KV_HEADS, HEAD_DIM), a_kn)
        va = va.reshape(bsz, n, A_KV_HEADS, HEAD_DIM)
        qb = rms_norm(qb.reshape(bsz, n, B_HEADS, HEAD_DIM), b_qn)
        kb = rms_norm(kb.reshape(bsz, n, B_HEADS, HEAD_DIM), b_kn)
        vb = vb.reshape(bsz, n, B_HEADS, HEAD_DIM)
        return qa, ka, va, qb, kb, vb

    qa, ka, va, qb, kb, vb = heads(h_lat)
    qa_c, ka_c, va_c, qb_c, kb_c, vb_c = heads(h_ctx)
    qa = rope_2d(qa, row, col)
    ka = rope_2d(ka, row, col)
    o_a = window_attention(qa, ka, va, ka_c, va_c, a_sink)
    o_b = neighbourhood_attention(qb, kb, vb, kb_c, vb_c, b_rpb, rows_n)
    y_lat = jnp.concatenate([o_a, o_b], axis=-1) @ w_out
    y_ctx = None
    if need_ctx_out:
        scale = HEAD_DIM ** -0.5
        oa_c = ctx_self_attention(qa_c, ka_c, va_c, scale, a_sink)
        ob_c = ctx_self_attention(qb_c, kb_c, vb_c, scale)
        y_ctx = jnp.concatenate([oa_c, ob_c], axis=-1) @ w_out
    return y_lat, y_ctx


def mla_project(hh, w_in, qa_norm, kva_norm, w_uq, w_ukv, qn_nope, qn_rope, kn_nope, kn_rope):
    bsz, n, _ = hh.shape
    cq, ckv, kr = jnp.split(hh @ w_in, (C_Q_RANK, C_Q_RANK + C_KV_RANK), axis=-1)
    q = (rms_norm(cq, qa_norm) @ w_uq).reshape(bsz, n, C_HEADS, C_NOPE + C_ROPE)
    kv = (rms_norm(ckv, kva_norm) @ w_ukv).reshape(bsz, n, C_HEADS, C_NOPE + C_V)
    q_nope = rms_norm(q[..., :C_NOPE], qn_nope)
    q_rope = rms_norm(q[..., C_NOPE:], qn_rope)
    k_nope = rms_norm(kv[..., :C_NOPE], kn_nope)
    v = kv[..., C_NOPE:]
    k_rope = rms_norm(kr[:, :, None, :], kn_rope)
    return q_nope, q_rope, k_nope, k_rope, v


def odd_mixer(h_lat, h_ctx, row, col, w_in, qa_norm, kva_norm, w_uq, w_ukv,
              qn_nope, qn_rope, kn_nope, kn_rope, w_out, need_ctx_out):
    q_nope, q_rope, k_nope, k_rope, v = mla_project(h_lat, w_in, qa_norm, kva_norm, w_uq, w_ukv,
                                                    qn_nope, qn_rope, kn_nope, kn_rope)
    q = jnp.concatenate([q_nope, rope_2d(q_rope, row, col)], axis=-1)
    k_rope = rope_2d(k_rope, row, col)
    k = jnp.concatenate([k_nope, jnp.broadcast_to(k_rope, k_nope.shape[:-1] + (C_ROPE,))], axis=-1)
    qc_nope, qc_rope, kc_nope, kc_rope, vc = mla_project(h_ctx, w_in, qa_norm, kva_norm, w_uq, w_ukv,
                                                         qn_nope, qn_rope, kn_nope, kn_rope)
    kc = jnp.concatenate([kc_nope, jnp.broadcast_to(kc_rope, kc_nope.shape[:-1] + (C_ROPE,))], axis=-1)
    scale = (C_NOPE + C_ROPE) ** -0.5
    y_lat = blockwise_dense_attention(q, k, v, kc, vc, scale) @ w_out
    y_ctx = None
    if need_ctx_out:
        qc = jnp.concatenate([qc_nope, qc_rope], axis=-1)
        y_ctx = ctx_self_attention(qc, kc, vc, scale) @ w_out
    return y_lat, y_ctx


def setup_inputs(seed: int = 0) -> dict:
    key = jax.random.key(seed)
    ks = jax.random.split(key, 32)
    f32 = jnp.float32

    def nrm(k, shape, fan_in):
        return jax.random.normal(k, shape, f32) * fan_in ** -0.5

    def gain(k, shape):
        return 1.0 + 0.05 * jax.random.normal(k, shape, f32)

    return {
        'x': jax.random.normal(ks[0], (BATCH, SEQ, D_MODEL), f32),
        'c': jax.random.normal(ks[1], (BATCH, D_MODEL), f32),
        'ctx': jax.random.normal(ks[2], (BATCH, CTX_LEN, D_MODEL), f32),
        'c_ctx': jax.random.normal(ks[3], (D_MODEL,), f32),
        'ada_w': nrm(ks[4], (DEPTH, D_MODEL, 6 * D_MODEL), D_MODEL),
        'ada_b': 0.02 * jax.random.normal(ks[5], (DEPTH, 6 * D_MODEL), f32),
        'norm_mix': gain(ks[6], (DEPTH, D_MODEL)),
        'norm_mlp': gain(ks[7], (DEPTH, D_MODEL)),
        'mlp_w1': nrm(ks[8], (DEPTH, D_MODEL, D_FF), D_MODEL),
        'mlp_w2': nrm(ks[9], (DEPTH, D_FF, D_MODEL), D_FF),
        'e_w_in': nrm(ks[10], (N_EVEN, D_MODEL, EVEN_IN), D_MODEL),
        'e_w_out': nrm(ks[11], (N_EVEN, EVEN_OUT, D_MODEL), EVEN_OUT),
        'a_q_norm': gain(ks[12], (N_EVEN, HEAD_DIM)),
        'a_k_norm': gain(ks[13], (N_EVEN, HEAD_DIM)),
        'a_sink': jax.random.normal(ks[14], (N_EVEN, A_HEADS), f32),
        'b_q_norm': gain(ks[15], (N_EVEN, HEAD_DIM)),
        'b_k_norm': gain(ks[16], (N_EVEN, HEAD_DIM)),
        'b_rpb': 0.1 * jax.random.normal(ks[17], (N_EVEN, B_HEADS, 2 * NA_ROWS - 1, 2 * NA_COLS - 1), f32),
        'o_w_in': nrm(ks[18], (N_ODD, D_MODEL, C_IN), D_MODEL),
        'o_qa_norm': gain(ks[19], (N_ODD, C_Q_RANK)),
        'o_kva_norm': gain(ks[20], (N_ODD, C_KV_RANK)),
        'o_w_uq': nrm(ks[21], (N_ODD, C_Q_RANK, C_HEADS * (C_NOPE + C_ROPE)), C_Q_RANK),
        'o_w_ukv': nrm(ks[22], (N_ODD, C_KV_RANK, C_HEADS * (C_NOPE + C_V)), C_KV_RANK),
        'o_qn_nope': gain(ks[23], (N_ODD, C_NOPE)),
        'o_qn_rope': gain(ks[24], (N_ODD, C_ROPE)),
        'o_kn_nope': gain(ks[25], (N_ODD, C_NOPE)),
        'o_kn_rope': gain(ks[26], (N_ODD, C_ROPE)),
        'o_w_out': nrm(ks[27], (N_ODD, C_OUT, D_MODEL), C_OUT),
    }


def reference(x, c, ctx, c_ctx, ada_w, ada_b, norm_mix, norm_mlp, mlp_w1, mlp_w2,
              e_w_in, e_w_out, a_q_norm, a_k_norm, a_sink, b_q_norm, b_k_norm, b_rpb,
              o_w_in, o_qa_norm, o_kva_norm, o_w_uq, o_w_ukv,
              o_qn_nope, o_qn_rope, o_kn_nope, o_kn_rope, o_w_out):
    seq = x.shape[1]
    rows_n = seq // GRID_W
    t = jnp.arange(seq, dtype=jnp.int32)
    row, col = t // GRID_W, t % GRID_W
    for i in range(DEPTH):
        last = i == DEPTH - 1
        j = i // 2
        sh1, sc1, g1, sh2, sc2, g2 = adaln(c, ada_w[i], ada_b[i])
        csh1, csc1, cg1, csh2, csc2, cg2 = adaln(c_ctx[None, :], ada_w[i], ada_b[i])
        h_lat = modulate(rms_norm(x, norm_mix[i]), sh1, sc1)
        h_ctx = modulate(rms_norm(ctx, norm_mix[i]), csh1, csc1)
        if i % 2 == 0:
            y_lat, y_ctx = even_mixer(h_lat, h_ctx, row, col, rows_n, e_w_in[j], e_w_out[j],
                                      a_q_norm[j], a_k_norm[j], a_sink[j],
                                      b_q_norm[j], b_k_norm[j], b_rpb[j], not last)
        else:
            y_lat, y_ctx = odd_mixer(h_lat, h_ctx, row, col, o_w_in[j], o_qa_norm[j], o_kva_norm[j],
                                     o_w_uq[j], o_w_ukv[j], o_qn_nope[j], o_qn_rope[j],
                                     o_kn_nope[j], o_kn_rope[j], o_w_out[j], not last)
        x = x + g1 * y_lat
        x = x + g2 * squared_relu_mlp(modulate(rms_norm(x, norm_mlp[i]), sh2, sc2), mlp_w1[i], mlp_w2[i])
        if not last:
            ctx = ctx + cg1 * y_ctx
            ctx = ctx + cg2 * squared_relu_mlp(modulate(rms_norm(ctx, norm_mlp[i]), csh2, csc2),
                                               mlp_w1[i], mlp_w2[i])
    return x
```

```python
import functools
import math

import jax
import jax.numpy as jnp
from jax import lax
from jax.experimental import pallas as pl
from jax.experimental.pallas import tpu as pltpu

F32 = jnp.float32
BF16 = jnp.bfloat16

D_MODEL = 1024
CTX_LEN = 256
GRID_W = 64
HEAD_DIM = 64
A_HEADS = 8
A_KV_HEADS = 2
A_WINDOW = 128
A_BLOCK = 128
B_HEADS = 8
NA_ROWS = 8
NA_COLS = 16
C_HEADS = 16
C_Q_RANK = 384
C_KV_RANK = 256
C_NOPE = 64
C_ROPE = 32
D_FF = 4 * D_MODEL
ROPE_THETA = 10000.0
NORM_EPS = 1e-6
NEG_INF = -1e30

LANES = 128
MXU_DIM = 256
TM = CTX_LEN
VMEM_LIMIT = 56 * 1024 * 1024


def _dot(a, b):
    return jnp.dot(a, b, preferred_element_type=F32)


def _dot_nt(a, b):
    return lax.dot_general(a, b, (((1,), (1,)), ((), ())), preferred_element_type=F32)


def _params(*sem):
    return pltpu.CompilerParams(dimension_semantics=sem, vmem_limit_bytes=VMEM_LIMIT)


def _norm_mod(x, g, sh, sc):
    ms = jnp.mean(x * x, axis=-1, keepdims=True)
    return (x * lax.rsqrt(ms + NORM_EPS) * g) * (1.0 + sc) + sh


def _group_rms(y, e, group):
    cw = min(MXU_DIM, y.shape[1])
    out = []
    for c in range(y.shape[1] // cw):
        yc = y[:, c * cw:(c + 1) * cw]
        ss = _dot((yc * yc).astype(BF16), e[:cw, :cw])
        out.append(yc * lax.rsqrt(ss * (1.0 / group) + NORM_EPS))
    return out[0] if len(out) == 1 else jnp.concatenate(out, axis=1)


def _rope(r, cos, sin, half):
    w = r.shape[1]
    reps = w // LANES
    if reps > 1:
        cos = jnp.concatenate([cos] * reps, axis=1)
        sin = jnp.concatenate([sin] * reps, axis=1)
    lane = lax.broadcasted_iota(jnp.int32, r.shape, 1)
    up = pltpu.roll(r, w - half, axis=1)
    dn = pltpu.roll(r, half, axis=1)
    sw = jnp.where((lane & half) == 0, up, dn)
    return r * cos + sw * sin


def _stack_heads(q, lane_lo):
    z = jnp.zeros_like(q)
    return jnp.concatenate([jnp.where(lane_lo, q, z), jnp.where(lane_lo, z, q)], axis=0)


def _softmax_pv(parts, extra=None):
    m = None
    for s, _ in parts:
        mx = jnp.max(s, axis=1, keepdims=True)
        m = mx if m is None else jnp.maximum(m, mx)
    if extra is not None:
        m = jnp.maximum(m, extra)
    den = None
    o = None
    for s, v in parts:
        p = jnp.exp(s - m)
        d = jnp.sum(p, axis=1, keepdims=True)
        pv = _dot(p.astype(BF16), v)
        den = d if den is None else den + d
        o = pv if o is None else o + pv
    if extra is not None:
        den = den + jnp.exp(extra - m)
    return o / den


def _ada_kernel(cond_ref, w_ref, b_ref, o_ref):
    c = cond_ref[...]
    s = (c * jax.nn.sigmoid(c)).astype(BF16)
    o_ref[0] = _dot(s, w_ref[0].astype(BF16)) + b_ref[0]


def _adaln(cond, ada_w, ada_b):
    depth, d, n6 = ada_w.shape
    tn = 1536
    return pl.pallas_call(
        _ada_kernel,
        grid=(depth, n6 // tn),
        in_specs=[
            pl.BlockSpec((8, d), lambda l, j: (0, 0)),
            pl.BlockSpec((1, d, tn), lambda l, j: (l, 0, j)),
            pl.BlockSpec((1, 1, tn), lambda l, j: (l, 0, j)),
        ],
        out_specs=pl.BlockSpec((1, 8, tn), lambda l, j: (l, 0, j)),
        out_shape=jax.ShapeDtypeStruct((depth, 8, n6), F32),
        compiler_params=_params("arbitrary", "arbitrary"),
        name="adaln",
    )(cond, ada_w, ada_b.reshape(depth, 1, n6))


def _mod_spec(chunk, lat_only=False):
    if lat_only:
        return pl.BlockSpec((1, 1, D_MODEL), lambda b, i: (2 * b + 1, 0, chunk))
    return pl.BlockSpec((1, 1, D_MODEL), lambda b, i: (2 * b + jnp.minimum(i, 1), 0, chunk))


def _const_spec(shape):
    nd = len(shape)
    return pl.BlockSpec(shape, lambda *_: (0,) * nd)


EVEN_NORMED = 1792
EVEN_ROPED = 768
EVEN_COLS = 2560


def _even_proj_kernel(x_ref, sh_ref, sc_ref, g_ref, w_ref, gains_ref, e_ref, cos_ref, sin_ref,
                      qa_ref, ka_ref, qb_ref, kb_ref, va_ref, vb_ref):
    h = _norm_mod(x_ref[0], g_ref[...], sh_ref[0], sc_ref[0]).astype(BF16)
    y = _dot(h, w_ref[...])
    yn = _group_rms(y[:, :EVEN_NORMED], e_ref[...], HEAD_DIM) * gains_ref[...]
    r = _rope(yn[:, :EVEN_ROPED], cos_ref[...], sin_ref[...], HEAD_DIM // 4)
    qa_ref[0] = r[:, 0:512].astype(BF16)
    ka_ref[0] = r[:, 512:768].astype(BF16)
    qb_ref[0] = yn[:, 768:1280].astype(BF16)
    kb_ref[0] = yn[:, 1280:1792].astype(BF16)
    va_ref[0] = y[:, 1792:2048].astype(BF16)
    vb_ref[0] = y[:, 2048:2560].astype(BF16)


def _even_proj(xa, mods, g, w, gains, e64, cos, sin):
    b, n, d = xa.shape
    widths = (512, 256, 512, 512, 256, 512)
    return pl.pallas_call(
        _even_proj_kernel,
        grid=(b, n // TM),
        in_specs=[
            pl.BlockSpec((1, TM, d), lambda bb, i: (bb, i, 0)),
            _mod_spec(0), _mod_spec(1),
            _const_spec((1, d)),
            _const_spec((d, EVEN_COLS)),
            _const_spec((1, EVEN_NORMED)),
            _const_spec((MXU_DIM, MXU_DIM)),
            pl.BlockSpec((TM, LANES), lambda bb, i: (i, 0)),
            pl.BlockSpec((TM, LANES), lambda bb, i: (i, 0)),
        ],
        out_specs=[pl.BlockSpec((1, TM, wd), lambda bb, i: (bb, i, 0)) for wd in widths],
        out_shape=[jax.ShapeDtypeStruct((b, n, wd), BF16) for wd in widths],
        compiler_params=_params("arbitrary", "arbitrary"),
        name="even_proj",
    )(xa, mods, mods, g, w, gains, e64, cos, sin)


def _window_kernel(q_ref, k_ref, v_ref, sink_ref, o_ref, *, seq):
    i = pl.program_id(2)
    band = 3 * A_BLOCK
    lane_lo = lax.broadcasted_iota(jnp.int32, (A_BLOCK, LANES), 1) < HEAD_DIM
    sink = sink_ref[0]
    kc = k_ref[0, 0:CTX_LEN, :]
    vc = v_ref[0, 0:CTX_LEN, :]

    def stack(qblk):
        return jnp.concatenate([_stack_heads(qblk[:, :LANES], lane_lo),
                                _stack_heads(qblk[:, LANES:], lane_lo)], axis=0)

    def unstack(o):
        a = jnp.where(lane_lo, o[0:128], o[128:256])
        c = jnp.where(lane_lo, o[256:384], o[384:512])
        return jnp.concatenate([a, c], axis=1)

    @pl.when(i == 0)
    def _():
        for blk in range(TM // A_BLOCK):
            rows = slice(blk * A_BLOCK, (blk + 1) * A_BLOCK)
            qs = stack(q_ref[0, rows, :])
            o = _softmax_pv([(_dot_nt(qs, kc), vc)], sink)
            o_ref[0, rows, :] = unstack(o).astype(o_ref.dtype)

    @pl.when(i > 0)
    def _():
        rel = (lax.broadcasted_iota(jnp.int32, (4 * A_BLOCK, band), 0) % A_BLOCK
               - lax.broadcasted_iota(jnp.int32, (4 * A_BLOCK, band), 1))
        for blk in range(TM // A_BLOCK):
            rows = slice(blk * A_BLOCK, (blk + 1) * A_BLOCK)
            n = (i - 1) * (TM // A_BLOCK) + blk
            start = pl.multiple_of(jnp.clip((n - 1) * A_BLOCK, 0, seq - band), A_BLOCK)
            kl = k_ref[0, pl.ds(CTX_LEN + start, band), :]
            vl = v_ref[0, pl.ds(CTX_LEN + start, band), :]
            qs = stack(q_ref[0, rows, :])
            diff = rel + (n * A_BLOCK - start)
            s_loc = jnp.where(jnp.abs(diff) <= A_WINDOW, _dot_nt(qs, kl), NEG_INF)
            o = _softmax_pv([(s_loc, vl), (_dot_nt(qs, kc), vc)], sink)
            o_ref[0, rows, :] = unstack(o).astype(o_ref.dtype)


def _window_attn(qa, ka, va, sink_col):
    b, n, _ = qa.shape
    return pl.pallas_call(
        functools.partial(_window_kernel, seq=n - CTX_LEN),
        grid=(b, A_KV_HEADS, n // TM),
        in_specs=[
            pl.BlockSpec((1, TM, 2 * LANES), lambda bb, g, i: (bb, i, g)),
            pl.BlockSpec((1, n, LANES), lambda bb, g, i: (bb, 0, g)),
            pl.BlockSpec((1, n, LANES), lambda bb, g, i: (bb, 0, g)),
            pl.BlockSpec((1, 4 * A_BLOCK, 1), lambda bb, g, i: (g, 0, 0)),
        ],
        out_specs=pl.BlockSpec((1, TM, 2 * LANES), lambda bb, g, i: (bb, i, g)),
        out_shape=jax.ShapeDtypeStruct((b, n, A_HEADS * HEAD_DIM), BF16),
        compiler_params=_params("arbitrary", "arbitrary", "arbitrary"),
        name="window_attn",
    )(qa, ka, va, sink_col)


def _na_kernel(q_ref, k_ref, v_ref, t_ref, o_ref, *, rows_n):
    i = pl.program_id(2)
    kc = k_ref[0, 0:CTX_LEN, :]
    vc = v_ref[0, 0:CTX_LEN, :]

    @pl.when(i == 0)
    def _():
        lane_lo = lax.broadcasted_iota(jnp.int32, (TM, LANES), 1) < HEAD_DIM
        qs = _stack_heads(q_ref[0], lane_lo)
        o = _softmax_pv([(_dot_nt(qs, kc), vc)])
        o_ref[0] = jnp.where(lane_lo, o[0:TM], o[TM:2 * TM]).astype(o_ref.dtype)

    @pl.when(i > 0)
    def _():
        lane_lo = lax.broadcasted_iota(jnp.int32, (GRID_W, LANES), 1) < HEAD_DIM
        nloc = NA_ROWS * GRID_W
        for rr in range(TM // GRID_W):
            rows = slice(rr * GRID_W, (rr + 1) * GRID_W)
            r = (i - 1) * (TM // GRID_W) + rr
            r0 = jnp.clip(r - NA_ROWS // 2, 0, rows_n - NA_ROWS)
            d0 = r0 - r + NA_ROWS - 1
            kstart = pl.multiple_of(CTX_LEN + r0 * GRID_W, GRID_W)
            kl = k_ref[0, pl.ds(kstart, nloc), :]
            vl = v_ref[0, pl.ds(kstart, nloc), :]
            qs = _stack_heads(q_ref[0, rows, :], lane_lo)
            s_loc = _dot_nt(qs, kl)
            bias = jnp.concatenate([t_ref[0, d0 + 2 * c] for c in range(nloc // LANES)], axis=1)
            o = _softmax_pv([(s_loc + bias, vl), (_dot_nt(qs, kc), vc)])
            o_ref[0, rows, :] = jnp.where(lane_lo, o[0:GRID_W], o[GRID_W:2 * GRID_W]).astype(o_ref.dtype)


def _na_attn(qb, kb, vb, t2):
    b, n, _ = qb.shape
    rows_n = (n - CTX_LEN) // GRID_W
    assert rows_n >= NA_ROWS
    return pl.pallas_call(
        functools.partial(_na_kernel, rows_n=rows_n),
        grid=(b, B_HEADS // 2, n // TM),
        in_specs=[
            pl.BlockSpec((1, TM, LANES), lambda bb, p, i: (bb, i, p)),
            pl.BlockSpec((1, n, LANES), lambda bb, p, i: (bb, 0, p)),
            pl.BlockSpec((1, n, LANES), lambda bb, p, i: (bb, 0, p)),
            pl.BlockSpec((1, 2 * NA_ROWS - 2, 2 * GRID_W, LANES), lambda bb, p, i: (p, 0, 0, 0)),
        ],
        out_specs=pl.BlockSpec((1, TM, LANES), lambda bb, p, i: (bb, i, p)),
        out_shape=jax.ShapeDtypeStruct((b, n, B_HEADS * HEAD_DIM), BF16),
        compiler_params=_params("arbitrary", "arbitrary", "arbitrary"),
        name="na_attn",
    )(qb, kb, vb, t2)


def _mlp_tail(x, y, g1, sh2, sc2, g2, gn, w1_ref, w2_ref):
    x1 = x + g1 * y
    h = _norm_mod(x1, gn, sh2, sc2).astype(BF16)
    mlp = None
    for j in range(D_FF // D_MODEL):
        cols = slice(j * D_MODEL, (j + 1) * D_MODEL)
        u = jnp.maximum(_dot(h, w1_ref[:, cols]), 0.0)
        part = _dot((u * u).astype(BF16), w2_ref[cols, :])
        mlp = part if mlp is None else mlp + part
    return x1 + g2 * mlp


def _even_out_kernel(x_ref, oa_ref, ob_ref, g1_ref, sh2_ref, sc2_ref, g2_ref, gn_ref,
                     woa_ref, wob_ref, w1_ref, w2_ref, o_ref):
    y = _dot(oa_ref[0], woa_ref[...]) + _dot(ob_ref[0], wob_ref[...])
    o_ref[0] = _mlp_tail(x_ref[0], y, g1_ref[0], sh2_ref[0], sc2_ref[0], g2_ref[0], gn_ref[...],
                         w1_ref, w2_ref)


def _even_out(xa, oa, ob, mods, gn, woa, wob, w1, w2):
    b, n, d = xa.shape
    return pl.pallas_call(
        _even_out_kernel,
        grid=(b, n // TM),
        in_specs=[
            pl.BlockSpec((1, TM, d), lambda bb, i: (bb, i, 0)),
            pl.BlockSpec((1, TM, 512), lambda bb, i: (bb, i, 0)),
            pl.BlockSpec((1, TM, 512), lambda bb, i: (bb, i, 0)),
            _mod_spec(2), _mod_spec(3), _mod_spec(4), _mod_spec(5),
            _const_spec((1, d)),
            _const_spec((512, d)), _const_spec((512, d)),
            _const_spec((d, D_FF)), _const_spec((D_FF, d)),
        ],
        out_specs=pl.BlockSpec((1, TM, d), lambda bb, i: (bb, i, 0)),
        out_shape=jax.ShapeDtypeStruct((b, n, d), F32),
        compiler_params=_params("arbitrary", "arbitrary"),
        name="even_out_mlp",
    )(xa, oa, ob, mods, mods, mods, mods, gn, woa, wob, w1, w2)


def _odd_out_kernel(x_ref, ot_ref, g1_ref, sh2_ref, sc2_ref, g2_ref, gn_ref,
                    wo_ref, w1_ref, w2_ref, o_ref):
    o = ot_ref[0].astype(F32).T.astype(BF16)
    y = _dot(o, wo_ref[...])
    o_ref[0] = _mlp_tail(x_ref[0], y, g1_ref[0], sh2_ref[0], sc2_ref[0], g2_ref[0], gn_ref[...],
                         w1_ref, w2_ref)


def _odd_out(xa, ot, mods, gn, wo, w1, w2):
    b, n, d = xa.shape
    s = n - CTX_LEN
    return pl.pallas_call(
        _odd_out_kernel,
        grid=(b, s // TM),
        in_specs=[
            pl.BlockSpec((1, TM, d), lambda bb, i: (bb, i + 1, 0)),
            pl.BlockSpec((1, d, TM), lambda bb, i: (bb, 0, i)),
            _mod_spec(2, True), _mod_spec(3, True), _mod_spec(4, True), _mod_spec(5, True),
            _const_spec((1, d)),
            _const_spec((d, d)),
            _const_spec((d, D_FF)), _const_spec((D_FF, d)),
        ],
        out_specs=pl.BlockSpec((1, TM, d), lambda bb, i: (bb, i, 0)),
        out_shape=jax.ShapeDtypeStruct((b, s, d), F32),
        compiler_params=_params("arbitrary", "arbitrary"),
        name="odd_out_mlp",
    )(xa, ot, mods, mods, mods, mods, gn, wo, w1, w2)


ODD_IN_COLS = C_Q_RANK + C_KV_RANK + LANES
Q_NOPE_W = C_HEADS * C_NOPE
Q_ROPE_W = C_HEADS * C_ROPE


def _odd_proj_kernel(x_ref, sh_ref, sc_ref, g_ref, win_ref, qag_ref, kvg_ref, wuq_ref, wukv_ref,
                     gq_ref, gk_ref, e64_ref, e32_ref, cos_ref, sin_ref,
                     qn_ref, qr_ref, kn_ref, kr_ref, vt_ref):
    h = _norm_mod(x_ref[0], g_ref[...], sh_ref[0], sc_ref[0]).astype(BF16)
    y = _dot(h, win_ref[...])
    cq = y[:, :C_Q_RANK]
    ckv = y[:, C_Q_RANK:C_Q_RANK + C_KV_RANK]
    kr = y[:, C_Q_RANK + C_KV_RANK:]

    def rms(t, g):
        return (t * lax.rsqrt(jnp.mean(t * t, axis=-1, keepdims=True) + NORM_EPS) * g).astype(BF16)

    q = _dot(rms(cq, qag_ref[...]), wuq_ref[...])
    kv = _dot(rms(ckv, kvg_ref[...]), wukv_ref[...])
    gq = gq_ref[...]
    gk = gk_ref[...]
    cos = cos_ref[...]
    sin = sin_ref[...]
    qn_ref[0] = (_group_rms(q[:, :Q_NOPE_W], e64_ref[...], C_NOPE) * gq[:, :Q_NOPE_W]).astype(BF16)
    qr = _group_rms(q[:, Q_NOPE_W:], e32_ref[...], C_ROPE) * gq[:, Q_NOPE_W:]
    qr_ref[0] = _rope(qr, cos, sin, C_ROPE // 4).astype(BF16)
    kn_ref[0] = (_group_rms(kv[:, :Q_NOPE_W], e64_ref[...], C_NOPE) * gk[:, :Q_NOPE_W]).astype(BF16)
    krn = _group_rms(kr, e32_ref[...], C_ROPE) * gk[:, Q_NOPE_W:]
    kr_ref[0] = _rope(krn, cos, sin, C_ROPE // 4).astype(BF16)
    vt_ref[0] = kv[:, Q_NOPE_W:].T.astype(BF16)


def _odd_proj(xa, mods, g, win, qag, kvg, wuq, wukv, gq, gk, e64, e32, cos, sin):
    b, n, d = xa.shape
    tok = lambda wd: pl.BlockSpec((1, TM, wd), lambda bb, i: (bb, i, 0))
    return pl.pallas_call(
        _odd_proj_kernel,
        grid=(b, n // TM),
        in_specs=[
            tok(d), _mod_spec(0), _mod_spec(1),
            _const_spec((1, d)),
            _const_spec((d, ODD_IN_COLS)),
            _const_spec((1, C_Q_RANK)), _const_spec((1, C_KV_RANK)),
            _const_spec((C_Q_RANK, Q_NOPE_W + Q_ROPE_W)),
            _const_spec((C_KV_RANK, 2 * Q_NOPE_W)),
            _const_spec((1, Q_NOPE_W + Q_ROPE_W)), _const_spec((1, Q_NOPE_W + LANES)),
            _const_spec((MXU_DIM, MXU_DIM)), _const_spec((MXU_DIM, MXU_DIM)),
            pl.BlockSpec((TM, LANES), lambda bb, i: (i, 0)),
            pl.BlockSpec((TM, LANES), lambda bb, i: (i, 0)),
        ],
        out_specs=[tok(Q_NOPE_W), tok(Q_ROPE_W), tok(Q_NOPE_W), tok(LANES),
                   pl.BlockSpec((1, Q_NOPE_W, TM), lambda bb, i: (bb, 0, i))],
        out_shape=[jax.ShapeDtypeStruct((b, n, Q_NOPE_W), BF16),
                   jax.ShapeDtypeStruct((b, n, Q_ROPE_W), BF16),
                   jax.ShapeDtypeStruct((b, n, Q_NOPE_W), BF16),
                   jax.ShapeDtypeStruct((b, n, LANES), BF16),
                   jax.ShapeDtypeStruct((b, Q_NOPE_W, n), BF16)],
        compiler_params=_params("arbitrary", "arbitrary"),
        name="odd_proj",
    )(xa, mods, mods, g, win, qag, kvg, wuq, wukv, gq, gk, e64, e32, cos, sin)


def _kv_tile(n):
    for t in (768, 512, 384, 256):
        if n % t == 0:
            return t
    raise ValueError(f"joint sequence length {n} is not a multiple of 256")


def _flash_kernel(qn_ref, qr_ref, kn_ref, kr_ref, vt_ref, o_ref, *, tk):
    p = pl.program_id(1)
    n = kn_ref.shape[1]
    tq = qn_ref.shape[1]
    lane = lax.broadcasted_iota(jnp.int32, (tq, LANES), 1)
    qn = qn_ref[0]
    qr = qr_ref[0]
    zero = jnp.zeros_like(qn)
    qcat = []
    for hh in range(2):
        off = ((2 * p + hh) % 4) * C_ROPE
        keep_n = (lane < C_NOPE) if hh == 0 else (lane >= C_NOPE)
        keep_r = (lane >= off) & (lane < off + C_ROPE)
        qcat.append(jnp.concatenate([jnp.where(keep_n, qn, zero), jnp.where(keep_r, qr, zero)], axis=1))

    def body(j, carry):
        k0 = pl.multiple_of(j * tk, tk)
        kcat = jnp.concatenate([kn_ref[0, pl.ds(k0, tk), :], kr_ref[0, pl.ds(k0, tk), :]], axis=1)
        new = []
        for hh in range(2):
            m_old, l_old, acc = carry[3 * hh:3 * hh + 3]
            st = _dot_nt(kcat, qcat[hh])
            m_new = jnp.maximum(m_old, jnp.max(st, axis=0, keepdims=True))
            alpha = jnp.exp2(m_old - m_new)
            pt = jnp.exp2(st - m_new)
            l_new = alpha * l_old + jnp.sum(pt, axis=0, keepdims=True)
            vt = vt_ref[0, hh * C_NOPE:(hh + 1) * C_NOPE, pl.ds(k0, tk)]
            acc = alpha * acc + _dot(vt, pt.astype(BF16))
            new += [m_new, l_new, acc]
        return tuple(new)

    init = []
    for _ in range(2):
        init += [jnp.full((1, tq), NEG_INF, F32), jnp.zeros((1, tq), F32), jnp.zeros((C_NOPE, tq), F32)]
    res = lax.fori_loop(0, n // tk, body, tuple(init))
    for hh in range(2):
        _, l_fin, acc = res[3 * hh:3 * hh + 3]
        o_ref[0, hh * C_NOPE:(hh + 1) * C_NOPE, :] = (acc / l_fin).astype(o_ref.dtype)


def _flash_attn(qn, qr, kn, kr, vt):
    b, n, _ = qn.shape
    s = n - CTX_LEN
    tq = TM
    return pl.pallas_call(
        functools.partial(_flash_kernel, tk=_kv_tile(n)),
        grid=(b, C_HEADS // 2, s // tq),
        in_specs=[
            pl.BlockSpec((1, tq, LANES), lambda bb, p, i: (bb, i + 1, p)),
            pl.BlockSpec((1, tq, LANES), lambda bb, p, i: (bb, i + 1, p // 2)),
            pl.BlockSpec((1, n, LANES), lambda bb, p, i: (bb, 0, p)),
            pl.BlockSpec((1, n, LANES), lambda bb, p, i: (bb, 0, 0)),
            pl.BlockSpec((1, LANES, n), lambda bb, p, i: (bb, p, 0)),
        ],
        out_specs=pl.BlockSpec((1, LANES, tq), lambda bb, p, i: (bb, p, i)),
        out_shape=jax.ShapeDtypeStruct((b, C_HEADS * C_NOPE, s), BF16),
        compiler_params=_params("arbitrary", "arbitrary", "arbitrary"),
        name="mla_flash",
    )(qn, qr, kn, kr, vt)


def _rope_tables(seq, dim):
    t = jnp.arange(seq, dtype=jnp.int32)
    pos = jnp.stack([t // GRID_W, t % GRID_W], axis=0).astype(F32)
    half = dim // 2
    q = half // 2
    inv = ROPE_THETA ** (-jnp.arange(q, dtype=F32) / q)
    j = jnp.arange(dim)
    ang = pos[j // half].T * inv[j % q][None, :]
    sign = jnp.where((j % half) < q, -1.0, 1.0).astype(F32)
    cos = jnp.cos(ang)
    sin = jnp.sin(ang) * sign[None, :]
    reps = LANES // dim
    cos = jnp.tile(cos, (1, reps))
    sin = jnp.tile(sin, (1, reps))
    cos = jnp.concatenate([jnp.ones((CTX_LEN, LANES), F32), cos], axis=0)
    sin = jnp.concatenate([jnp.zeros((CTX_LEN, LANES), F32), sin], axis=0)
    return cos, sin


def _group_ones(group):
    r = jnp.arange(MXU_DIM) // group
    return (r[:, None] == r[None, :]).astype(BF16)


def _na_bias_table(rpb):
    cq = jnp.arange(GRID_W)
    c0 = jnp.clip(cq - NA_COLS // 2, 0, GRID_W - NA_COLS)
    col_ok = (cq[None, :] >= c0[:, None]) & (cq[None, :] < c0[:, None] + NA_COLS)
    dci = jnp.clip(cq[None, :] - cq[:, None], 1 - NA_COLS, NA_COLS - 1) + NA_COLS - 1
    tm = jnp.where(col_ok[None, None], rpb.astype(F32)[:, :, dci], NEG_INF)
    nd = 2 * NA_ROWS - 2
    t2 = jnp.concatenate([tm[:, :nd], tm[:, 1:nd + 1]], axis=-1)
    t2 = t2.reshape(B_HEADS // 2, 2, nd, GRID_W, LANES).transpose(0, 2, 1, 3, 4)
    return t2.reshape(B_HEADS // 2, nd, 2 * GRID_W, LANES)


def kernel(x, c, ctx, c_ctx, ada_w, ada_b, norm_mix, norm_mlp, mlp_w1, mlp_w2, e_w_in, e_w_out, a_q_norm, a_k_norm, a_sink, b_q_norm, b_k_norm, b_rpb, o_w_in, o_qa_norm, o_kva_norm, o_w_uq, o_w_ukv, o_qn_nope, o_qn_rope, o_kn_nope, o_kn_rope, o_w_out):
    bsz, seq, d = x.shape
    assert d == D_MODEL and ctx.shape[1] == CTX_LEN and seq % TM == 0 and ada_w.shape[0] == 2
    assert bsz + 1 <= 8

    cond = jnp.zeros((8, d), F32).at[:bsz].set(c).at[bsz].set(c_ctx)
    m = _adaln(cond, ada_w, ada_b)
    mods = [jnp.stack([jnp.broadcast_to(m[i, bsz], (bsz, 6 * d)), m[i, :bsz]], axis=1).reshape(2 * bsz, 1, 6 * d)
            for i in range(2)]

    xa = jnp.concatenate([ctx, x], axis=1)
    e64 = _group_ones(HEAD_DIM)
    e32 = _group_ones(C_ROPE)

    w = e_w_in[0]
    dup = lambda t: jnp.concatenate([t[:, :64], t[:, :64], t[:, 64:], t[:, 64:]], axis=1)
    w_ext = jnp.concatenate([w[:, 0:512], dup(w[:, 512:640]), w[:, 768:1280], w[:, 1280:1792],
                             dup(w[:, 640:768]), w[:, 1792:2304]], axis=1).astype(BF16)
    scale = HEAD_DIM ** -0.5
    gains = jnp.concatenate([jnp.tile(a_q_norm[0], 8) * scale, jnp.tile(a_k_norm[0], 4),
                             jnp.tile(b_q_norm[0], 8) * scale, jnp.tile(b_k_norm[0], 8)])[None, :]
    cos64, sin64 = _rope_tables(seq, HEAD_DIM)
    qa, ka, qb, kb, va, vb = _even_proj(xa, mods[0], norm_mix[0][None, :], w_ext, gains, e64, cos64, sin64)
    sink_col = jnp.repeat(a_sink[0].reshape(A_KV_HEADS, 4), A_BLOCK, axis=1).reshape(A_KV_HEADS, 4 * A_BLOCK, 1)
    oa = _window_attn(qa, ka, va, sink_col)
    ob = _na_attn(qb, kb, vb, _na_bias_table(b_rpb[0]))
    wo = e_w_out[0].astype(BF16)
    xa = _even_out(xa, oa, ob, mods[0], norm_mlp[0][None, :], wo[:512], wo[512:],
                   mlp_w1[0].astype(BF16), mlp_w2[0].astype(BF16))

    wi = o_w_in[0]
    win = jnp.concatenate([wi[:, :C_Q_RANK + C_KV_RANK]] + [wi[:, C_Q_RANK + C_KV_RANK:]] * 4, axis=1).astype(BF16)
    wuq = o_w_uq[0].reshape(C_Q_RANK, C_HEADS, C_NOPE + C_ROPE)
    wuq = jnp.concatenate([wuq[:, :, :C_NOPE].reshape(C_Q_RANK, -1), wuq[:, :, C_NOPE:].reshape(C_Q_RANK, -1)],
                          axis=1).astype(BF16)
    wukv = o_w_ukv[0].reshape(C_KV_RANK, C_HEADS, 2 * C_NOPE)
    wukv = jnp.concatenate([wukv[:, :, :C_NOPE].reshape(C_KV_RANK, -1), wukv[:, :, C_NOPE:].reshape(C_KV_RANK, -1)],
                           axis=1).astype(BF16)
    qscale = (C_NOPE + C_ROPE) ** -0.5 * math.log2(math.e)
    gq = (jnp.concatenate([jnp.tile(o_qn_nope[0], C_HEADS), jnp.tile(o_qn_rope[0], C_HEADS)]) * qscale)[None, :]
    gk = jnp.concatenate([jnp.tile(o_kn_nope[0], C_HEADS), jnp.tile(o_kn_rope[0], LANES // C_ROPE)])[None, :]
    cos32, sin32 = _rope_tables(seq, C_ROPE)
    qn, qr, kn, kr, vt = _odd_proj(xa, mods[1], norm_mix[1][None, :], win, o_qa_norm[0][None, :],
                                   o_kva_norm[0][None, :], wuq, wukv, gq, gk, e64, e32, cos32, sin32)
    ot = _flash_attn(qn, qr, kn, kr, vt)
    return _odd_out(xa, ot, mods[1], norm_mlp[1][None, :], o_w_out[0].astype(BF16),
                    mlp_w1[1].astype(BF16), mlp_w2[1].astype(BF16))
```

```python
import functools
import math

import jax
import jax.numpy as jnp
from jax import lax
from jax.experimental import pallas as pl
from jax.experimental.pallas import tpu as pltpu

F32 = jnp.float32
BF16 = jnp.bfloat16

D_MODEL = 1024
CTX_LEN = 256
GRID_W = 64
HEAD_DIM = 64
A_HEADS = 8
A_KV_HEADS = 2
A_WINDOW = 128
A_BLOCK = 128
B_HEADS = 8
NA_ROWS = 8
NA_COLS = 16
C_HEADS = 16
C_Q_RANK = 384
C_KV_RANK = 256
C_NOPE = 64
C_ROPE = 32
D_FF = 4 * D_MODEL
ROPE_THETA = 10000.0
NORM_EPS = 1e-6
NEG_INF = -1e30

LANES = 128
MXU_DIM = 256
TM = CTX_LEN
VMEM_LIMIT = 56 * 1024 * 1024


def _dot(a, b):
    return jnp.dot(a, b, preferred_element_type=F32)


def _dot_nt(a, b):
    return lax.dot_general(a, b, (((1,), (1,)), ((), ())), preferred_element_type=F32)


def _params(*sem):
    return pltpu.CompilerParams(dimension_semantics=sem, vmem_limit_bytes=VMEM_LIMIT)


def _norm_mod(x, g, sh, sc):
    ms = jnp.mean(x * x, axis=-1, keepdims=True)
    return (x * lax.rsqrt(ms + NORM_EPS) * g) * (1.0 + sc) + sh


def _group_rms(y, e, group):
    cw = min(MXU_DIM, y.shape[1])
    out = []
    for c in range(y.shape[1] // cw):
        yc = y[:, c * cw:(c + 1) * cw]
        ss = _dot((yc * yc).astype(BF16), e[:cw, :cw])
        out.append(yc * lax.rsqrt(ss * (1.0 / group) + NORM_EPS))
    return out[0] if len(out) == 1 else jnp.concatenate(out, axis=1)


def _rope(r, cos, sin, half):
    w = r.shape[1]
    reps = w // LANES
    if reps > 1:
        cos = jnp.concatenate([cos] * reps, axis=1)
        sin = jnp.concatenate([sin] * reps, axis=1)
    lane = lax.broadcasted_iota(jnp.int32, r.shape, 1)
    up = pltpu.roll(r, w - half, axis=1)
    dn = pltpu.roll(r, half, axis=1)
    sw = jnp.where((lane & half) == 0, up, dn)
    return r * cos + sw * sin


def _stack_heads(q, lane_lo):
    z = jnp.zeros_like(q)
    return jnp.concatenate([jnp.where(lane_lo, q, z), jnp.where(lane_lo, z, q)], axis=0)


def _softmax_pv(parts, extra=None):
    m = None
    for s, _ in parts:
        mx = jnp.max(s, axis=1, keepdims=True)
        m = mx if m is None else jnp.maximum(m, mx)
    if extra is not None:
        m = jnp.maximum(m, extra)
    den = None
    o = None
    for s, v in parts:
        p = jnp.exp(s - m)
        d = jnp.sum(p, axis=1, keepdims=True)
        pv = _dot(p.astype(BF16), v)
        den = d if den is None else den + d
        o = pv if o is None else o + pv
    if extra is not None:
        den = den + jnp.exp(extra - m)
    return o / den


def _ada_kernel(cond_ref, w_ref, b_ref, o_ref):
    c = cond_ref[...]
    s = (c * jax.nn.sigmoid(c)).astype(BF16)
    o_ref[0] = _dot(s, w_ref[0].astype(BF16)) + b_ref[0]


def _adaln(cond, ada_w, ada_b):
    depth, d, n6 = ada_w.shape
    tn = 1536
    return pl.pallas_call(
        _ada_kernel,
        grid=(depth, n6 // tn),
        in_specs=[
            pl.BlockSpec((8, d), lambda l, j: (0, 0)),
            pl.BlockSpec((1, d, tn), lambda l, j: (l, 0, j)),
            pl.BlockSpec((1, 1, tn), lambda l, j: (l, 0, j)),
        ],
        out_specs=pl.BlockSpec((1, 8, tn), lambda l, j: (l, 0, j)),
        out_shape=jax.ShapeDtypeStruct((depth, 8, n6), F32),
        compiler_params=_params("arbitrary", "arbitrary"),
        name="adaln",
    )(cond, ada_w, ada_b.reshape(depth, 1, n6))


def _mod_spec(chunk, lat_only=False):
    if lat_only:
        return pl.BlockSpec((1, 1, D_MODEL), lambda b, i: (2 * b + 1, 0, chunk))
    return pl.BlockSpec((1, 1, D_MODEL), lambda b, i: (2 * b + jnp.minimum(i, 1), 0, chunk))


def _const_spec(shape):
    nd = len(shape)
    return pl.BlockSpec(shape, lambda *_: (0,) * nd)


EVEN_NORMED = 1792
EVEN_ROPED = 768
EVEN_COLS = 2560


def _even_proj_kernel(x_ref, sh_ref, sc_ref, g_ref, w_ref, gains_ref, e_ref, cos_ref, sin_ref,
                      qa_ref, ka_ref, qb_ref, kb_ref, va_ref, vb_ref):
    h = _norm_mod(x_ref[0], g_ref[...], sh_ref[0], sc_ref[0]).astype(BF16)
    y = _dot(h, w_ref[...])
    yn = _group_rms(y[:, :EVEN_NORMED], e_ref[...], HEAD_DIM) * gains_ref[...]
    r = _rope(yn[:, :EVEN_ROPED], cos_ref[...], sin_ref[...], HEAD_DIM // 4)
    qa_ref[0] = r[:, 0:512].astype(BF16)
    ka_ref[0] = r[:, 512:768].astype(BF16)
    qb_ref[0] = yn[:, 768:1280].astype(BF16)
    kb_ref[0] = yn[:, 1280:1792].astype(BF16)
    va_ref[0] = y[:, 1792:2048].astype(BF16)
    vb_ref[0] = y[:, 2048:2560].astype(BF16)


def _even_proj(xa, mods, g, w, gains, e64, cos, sin):
    b, n, d = xa.shape
    widths = (512, 256, 512, 512, 256, 512)
    return pl.pallas_call(
        _even_proj_kernel,
        grid=(b, n // TM),
        in_specs=[
            pl.BlockSpec((1, TM, d), lambda bb, i: (bb, i, 0)),
            _mod_spec(0), _mod_spec(1),
            _const_spec((1, d)),
            _const_spec((d, EVEN_COLS)),
            _const_spec((1, EVEN_NORMED)),
            _const_spec((MXU_DIM, MXU_DIM)),
            pl.BlockSpec((TM, LANES), lambda bb, i: (i, 0)),
            pl.BlockSpec((TM, LANES), lambda bb, i: (i, 0)),
        ],
        out_specs=[pl.BlockSpec((1, TM, wd), lambda bb, i: (bb, i, 0)) for wd in widths],
        out_shape=[jax.ShapeDtypeStruct((b, n, wd), BF16) for wd in widths],
        compiler_params=_params("arbitrary", "arbitrary"),
        name="even_proj",
    )(xa, mods, mods, g, w, gains, e64, cos, sin)


def _window_kernel(q_ref, k_ref, v_ref, sink_ref, o_ref, *, seq):
    i = pl.program_id(2)
    band = 3 * A_BLOCK
    lane_lo = lax.broadcasted_iota(jnp.int32, (A_BLOCK, LANES), 1) < HEAD_DIM
    sink = sink_ref[0]
    kc = k_ref[0, 0:CTX_LEN, :]
    vc = v_ref[0, 0:CTX_LEN, :]

    def stack(qblk):
        return jnp.concatenate([_stack_heads(qblk[:, :LANES], lane_lo),
                                _stack_heads(qblk[:, LANES:], lane_lo)], axis=0)

    def unstack(o):
        a = jnp.where(lane_lo, o[0:128], o[128:256])
        c = jnp.where(lane_lo, o[256:384], o[384:512])
        return jnp.concatenate([a, c], axis=1)

    @pl.when(i == 0)
    def _():
        for blk in range(TM // A_BLOCK):
            rows = slice(blk * A_BLOCK, (blk + 1) * A_BLOCK)
            qs = stack(q_ref[0, rows, :])
            o = _softmax_pv([(_dot_nt(qs, kc), vc)], sink)
            o_ref[0, rows, :] = unstack(o).astype(o_ref.dtype)

    @pl.when(i > 0)
    def _():
        rel = (lax.broadcasted_iota(jnp.int32, (4 * A_BLOCK, band), 0) % A_BLOCK
               - lax.broadcasted_iota(jnp.int32, (4 * A_BLOCK, band), 1))
        for blk in range(TM // A_BLOCK):
            rows = slice(blk * A_BLOCK, (blk + 1) * A_BLOCK)
            n = (i - 1) * (TM // A_BLOCK) + blk
            start = pl.multiple_of(jnp.clip((n - 1) * A_BLOCK, 0, seq - band), A_BLOCK)
            kl = k_ref[0, pl.ds(CTX_LEN + start, band), :]
            vl = v_ref[0, pl.ds(CTX_LEN + start, band), :]
            qs = stack(q_ref[0, rows, :])
            diff = rel + (n * A_BLOCK - start)
            s_loc = jnp.where(jnp.abs(diff) <= A_WINDOW, _dot_nt(qs, kl), NEG_INF)
            o = _softmax_pv([(s_loc, vl), (_dot_nt(qs, kc), vc)], sink)
            o_ref[0, rows, :] = unstack(o).astype(o_ref.dtype)


def _window_attn(qa, ka, va, sink_col):
    b, n, _ = qa.shape
    return pl.pallas_call(
        functools.partial(_window_kernel, seq=n - CTX_LEN),
        grid=(b, A_KV_HEADS, n // TM),
        in_specs=[
            pl.BlockSpec((1, TM, 2 * LANES), lambda bb, g, i: (bb, i, g)),
            pl.BlockSpec((1, n, LANES), lambda bb, g, i: (bb, 0, g)),
            pl.BlockSpec((1, n, LANES), lambda bb, g, i: (bb, 0, g)),
            pl.BlockSpec((1, 4 * A_BLOCK, 1), lambda bb, g, i: (g, 0, 0)),
        ],
        out_specs=pl.BlockSpec((1, TM, 2 * LANES), lambda bb, g, i: (bb, i, g)),
        out_shape=jax.ShapeDtypeStruct((b, n, A_HEADS * HEAD_DIM), BF16),
        compiler_params=_params("arbitrary", "arbitrary", "arbitrary"),
        name="window_attn",
    )(qa, ka, va, sink_col)


def _na_kernel(q_ref, k_ref, v_ref, t_ref, o_ref, *, rows_n):
    i = pl.program_id(2)
    kc = k_ref[0, 0:CTX_LEN, :]
    vc = v_ref[0, 0:CTX_LEN, :]

    @pl.when(i == 0)
    def _():
        lane_lo = lax.broadcasted_iota(jnp.int32, (TM, LANES), 1) < HEAD_DIM
        qs = _stack_heads(q_ref[0], lane_lo)
        o = _softmax_pv([(_dot_nt(qs, kc), vc)])
        o_ref[0] = jnp.where(lane_lo, o[0:TM], o[TM:2 * TM]).astype(o_ref.dtype)

    @pl.when(i > 0)
    def _():
        lane_lo = lax.broadcasted_iota(jnp.int32, (GRID_W, LANES), 1) < HEAD_DIM
        nloc = NA_ROWS * GRID_W
        for rr in range(TM // GRID_W):
            rows = slice(rr * GRID_W, (rr + 1) * GRID_W)
            r = (i - 1) * (TM // GRID_W) + rr
            r0 = jnp.clip(r - NA_ROWS // 2, 0, rows_n - NA_ROWS)
            d0 = r0 - r + NA_ROWS - 1
            kstart = pl.multiple_of(CTX_LEN + r0 * GRID_W, GRID_W)
            kl = k_ref[0, pl.ds(kstart, nloc), :]
            vl = v_ref[0, pl.ds(kstart, nloc), :]
            qs = _stack_heads(q_ref[0, rows, :], lane_lo)
            s_loc = _dot_nt(qs, kl)
            bias = jnp.concatenate([t_ref[0, d0 + 2 * c] for c in range(nloc // LANES)], axis=1)
            o = _softmax_pv([(s_loc + bias, vl), (_dot_nt(qs, kc), vc)])
            o_ref[0, rows, :] = jnp.where(lane_lo, o[0:GRID_W], o[GRID_W:2 * GRID_W]).astype(o_ref.dtype)


def _na_attn(qb, kb, vb, t2):
    b, n, _ = qb.shape
    rows_n = (n - CTX_LEN) // GRID_W
    assert rows_n >= NA_ROWS
    return pl.pallas_call(
        functools.partial(_na_kernel, rows_n=rows_n),
        grid=(b, B_HEADS // 2, n // TM),
        in_specs=[
            pl.BlockSpec((1, TM, LANES), lambda bb, p, i: (bb, i, p)),
            pl.BlockSpec((1, n, LANES), lambda bb, p, i: (bb, 0, p)),
            pl.BlockSpec((1, n, LANES), lambda bb, p, i: (bb, 0, p)),
            pl.BlockSpec((1, 2 * NA_ROWS - 2, 2 * GRID_W, LANES), lambda bb, p, i: (p, 0, 0, 0)),
        ],
        out_specs=pl.BlockSpec((1, TM, LANES), lambda bb, p, i: (bb, i, p)),
        out_shape=jax.ShapeDtypeStruct((b, n, B_HEADS * HEAD_DIM), BF16),
        compiler_params=_params("arbitrary", "arbitrary", "arbitrary"),
        name="na_attn",
    )(qb, kb, vb, t2)


def _mlp_tail(x, y, g1, sh2, sc2, g2, gn, w1_ref, w2_ref):
    x1 = x + g1 * y
    h = _norm_mod(x1, gn, sh2, sc2).astype(BF16)
    mlp = None
    for j in range(D_FF // D_MODEL):
        cols = slice(j * D_MODEL, (j + 1) * D_MODEL)
        u = jnp.maximum(_dot(h, w1_ref[:, cols]), 0.0)
        part = _dot((u * u).astype(BF16), w2_ref[cols, :])
        mlp = part if mlp is None else mlp + part
    return x1 + g2 * mlp


def _even_out_kernel(x_ref, oa_ref, ob_ref, g1_ref, sh2_ref, sc2_ref, g2_ref, gn_ref,
                     woa_ref, wob_ref, w1_ref, w2_ref, o_ref):
    y = _dot(oa_ref[0], woa_ref[...]) + _dot(ob_ref[0], wob_ref[...])
    o_ref[0] = _mlp_tail(x_ref[0], y, g1_ref[0], sh2_ref[0], sc2_ref[0], g2_ref[0], gn_ref[...],
                         w1_ref, w2_ref)


def _even_out(xa, oa, ob, mods, gn, woa, wob, w1, w2):
    b, n, d = xa.shape
    return pl.pallas_call(
        _even_out_kernel,
        grid=(b, n // TM),
        in_specs=[
            pl.BlockSpec((1, TM, d), lambda bb, i: (bb, i, 0)),
            pl.BlockSpec((1, TM, 512), lambda bb, i: (bb, i, 0)),
            pl.BlockSpec((1, TM, 512), lambda bb, i: (bb, i, 0)),
            _mod_spec(2), _mod_spec(3), _mod_spec(4), _mod_spec(5),
            _const_spec((1, d)),
            _const_spec((512, d)), _const_spec((512, d)),
            _const_spec((d, D_FF)), _const_spec((D_FF, d)),
        ],
        out_specs=pl.BlockSpec((1, TM, d), lambda bb, i: (bb, i, 0)),
        out_shape=jax.ShapeDtypeStruct((b, n, d), F32),
        compiler_params=_params("arbitrary", "arbitrary"),
        name="even_out_mlp",
    )(xa, oa, ob, mods, mods, mods, mods, gn, woa, wob, w1, w2)


def _odd_out_kernel(x_ref, ot_ref, g1_ref, sh2_ref, sc2_ref, g2_ref, gn_ref,
                    wo_ref, w1_ref, w2_ref, o_ref):
    o = ot_ref[0].astype(F32).T.astype(BF16)
    y = _dot(o, wo_ref[...])
    o_ref[0] = _mlp_tail(x_ref[0], y, g1_ref[0], sh2_ref[0], sc2_ref[0], g2_ref[0], gn_ref[...],
                         w1_ref, w2_ref)


def _odd_out(xa, ot, mods, gn, wo, w1, w2):
    b, n, d = xa.shape
    s = n - CTX_LEN
    return pl.pallas_call(
        _odd_out_kernel,
        grid=(b, s // TM),
        in_specs=[
            pl.BlockSpec((1, TM, d), lambda bb, i: (bb, i + 1, 0)),
            pl.BlockSpec((1, d, TM), lambda bb, i: (bb, 0, i)),
            _mod_spec(2, True), _mod_spec(3, True), _mod_spec(4, True), _mod_spec(5, True),
            _const_spec((1, d)),
            _const_spec((d, d)),
            _const_spec((d, D_FF)), _const_spec((D_FF, d)),
        ],
        out_specs=pl.BlockSpec((1, TM, d), lambda bb, i: (bb, i, 0)),
        out_shape=jax.ShapeDtypeStruct((b, s, d), F32),
        compiler_params=_params("arbitrary", "arbitrary"),
        name="odd_out_mlp",
    )(xa, ot, mods, mods, mods, mods, gn, wo, w1, w2)


ODD_IN_COLS = C_Q_RANK + C_KV_RANK + LANES
Q_NOPE_W = C_HEADS * C_NOPE
Q_ROPE_W = C_HEADS * C_ROPE


def _odd_proj_kernel(x_ref, sh_ref, sc_ref, g_ref, win_ref, qag_ref, kvg_ref, wuq_ref, wukv_ref,
                     gq_ref, gk_ref, e64_ref, e32_ref, cos_ref, sin_ref,
                     qn_ref, qr_ref, kn_ref, kr_ref, vt_ref):
    h = _norm_mod(x_ref[0], g_ref[...], sh_ref[0], sc_ref[0]).astype(BF16)
    y = _dot(h, win_ref[...])
    cq = y[:, :C_Q_RANK]
    ckv = y[:, C_Q_RANK:C_Q_RANK + C_KV_RANK]
    kr = y[:, C_Q_RANK + C_KV_RANK:]

    def rms(t, g):
        return (t * lax.rsqrt(jnp.mean(t * t, axis=-1, keepdims=True) + NORM_EPS) * g).astype(BF16)

    q = _dot(rms(cq, qag_ref[...]), wuq_ref[...])
    kv = _dot(rms(ckv, kvg_ref[...]), wukv_ref[...])
    gq = gq_ref[...]
    gk = gk_ref[...]
    cos = cos_ref[...]
    sin = sin_ref[...]
    qn = _group_rms(q[:, :Q_NOPE_W], e64_ref[...], C_NOPE) * gq[:, :Q_NOPE_W]
    qn_ref[0] = qn.T.astype(BF16)
    qr = _group_rms(q[:, Q_NOPE_W:], e32_ref[...], C_ROPE) * gq[:, Q_NOPE_W:]
    qr_ref[0] = _rope(qr, cos, sin, C_ROPE // 4).T.astype(BF16)
    kn_ref[0] = (_group_rms(kv[:, :Q_NOPE_W], e64_ref[...], C_NOPE) * gk[:, :Q_NOPE_W]).astype(BF16)
    krn = _group_rms(kr, e32_ref[...], C_ROPE) * gk[:, Q_NOPE_W:]
    kr_ref[0] = _rope(krn, cos, sin, C_ROPE // 4).astype(BF16)
    vt_ref[0] = kv[:, Q_NOPE_W:].T.astype(BF16)


def _odd_proj(xa, mods, g, win, qag, kvg, wuq, wukv, gq, gk, e64, e32, cos, sin):
    b, n, d = xa.shape
    tok = lambda wd: pl.BlockSpec((1, TM, wd), lambda bb, i: (bb, i, 0))
    return pl.pallas_call(
        _odd_proj_kernel,
        grid=(b, n // TM),
        in_specs=[
            tok(d), _mod_spec(0), _mod_spec(1),
            _const_spec((1, d)),
            _const_spec((d, ODD_IN_COLS)),
            _const_spec((1, C_Q_RANK)), _const_spec((1, C_KV_RANK)),
            _const_spec((C_Q_RANK, Q_NOPE_W + Q_ROPE_W)),
            _const_spec((C_KV_RANK, 2 * Q_NOPE_W)),
            _const_spec((1, Q_NOPE_W + Q_ROPE_W)), _const_spec((1, Q_NOPE_W + LANES)),
            _const_spec((MXU_DIM, MXU_DIM)), _const_spec((MXU_DIM, MXU_DIM)),
            pl.BlockSpec((TM, LANES), lambda bb, i: (i, 0)),
            pl.BlockSpec((TM, LANES), lambda bb, i: (i, 0)),
        ],
        out_specs=[pl.BlockSpec((1, Q_NOPE_W, TM), lambda bb, i: (bb, 0, i)),
                   pl.BlockSpec((1, Q_ROPE_W, TM), lambda bb, i: (bb, 0, i)),
                   tok(Q_NOPE_W), tok(LANES),
                   pl.BlockSpec((1, Q_NOPE_W, TM), lambda bb, i: (bb, 0, i))],
        out_shape=[jax.ShapeDtypeStruct((b, Q_NOPE_W, n), BF16),
                   jax.ShapeDtypeStruct((b, Q_ROPE_W, n), BF16),
                   jax.ShapeDtypeStruct((b, n, Q_NOPE_W), BF16),
                   jax.ShapeDtypeStruct((b, n, LANES), BF16),
                   jax.ShapeDtypeStruct((b, Q_NOPE_W, n), BF16)],
        compiler_params=_params("arbitrary", "arbitrary"),
        name="odd_proj",
    )(xa, mods, mods, g, win, qag, kvg, wuq, wukv, gq, gk, e64, e32, cos, sin)


SUM_ROWS = 16


def _kv_tile(n):
    for t in (768, 256):
        if n % t == 0 and (n // t) % 2 == 1:
            return t
    raise ValueError(f"joint sequence length {n} has no odd split into 256-multiples")


def _flash_kernel(qnt_ref, qrt_ref, kn_ref, kr_ref, vt_ref, o_ref,
                  s_ref, mx_ref, m_ref, acc_ref, qt_ref, *, tk, tq):
    p = pl.program_id(1)
    n = kn_ref.shape[1]
    nq = (n - CTX_LEN) // tq
    nch = n // tk
    assert nq % 2 == 0 and nch % 2 == 1
    ones = jnp.ones((SUM_ROWS, tk), BF16)
    row = lax.broadcasted_iota(jnp.int32, (LANES, tq), 0)

    def prep_q(qi, qslot):
        c0 = pl.multiple_of(CTX_LEN + qi * tq, tq)
        qn = qnt_ref[0, :, pl.ds(c0, tq)]
        qr = qrt_ref[0, :, pl.ds(c0, tq)]
        zero = jnp.zeros_like(qn)
        for hh in range(2):
            off = ((2 * p + hh) % 4) * C_ROPE
            keep_n = (row < C_NOPE) if hh == 0 else (row >= C_NOPE)
            keep_r = (row >= off) & (row < off + C_ROPE)
            qt_ref[qslot, 0:LANES, hh * tq:(hh + 1) * tq] = jnp.where(keep_n, qn, zero)
            qt_ref[qslot, LANES:, hh * tq:(hh + 1) * tq] = jnp.where(keep_r, qr, zero)

    def scores(qslot, j, sslot):
        k0 = pl.multiple_of(j * tk, tk)
        kcat = jnp.concatenate([kn_ref[0, pl.ds(k0, tk), :], kr_ref[0, pl.ds(k0, tk), :]], axis=1)
        st = _dot(kcat, qt_ref[qslot])
        s_ref[sslot] = st
        mx_ref[sslot] = jnp.max(st, axis=0, keepdims=True)

    def update(j, sslot):
        k0 = pl.multiple_of(j * tk, tk)
        m_old = m_ref[...]
        m_new = jnp.maximum(m_old, mx_ref[sslot])
        alpha = jnp.exp2(m_old - m_new)
        pt = jnp.exp2(s_ref[sslot] - m_new).astype(BF16)
        va = jnp.concatenate([vt_ref[0, :, pl.ds(k0, tk)], ones], axis=0)
        m_ref[...] = m_new
        acc_ref[...] = alpha * acc_ref[...] + _dot(va, pt)

    def reset():
        m_ref[...] = jnp.full(m_ref.shape, NEG_INF, F32)
        acc_ref[...] = jnp.zeros(acc_ref.shape, F32)

    def finish(qi):
        acc = acc_ref[...]
        c0 = pl.multiple_of(qi * tq, tq)
        o_ref[0, 0:C_NOPE, pl.ds(c0, tq)] = (acc[0:C_NOPE, :tq] / acc[LANES:LANES + 1, :tq]).astype(o_ref.dtype)
        o_ref[0, C_NOPE:, pl.ds(c0, tq)] = (acc[C_NOPE:LANES, tq:] / acc[LANES:LANES + 1, tq:]).astype(o_ref.dtype)
        reset()

    def run_tile(qi, slot, next_qi):
        other = 1 - slot

        def pair(jj, carry):
            scores(slot, 2 * jj + 1, other)
            update(2 * jj, slot)
            scores(slot, 2 * jj + 2, slot)
            update(2 * jj + 1, other)
            return carry

        lax.fori_loop(0, nch // 2, pair, 0)
        prep_q(next_qi, other)
        scores(other, 0, other)
        update(nch - 1, slot)
        finish(qi)

    def two_tiles(t2, carry):
        run_tile(2 * t2, 0, 2 * t2 + 1)
        run_tile(2 * t2 + 1, 1, jnp.minimum(2 * t2 + 2, nq - 1))
        return carry

    reset()
    prep_q(0, 0)
    scores(0, 0, 0)
    lax.fori_loop(0, nq // 2, two_tiles, 0)


def _flash_attn(qnt, qrt, kn, kr, vt):
    b, n, _ = kn.shape
    s = n - CTX_LEN
    tq = TM
    tk = _kv_tile(n)
    return pl.pallas_call(
        functools.partial(_flash_kernel, tk=tk, tq=tq),
        grid=(b, C_HEADS // 2),
        scratch_shapes=[pltpu.VMEM((2, tk, 2 * tq), F32),
                        pltpu.VMEM((2, 1, 2 * tq), F32),
                        pltpu.VMEM((1, 2 * tq), F32),
                        pltpu.VMEM((LANES + SUM_ROWS, 2 * tq), F32),
                        pltpu.VMEM((2, 2 * LANES, 2 * tq), BF16)],
        in_specs=[
            pl.BlockSpec((1, LANES, n), lambda bb, p: (bb, p, 0)),
            pl.BlockSpec((1, LANES, n), lambda bb, p: (bb, p // 2, 0)),
            pl.BlockSpec((1, n, LANES), lambda bb, p: (bb, 0, p)),
            pl.BlockSpec((1, n, LANES), lambda bb, p: (bb, 0, 0)),
            pl.BlockSpec((1, LANES, n), lambda bb, p: (bb, p, 0)),
        ],
        out_specs=pl.BlockSpec((1, LANES, s), lambda bb, p: (bb, p, 0)),
        out_shape=jax.ShapeDtypeStruct((b, C_HEADS * C_NOPE, s), BF16),
        compiler_params=_params("arbitrary", "arbitrary"),
        name="mla_flash",
    )(qnt, qrt, kn, kr, vt)


def _rope_tables(seq, dim):
    t = jnp.arange(seq, dtype=jnp.int32)
    pos = jnp.stack([t // GRID_W, t % GRID_W], axis=0).astype(F32)
    half = dim // 2
    q = half // 2
    inv = ROPE_THETA ** (-jnp.arange(q, dtype=F32) / q)
    j = jnp.arange(dim)
    ang = pos[j // half].T * inv[j % q][None, :]
    sign = jnp.where((j % half) < q, -1.0, 1.0).astype(F32)
    cos = jnp.cos(ang)
    sin = jnp.sin(ang) * sign[None, :]
    reps = LANES // dim
    cos = jnp.tile(cos, (1, reps))
    sin = jnp.tile(sin, (1, reps))
    cos = jnp.concatenate([jnp.ones((CTX_LEN, LANES), F32), cos], axis=0)
    sin = jnp.concatenate([jnp.zeros((CTX_LEN, LANES), F32), sin], axis=0)
    return cos, sin


def _group_ones(group):
    r = jnp.arange(MXU_DIM) // group
    return (r[:, None] == r[None, :]).astype(BF16)


def _na_bias_table(rpb):
    cq = jnp.arange(GRID_W)
    c0 = jnp.clip(cq - NA_COLS // 2, 0, GRID_W - NA_COLS)
    col_ok = (cq[None, :] >= c0[:, None]) & (cq[None, :] < c0[:, None] + NA_COLS)
    dci = jnp.clip(cq[None, :] - cq[:, None], 1 - NA_COLS, NA_COLS - 1) + NA_COLS - 1
    tm = jnp.where(col_ok[None, None], rpb.astype(F32)[:, :, dci], NEG_INF)
    nd = 2 * NA_ROWS - 2
    t2 = jnp.concatenate([tm[:, :nd], tm[:, 1:nd + 1]], axis=-1)
    t2 = t2.reshape(B_HEADS // 2, 2, nd, GRID_W, LANES).transpose(0, 2, 1, 3, 4)
    return t2.reshape(B_HEADS // 2, nd, 2 * GRID_W, LANES)


def kernel(x, c, ctx, c_ctx, ada_w, ada_b, norm_mix, norm_mlp, mlp_w1, mlp_w2, e_w_in, e_w_out, a_q_norm, a_k_norm, a_sink, b_q_norm, b_k_norm, b_rpb, o_w_in, o_qa_norm, o_kva_norm, o_w_uq, o_w_ukv, o_qn_nope, o_qn_rope, o_kn_nope, o_kn_rope, o_w_out):
    bsz, seq, d = x.shape
    assert d == D_MODEL and ctx.shape[1] == CTX_LEN and seq % TM == 0 and ada_w.shape[0] == 2
    assert bsz + 1 <= 8

    cond = jnp.zeros((8, d), F32).at[:bsz].set(c).at[bsz].set(c_ctx)
    m = _adaln(cond, ada_w, ada_b)
    mods = [jnp.stack([jnp.broadcast_to(m[i, bsz], (bsz, 6 * d)), m[i, :bsz]], axis=1).reshape(2 * bsz, 1, 6 * d)
            for i in range(2)]

    xa = jnp.concatenate([ctx, x], axis=1)
    e64 = _group_ones(HEAD_DIM)
    e32 = _group_ones(C_ROPE)

    w = e_w_in[0]
    dup = lambda t: jnp.concatenate([t[:, :64], t[:, :64], t[:, 64:], t[:, 64:]], axis=1)
    w_ext = jnp.concatenate([w[:, 0:512], dup(w[:, 512:640]), w[:, 768:1280], w[:, 1280:1792],
                             dup(w[:, 640:768]), w[:, 1792:2304]], axis=1).astype(BF16)
    scale = HEAD_DIM ** -0.5
    gains = jnp.concatenate([jnp.tile(a_q_norm[0], 8) * scale, jnp.tile(a_k_norm[0], 4),
                             jnp.tile(b_q_norm[0], 8) * scale, jnp.tile(b_k_norm[0], 8)])[None, :]
    cos64, sin64 = _rope_tables(seq, HEAD_DIM)
    qa, ka, qb, kb, va, vb = _even_proj(xa, mods[0], norm_mix[0][None, :], w_ext, gains, e64, cos64, sin64)
    sink_col = jnp.repeat(a_sink[0].reshape(A_KV_HEADS, 4), A_BLOCK, axis=1).reshape(A_KV_HEADS, 4 * A_BLOCK, 1)
    oa = _window_attn(qa, ka, va, sink_col)
    ob = _na_attn(qb, kb, vb, _na_bias_table(b_rpb[0]))
    wo = e_w_out[0].astype(BF16)
    xa = _even_out(xa, oa, ob, mods[0], norm_mlp[0][None, :], wo[:512], wo[512:],
                   mlp_w1[0].astype(BF16), mlp_w2[0].astype(BF16))

    wi = o_w_in[0]
    win = jnp.concatenate([wi[:, :C_Q_RANK + C_KV_RANK]] + [wi[:, C_Q_RANK + C_KV_RANK:]] * 4, axis=1).astype(BF16)
    wuq = o_w_uq[0].reshape(C_Q_RANK, C_HEADS, C_NOPE + C_ROPE)
    wuq = jnp.concatenate([wuq[:, :, :C_NOPE].reshape(C_Q_RANK, -1), wuq[:, :, C_NOPE:].reshape(C_Q_RANK, -1)],
                          axis=1).astype(BF16)
    wukv = o_w_ukv[0].reshape(C_KV_RANK, C_HEADS, 2 * C_NOPE)
    wukv = jnp.concatenate([wukv[:, :, :C_NOPE].reshape(C_KV_RANK, -1), wukv[:, :, C_NOPE:].reshape(C_KV_RANK, -1)],
                           axis=1).astype(BF16)
    qscale = (C_NOPE + C_ROPE) ** -0.5 * math.log2(math.e)
    gq = (jnp.concatenate([jnp.tile(o_qn_nope[0], C_HEADS), jnp.tile(o_qn_rope[0], C_HEADS)]) * qscale)[None, :]
    gk = jnp.concatenate([jnp.tile(o_kn_nope[0], C_HEADS), jnp.tile(o_kn_rope[0], LANES // C_ROPE)])[None, :]
    cos32, sin32 = _rope_tables(seq, C_ROPE)
    qn, qr, kn, kr, vt = _odd_proj(xa, mods[1], norm_mix[1][None, :], win, o_qa_norm[0][None, :],
                                   o_kva_norm[0][None, :], wuq, wukv, gq, gk, e64, e32, cos32, sin32)
    ot = _flash_attn(qn, qr, kn, kr, vt)
    return _odd_out(xa, ot, mods[1], norm_mlp[1][None, :], o_w_out[0].astype(BF16),
                    mlp_w1[1].astype(BF16), mlp_w2[1].astype(BF16))
```

```python
import functools
import math

import jax
import jax.numpy as jnp
from jax import lax
from jax.experimental import pallas as pl
from jax.experimental.pallas import tpu as pltpu

F32 = jnp.float32
BF16 = jnp.bfloat16

D_MODEL = 1024
CTX_LEN = 256
GRID_W = 64
HEAD_DIM = 64
A_HEADS = 8
A_KV_HEADS = 2
A_WINDOW = 128
A_BLOCK = 128
B_HEADS = 8
NA_ROWS = 8
NA_COLS = 16
C_HEADS = 16
C_Q_RANK = 384
C_KV_RANK = 256
C_NOPE = 64
C_ROPE = 32
D_FF = 4 * D_MODEL
ROPE_THETA = 10000.0
NORM_EPS = 1e-6
NEG_INF = -1e30

LANES = 128
MXU_DIM = 256
TM = CTX_LEN
VMEM_LIMIT = 56 * 1024 * 1024


def _dot(a, b):
    return jnp.dot(a, b, preferred_element_type=F32)


def _dot_nt(a, b):
    return lax.dot_general(a, b, (((1,), (1,)), ((), ())), preferred_element_type=F32)


def _params(*sem):
    return pltpu.CompilerParams(dimension_semantics=sem, vmem_limit_bytes=VMEM_LIMIT)


def _norm_mod(x, g, sh, sc):
    ms = jnp.mean(x * x, axis=-1, keepdims=True)
    return (x * lax.rsqrt(ms + NORM_EPS) * g) * (1.0 + sc) + sh


def _group_rms(y, e, group):
    out = []
    for c0 in range(0, y.shape[1], MXU_DIM):
        cw = min(MXU_DIM, y.shape[1] - c0)
        yc = y[:, c0:c0 + cw]
        ss = _dot((yc * yc).astype(BF16), e[:cw, :cw])
        out.append(yc * lax.rsqrt(ss * (1.0 / group) + NORM_EPS))
    return out[0] if len(out) == 1 else jnp.concatenate(out, axis=1)


def _rope(r, cos, sin, half):
    w = r.shape[1]
    reps = w // LANES
    if reps > 1:
        cos = jnp.concatenate([cos] * reps, axis=1)
        sin = jnp.concatenate([sin] * reps, axis=1)
    lane = lax.broadcasted_iota(jnp.int32, r.shape, 1)
    up = pltpu.roll(r, w - half, axis=1)
    dn = pltpu.roll(r, half, axis=1)
    sw = jnp.where((lane & half) == 0, up, dn)
    return r * cos + sw * sin


def _stack_heads(q, lane_lo):
    z = jnp.zeros_like(q)
    return jnp.concatenate([jnp.where(lane_lo, q, z), jnp.where(lane_lo, z, q)], axis=0)


def _softmax_pv(parts, extra=None):
    m = None
    for s, _ in parts:
        mx = jnp.max(s, axis=1, keepdims=True)
        m = mx if m is None else jnp.maximum(m, mx)
    if extra is not None:
        m = jnp.maximum(m, extra)
    den = None
    o = None
    for s, v in parts:
        p = jnp.exp(s - m)
        d = jnp.sum(p, axis=1, keepdims=True)
        pv = _dot(p.astype(BF16), v)
        den = d if den is None else den + d
        o = pv if o is None else o + pv
    if extra is not None:
        den = den + jnp.exp(extra - m)
    return o / den


def _ada_kernel(cond_ref, w_ref, b_ref, o_ref):
    c = cond_ref[...]
    s = (c * jax.nn.sigmoid(c)).astype(BF16)
    o_ref[0] = _dot(s, w_ref[0].astype(BF16)) + b_ref[0]


def _adaln(cond, ada_w, ada_b):
    depth, d, n6 = ada_w.shape
    tn = 1536
    return pl.pallas_call(
        _ada_kernel,
        grid=(depth, n6 // tn),
        in_specs=[
            pl.BlockSpec((8, d), lambda l, j: (0, 0)),
            pl.BlockSpec((1, d, tn), lambda l, j: (l, 0, j)),
            pl.BlockSpec((1, 1, tn), lambda l, j: (l, 0, j)),
        ],
        out_specs=pl.BlockSpec((1, 8, tn), lambda l, j: (l, 0, j)),
        out_shape=jax.ShapeDtypeStruct((depth, 8, n6), F32),
        compiler_params=_params("arbitrary", "arbitrary"),
        name="adaln",
    )(cond, ada_w, ada_b.reshape(depth, 1, n6))


def _mod_spec(chunk, lat_only=False):
    if lat_only:
        return pl.BlockSpec((1, 1, D_MODEL), lambda b, i: (2 * b + 1, 0, chunk))
    return pl.BlockSpec((1, 1, D_MODEL), lambda b, i: (2 * b + jnp.minimum(i, 1), 0, chunk))


def _const_spec(shape):
    nd = len(shape)
    return pl.BlockSpec(shape, lambda *_: (0,) * nd)


EVEN_NORMED = 1664
EVEN_ROPED = 640
EVEN_COLS = 2304
QA_W = A_HEADS * HEAD_DIM
KVA_W = A_KV_HEADS * HEAD_DIM
QKVB_W = B_HEADS * HEAD_DIM


def _even_proj_kernel(x_ref, sh_ref, sc_ref, g_ref, w_ref, gains_ref, e_ref, cos_ref, sin_ref,
                      qat_ref, ka_ref, qb_ref, kb_ref, vat_ref, vb_ref):
    h = _norm_mod(x_ref[0], g_ref[...], sh_ref[0], sc_ref[0]).astype(BF16)
    y = _dot(h, w_ref[...])
    yn = _group_rms(y[:, :EVEN_NORMED], e_ref[...], HEAD_DIM) * gains_ref[...]
    r = _rope(yn[:, :EVEN_ROPED], cos_ref[...], sin_ref[...], HEAD_DIM // 4)
    qat_ref[0] = r[:, 0:QA_W].T.astype(BF16)
    ka_ref[0] = r[:, QA_W:EVEN_ROPED].astype(BF16)
    qb_ref[0] = yn[:, EVEN_ROPED:EVEN_ROPED + QKVB_W].astype(BF16)
    kb_ref[0] = yn[:, EVEN_ROPED + QKVB_W:EVEN_NORMED].astype(BF16)
    vat_ref[0] = y[:, EVEN_NORMED:EVEN_NORMED + KVA_W].T.astype(BF16)
    vb_ref[0] = y[:, EVEN_NORMED + KVA_W:].astype(BF16)


def _even_proj(xa, mods, g, w, gains, e64, cos, sin):
    b, n, d = xa.shape
    tok = lambda wd: (pl.BlockSpec((1, TM, wd), lambda bb, i: (bb, i, 0)),
                      jax.ShapeDtypeStruct((b, n, wd), BF16))
    tr = lambda wd: (pl.BlockSpec((1, wd, TM), lambda bb, i: (bb, 0, i)),
                     jax.ShapeDtypeStruct((b, wd, n), BF16))
    outs = (tr(QA_W), tok(KVA_W), tok(QKVB_W), tok(QKVB_W), tr(KVA_W), tok(QKVB_W))
    return pl.pallas_call(
        _even_proj_kernel,
        grid=(b, n // TM),
        in_specs=[
            pl.BlockSpec((1, TM, d), lambda bb, i: (bb, i, 0)),
            _mod_spec(0), _mod_spec(1),
            _const_spec((1, d)),
            _const_spec((d, EVEN_COLS)),
            _const_spec((1, EVEN_NORMED)),
            _const_spec((MXU_DIM, MXU_DIM)),
            pl.BlockSpec((TM, LANES), lambda bb, i: (i, 0)),
            pl.BlockSpec((TM, LANES), lambda bb, i: (i, 0)),
        ],
        out_specs=[o[0] for o in outs],
        out_shape=[o[1] for o in outs],
        compiler_params=_params("arbitrary", "arbitrary"),
        name="even_proj",
    )(xa, mods, mods, g, w, gains, e64, cos, sin)


GQA_R = A_HEADS // A_KV_HEADS
WIN_BAND = 3 * A_BLOCK
WIN_LANES = GQA_R * A_BLOCK


def _window_kernel(qt_ref, k_ref, vt_ref, sink_ref, o_ref, s_ref, mx_ref, qs_ref, *, seq):
    g = pl.program_id(1)
    nb = seq // A_BLOCK
    nkeys = WIN_BAND + CTX_LEN
    assert nb % 2 == 0
    sink = sink_ref[0]
    ones_loc = jnp.ones((SUM_ROWS, nkeys), BF16)
    ones_ctx = jnp.ones((SUM_ROWS, CTX_LEN), BF16)
    rel = (lax.broadcasted_iota(jnp.int32, (WIN_BAND, WIN_LANES), 1) % A_BLOCK
           - lax.broadcasted_iota(jnp.int32, (WIN_BAND, WIN_LANES), 0))

    def prep_q(col0, qslot):
        qt = qt_ref[0, :, pl.ds(col0, A_BLOCK)]
        zero = jnp.zeros((HEAD_DIM, A_BLOCK), BF16)
        for h in range(GQA_R):
            qh = qt[h * HEAD_DIM:(h + 1) * HEAD_DIM]
            lanes = slice(h * A_BLOCK, (h + 1) * A_BLOCK)
            qs_ref[qslot, 0:HEAD_DIM, lanes] = jnp.where(g == 0, qh, zero)
            qs_ref[qslot, HEAD_DIM:, lanes] = jnp.where(g == 0, zero, qh)

    def band_start(n):
        return pl.multiple_of(jnp.clip((n - 1) * A_BLOCK, 0, seq - WIN_BAND), A_BLOCK)

    def scores(qslot, n, sslot):
        start = band_start(n)
        keys = jnp.concatenate([k_ref[0, pl.ds(CTX_LEN + start, WIN_BAND), :], k_ref[0, 0:CTX_LEN, :]], axis=0)
        st = _dot(keys, qs_ref[qslot])
        diff = rel + (n * A_BLOCK - start)
        s_loc = jnp.where(jnp.abs(diff) <= A_WINDOW, st[:WIN_BAND], NEG_INF)
        s_ctx = st[WIN_BAND:]
        s_ref[sslot, 0:WIN_BAND] = s_loc
        s_ref[sslot, WIN_BAND:] = s_ctx
        mx_ref[sslot] = jnp.maximum(jnp.maximum(jnp.max(s_loc, axis=0, keepdims=True),
                                                jnp.max(s_ctx, axis=0, keepdims=True)), sink)

    def write(res, m, col0):
        o = res[0:HEAD_DIM] / (res[HEAD_DIM:HEAD_DIM + 1] + jnp.exp2(sink - m))
        for h in range(GQA_R):
            o_ref[0, h * HEAD_DIM:(h + 1) * HEAD_DIM, pl.ds(col0, A_BLOCK)] = (
                o[:, h * A_BLOCK:(h + 1) * A_BLOCK].astype(o_ref.dtype))

    def update(n, sslot):
        start = band_start(n)
        m = mx_ref[sslot]
        pt = jnp.exp2(s_ref[sslot] - m).astype(BF16)
        va = jnp.concatenate([vt_ref[0, :, pl.ds(CTX_LEN + start, WIN_BAND)], vt_ref[0, :, 0:CTX_LEN]], axis=1)
        res = _dot(jnp.concatenate([va, ones_loc], axis=0), pt)
        write(res, m, pl.multiple_of(CTX_LEN + n * A_BLOCK, A_BLOCK))

    for blk in range(CTX_LEN // A_BLOCK):
        prep_q(blk * A_BLOCK, 0)
        st = _dot(k_ref[0, 0:CTX_LEN, :], qs_ref[0])
        m = jnp.maximum(jnp.max(st, axis=0, keepdims=True), sink)
        pt = jnp.exp2(st - m).astype(BF16)
        res = _dot(jnp.concatenate([vt_ref[0, :, 0:CTX_LEN], ones_ctx], axis=0), pt)
        write(res, m, blk * A_BLOCK)

    def two_blocks(t2, carry):
        n1 = 2 * t2 + 1
        n2 = jnp.minimum(2 * t2 + 2, nb - 1)
        prep_q(pl.multiple_of(CTX_LEN + n1 * A_BLOCK, A_BLOCK), 1)
        scores(1, n1, 1)
        update(2 * t2, 0)
        prep_q(pl.multiple_of(CTX_LEN + n2 * A_BLOCK, A_BLOCK), 0)
        scores(0, n2, 0)
        update(n1, 1)
        return carry

    prep_q(CTX_LEN, 0)
    scores(0, 0, 0)
    lax.fori_loop(0, nb // 2, two_blocks, 0)


def _window_attn(qat, ka, vat, sink_row):
    b, _, n = qat.shape
    return pl.pallas_call(
        functools.partial(_window_kernel, seq=n - CTX_LEN),
        grid=(b, A_KV_HEADS),
        scratch_shapes=[pltpu.VMEM((2, WIN_BAND + CTX_LEN, WIN_LANES), F32),
                        pltpu.VMEM((2, 1, WIN_LANES), F32),
                        pltpu.VMEM((2, 2 * HEAD_DIM, WIN_LANES), BF16)],
        in_specs=[
            pl.BlockSpec((1, GQA_R * HEAD_DIM, n), lambda bb, g: (bb, g, 0)),
            pl.BlockSpec((1, n, KVA_W), lambda bb, g: (bb, 0, 0)),
            pl.BlockSpec((1, HEAD_DIM, n), lambda bb, g: (bb, g, 0)),
            pl.BlockSpec((1, 1, WIN_LANES), lambda bb, g: (g, 0, 0)),
        ],
        out_specs=pl.BlockSpec((1, GQA_R * HEAD_DIM, n), lambda bb, g: (bb, g, 0)),
        out_shape=jax.ShapeDtypeStruct((b, QA_W, n), BF16),
        compiler_params=_params("arbitrary", "arbitrary"),
        name="window_attn",
    )(qat, ka, vat, sink_row)


def _na_kernel(q_ref, k_ref, v_ref, t_ref, o_ref, *, rows_n):
    i = pl.program_id(2)
    kc = k_ref[0, 0:CTX_LEN, :]
    vc = v_ref[0, 0:CTX_LEN, :]

    @pl.when(i == 0)
    def _():
        lane_lo = lax.broadcasted_iota(jnp.int32, (TM, LANES), 1) < HEAD_DIM
        qs = _stack_heads(q_ref[0], lane_lo)
        o = _softmax_pv([(_dot_nt(qs, kc), vc)])
        o_ref[0] = jnp.where(lane_lo, o[0:TM], o[TM:2 * TM]).astype(o_ref.dtype)

    @pl.when(i > 0)
    def _():
        lane_lo = lax.broadcasted_iota(jnp.int32, (GRID_W, LANES), 1) < HEAD_DIM
        nloc = NA_ROWS * GRID_W
        for rr in range(TM // GRID_W):
            rows = slice(rr * GRID_W, (rr + 1) * GRID_W)
            r = (i - 1) * (TM // GRID_W) + rr
            r0 = jnp.clip(r - NA_ROWS // 2, 0, rows_n - NA_ROWS)
            d0 = r0 - r + NA_ROWS - 1
            kstart = pl.multiple_of(CTX_LEN + r0 * GRID_W, GRID_W)
            kl = k_ref[0, pl.ds(kstart, nloc), :]
            vl = v_ref[0, pl.ds(kstart, nloc), :]
            qs = _stack_heads(q_ref[0, rows, :], lane_lo)
            s_loc = _dot_nt(qs, kl)
            bias = jnp.concatenate([t_ref[0, d0 + 2 * c] for c in range(nloc // LANES)], axis=1)
            o = _softmax_pv([(s_loc + bias, vl), (_dot_nt(qs, kc), vc)])
            o_ref[0, rows, :] = jnp.where(lane_lo, o[0:GRID_W], o[GRID_W:2 * GRID_W]).astype(o_ref.dtype)


def _na_attn(qb, kb, vb, t2):
    b, n, _ = qb.shape
    rows_n = (n - CTX_LEN) // GRID_W
    assert rows_n >= NA_ROWS
    return pl.pallas_call(
        functools.partial(_na_kernel, rows_n=rows_n),
        grid=(b, B_HEADS // 2, n // TM),
        in_specs=[
            pl.BlockSpec((1, TM, LANES), lambda bb, p, i: (bb, i, p)),
            pl.BlockSpec((1, n, LANES), lambda bb, p, i: (bb, 0, p)),
            pl.BlockSpec((1, n, LANES), lambda bb, p, i: (bb, 0, p)),
            pl.BlockSpec((1, 2 * NA_ROWS - 2, 2 * GRID_W, LANES), lambda bb, p, i: (p, 0, 0, 0)),
        ],
        out_specs=pl.BlockSpec((1, TM, LANES), lambda bb, p, i: (bb, i, p)),
        out_shape=jax.ShapeDtypeStruct((b, n, B_HEADS * HEAD_DIM), BF16),
        compiler_params=_params("arbitrary", "arbitrary", "arbitrary"),
        name="na_attn",
    )(qb, kb, vb, t2)


def _mlp_tail(x, y, g1, sh2, sc2, g2, gn, w1_ref, w2_ref):
    x1 = x + g1 * y
    h = _norm_mod(x1, gn, sh2, sc2).astype(BF16)
    mlp = None
    for j in range(D_FF // D_MODEL):
        cols = slice(j * D_MODEL, (j + 1) * D_MODEL)
        u = jnp.maximum(_dot(h, w1_ref[:, cols]), 0.0)
        part = _dot((u * u).astype(BF16), w2_ref[cols, :])
        mlp = part if mlp is None else mlp + part
    return x1 + g2 * mlp


def _even_out_kernel(x_ref, oat_ref, ob_ref, g1_ref, sh2_ref, sc2_ref, g2_ref, gn_ref,
                     woa_ref, wob_ref, w1_ref, w2_ref, o_ref):
    oa = oat_ref[0].astype(F32).T.astype(BF16)
    y = _dot(oa, woa_ref[...]) + _dot(ob_ref[0], wob_ref[...])
    o_ref[0] = _mlp_tail(x_ref[0], y, g1_ref[0], sh2_ref[0], sc2_ref[0], g2_ref[0], gn_ref[...],
                         w1_ref, w2_ref)


def _even_out(xa, oa, ob, mods, gn, woa, wob, w1, w2):
    b, n, d = xa.shape
    return pl.pallas_call(
        _even_out_kernel,
        grid=(b, n // TM),
        in_specs=[
            pl.BlockSpec((1, TM, d), lambda bb, i: (bb, i, 0)),
            pl.BlockSpec((1, QA_W, TM), lambda bb, i: (bb, 0, i)),
            pl.BlockSpec((1, TM, QKVB_W), lambda bb, i: (bb, i, 0)),
            _mod_spec(2), _mod_spec(3), _mod_spec(4), _mod_spec(5),
            _const_spec((1, d)),
            _const_spec((QA_W, d)), _const_spec((QKVB_W, d)),
            _const_spec((d, D_FF)), _const_spec((D_FF, d)),
        ],
        out_specs=pl.BlockSpec((1, TM, d), lambda bb, i: (bb, i, 0)),
        out_shape=jax.ShapeDtypeStruct((b, n, d), F32),
        compiler_params=_params("arbitrary", "arbitrary"),
        name="even_out_mlp",
    )(xa, oa, ob, mods, mods, mods, mods, gn, woa, wob, w1, w2)


def _odd_out_kernel(x_ref, ot_ref, g1_ref, sh2_ref, sc2_ref, g2_ref, gn_ref,
                    wo_ref, w1_ref, w2_ref, o_ref):
    o = ot_ref[0].astype(F32).T.astype(BF16)
    y = _dot(o, wo_ref[...])
    o_ref[0] = _mlp_tail(x_ref[0], y, g1_ref[0], sh2_ref[0], sc2_ref[0], g2_ref[0], gn_ref[...],
                         w1_ref, w2_ref)


def _odd_out(xa, ot, mods, gn, wo, w1, w2):
    b, n, d = xa.shape
    s = n - CTX_LEN
    return pl.pallas_call(
        _odd_out_kernel,
        grid=(b, s // TM),
        in_specs=[
            pl.BlockSpec((1, TM, d), lambda bb, i: (bb, i + 1, 0)),
            pl.BlockSpec((1, d, TM), lambda bb, i: (bb, 0, i)),
            _mod_spec(2, True), _mod_spec(3, True), _mod_spec(4, True), _mod_spec(5, True),
            _const_spec((1, d)),
            _const_spec((d, d)),
            _const_spec((d, D_FF)), _const_spec((D_FF, d)),
        ],
        out_specs=pl.BlockSpec((1, TM, d), lambda bb, i: (bb, i, 0)),
        out_shape=jax.ShapeDtypeStruct((b, s, d), F32),
        compiler_params=_params("arbitrary", "arbitrary"),
        name="odd_out_mlp",
    )(xa, ot, mods, mods, mods, mods, gn, wo, w1, w2)


ODD_IN_COLS = C_Q_RANK + C_KV_RANK + LANES
Q_NOPE_W = C_HEADS * C_NOPE
Q_ROPE_W = C_HEADS * C_ROPE


def _odd_proj_kernel(x_ref, sh_ref, sc_ref, g_ref, win_ref, qag_ref, kvg_ref, wuq_ref, wukv_ref,
                     gq_ref, gk_ref, e64_ref, e32_ref, cos_ref, sin_ref,
                     qn_ref, qr_ref, kn_ref, kr_ref, vt_ref):
    h = _norm_mod(x_ref[0], g_ref[...], sh_ref[0], sc_ref[0]).astype(BF16)
    y = _dot(h, win_ref[...])
    cq = y[:, :C_Q_RANK]
    ckv = y[:, C_Q_RANK:C_Q_RANK + C_KV_RANK]
    kr = y[:, C_Q_RANK + C_KV_RANK:]

    def rms(t, g):
        return (t * lax.rsqrt(jnp.mean(t * t, axis=-1, keepdims=True) + NORM_EPS) * g).astype(BF16)

    q = _dot(rms(cq, qag_ref[...]), wuq_ref[...])
    kv = _dot(rms(ckv, kvg_ref[...]), wukv_ref[...])
    gq = gq_ref[...]
    gk = gk_ref[...]
    cos = cos_ref[...]
    sin = sin_ref[...]
    qn = _group_rms(q[:, :Q_NOPE_W], e64_ref[...], C_NOPE) * gq[:, :Q_NOPE_W]
    qn_ref[0] = qn.T.astype(BF16)
    qr = _group_rms(q[:, Q_NOPE_W:], e32_ref[...], C_ROPE) * gq[:, Q_NOPE_W:]
    qr_ref[0] = _rope(qr, cos, sin, C_ROPE // 4).T.astype(BF16)
    kn_ref[0] = (_group_rms(kv[:, :Q_NOPE_W], e64_ref[...], C_NOPE) * gk[:, :Q_NOPE_W]).astype(BF16)
    krn = _group_rms(kr, e32_ref[...], C_ROPE) * gk[:, Q_NOPE_W:]
    kr_ref[0] = _rope(krn, cos, sin, C_ROPE // 4).astype(BF16)
    vt_ref[0] = kv[:, Q_NOPE_W:].T.astype(BF16)


def _odd_proj(xa, mods, g, win, qag, kvg, wuq, wukv, gq, gk, e64, e32, cos, sin):
    b, n, d = xa.shape
    tok = lambda wd: pl.BlockSpec((1, TM, wd), lambda bb, i: (bb, i, 0))
    return pl.pallas_call(
        _odd_proj_kernel,
        grid=(b, n // TM),
        in_specs=[
            tok(d), _mod_spec(0), _mod_spec(1),
            _const_spec((1, d)),
            _const_spec((d, ODD_IN_COLS)),
            _const_spec((1, C_Q_RANK)), _const_spec((1, C_KV_RANK)),
            _const_spec((C_Q_RANK, Q_NOPE_W + Q_ROPE_W)),
            _const_spec((C_KV_RANK, 2 * Q_NOPE_W)),
            _const_spec((1, Q_NOPE_W + Q_ROPE_W)), _const_spec((1, Q_NOPE_W + LANES)),
            _const_spec((MXU_DIM, MXU_DIM)), _const_spec((MXU_DIM, MXU_DIM)),
            pl.BlockSpec((TM, LANES), lambda bb, i: (i, 0)),
            pl.BlockSpec((TM, LANES), lambda bb, i: (i, 0)),
        ],
        out_specs=[pl.BlockSpec((1, Q_NOPE_W, TM), lambda bb, i: (bb, 0, i)),
                   pl.BlockSpec((1, Q_ROPE_W, TM), lambda bb, i: (bb, 0, i)),
                   tok(Q_NOPE_W), tok(LANES),
                   pl.BlockSpec((1, Q_NOPE_W, TM), lambda bb, i: (bb, 0, i))],
        out_shape=[jax.ShapeDtypeStruct((b, Q_NOPE_W, n), BF16),
                   jax.ShapeDtypeStruct((b, Q_ROPE_W, n), BF16),
                   jax.ShapeDtypeStruct((b, n, Q_NOPE_W), BF16),
                   jax.ShapeDtypeStruct((b, n, LANES), BF16),
                   jax.ShapeDtypeStruct((b, Q_NOPE_W, n), BF16)],
        compiler_params=_params("arbitrary", "arbitrary"),
        name="odd_proj",
    )(xa, mods, mods, g, win, qag, kvg, wuq, wukv, gq, gk, e64, e32, cos, sin)


SUM_ROWS = 16


def _kv_tile(n):
    for t in (768, 256):
        if n % t == 0 and (n // t) % 2 == 1:
            return t
    raise ValueError(f"joint sequence length {n} has no odd split into 256-multiples")


def _flash_kernel(qnt_ref, qrt_ref, kn_ref, kr_ref, vt_ref, o_ref,
                  s_ref, mx_ref, m_ref, acc_ref, qt_ref, *, tk, tq):
    p = pl.program_id(1)
    n = kn_ref.shape[1]
    nq = (n - CTX_LEN) // tq
    nch = n // tk
    assert nq % 2 == 0 and nch % 2 == 1
    ones = jnp.ones((SUM_ROWS, tk), BF16)
    row = lax.broadcasted_iota(jnp.int32, (LANES, tq), 0)

    def prep_q(qi, qslot):
        c0 = pl.multiple_of(CTX_LEN + qi * tq, tq)
        qn = qnt_ref[0, :, pl.ds(c0, tq)]
        qr = qrt_ref[0, :, pl.ds(c0, tq)]
        zero = jnp.zeros_like(qn)
        for hh in range(2):
            off = ((2 * p + hh) % 4) * C_ROPE
            keep_n = (row < C_NOPE) if hh == 0 else (row >= C_NOPE)
            keep_r = (row >= off) & (row < off + C_ROPE)
            qt_ref[qslot, 0:LANES, hh * tq:(hh + 1) * tq] = jnp.where(keep_n, qn, zero)
            qt_ref[qslot, LANES:, hh * tq:(hh + 1) * tq] = jnp.where(keep_r, qr, zero)

    def scores(qslot, j, sslot):
        k0 = j * tk
        kcat = jnp.concatenate([kn_ref[0, pl.ds(k0, tk), :], kr_ref[0, pl.ds(k0, tk), :]], axis=1)
        st = _dot(kcat, qt_ref[qslot])
        s_ref[sslot] = st
        mx_ref[sslot] = jnp.max(st, axis=0, keepdims=True)

    def update(j, sslot):
        k0 = j * tk
        m_old = m_ref[...]
        m_new = jnp.maximum(m_old, mx_ref[sslot])
        alpha = jnp.exp2(m_old - m_new)
        pt = jnp.exp2(s_ref[sslot] - m_new).astype(BF16)
        va = jnp.concatenate([vt_ref[0, :, pl.ds(k0, tk)], ones], axis=0)
        m_ref[...] = m_new
        acc_ref[...] = alpha * acc_ref[...] + _dot(va, pt)

    def reset():
        m_ref[...] = jnp.full(m_ref.shape, NEG_INF, F32)
        acc_ref[...] = jnp.zeros(acc_ref.shape, F32)

    def finish(qi):
        acc = acc_ref[...]
        c0 = pl.multiple_of(qi * tq, tq)
        o_ref[0, 0:C_NOPE, pl.ds(c0, tq)] = (acc[0:C_NOPE, :tq] / acc[LANES:LANES + 1, :tq]).astype(o_ref.dtype)
        o_ref[0, C_NOPE:, pl.ds(c0, tq)] = (acc[C_NOPE:LANES, tq:] / acc[LANES:LANES + 1, tq:]).astype(o_ref.dtype)
        reset()

    def run_tile(qi, slot, next_qi):
        other = 1 - slot
        for j in range(nch - 1):
            scores(slot, j + 1, other if j % 2 == 0 else slot)
            update(j, slot if j % 2 == 0 else other)
        prep_q(next_qi, other)
        scores(other, 0, other)
        update(nch - 1, slot)
        finish(qi)

    def two_tiles(t2, carry):
        run_tile(2 * t2, 0, 2 * t2 + 1)
        run_tile(2 * t2 + 1, 1, jnp.minimum(2 * t2 + 2, nq - 1))
        return carry

    reset()
    prep_q(0, 0)
    scores(0, 0, 0)
    lax.fori_loop(0, nq // 2, two_tiles, 0)


def _flash_attn(qnt, qrt, kn, kr, vt):
    b, n, _ = kn.shape
    s = n - CTX_LEN
    tq = TM
    tk = _kv_tile(n)
    return pl.pallas_call(
        functools.partial(_flash_kernel, tk=tk, tq=tq),
        grid=(b, C_HEADS // 2),
        scratch_shapes=[pltpu.VMEM((2, tk, 2 * tq), F32),
                        pltpu.VMEM((2, 1, 2 * tq), F32),
                        pltpu.VMEM((1, 2 * tq), F32),
                        pltpu.VMEM((LANES + SUM_ROWS, 2 * tq), F32),
                        pltpu.VMEM((2, 2 * LANES, 2 * tq), BF16)],
        in_specs=[
            pl.BlockSpec((1, LANES, n), lambda bb, p: (bb, p, 0)),
            pl.BlockSpec((1, LANES, n), lambda bb, p: (bb, p // 2, 0)),
            pl.BlockSpec((1, n, LANES), lambda bb, p: (bb, 0, p)),
            pl.BlockSpec((1, n, LANES), lambda bb, p: (bb, 0, 0)),
            pl.BlockSpec((1, LANES, n), lambda bb, p: (bb, p, 0)),
        ],
        out_specs=pl.BlockSpec((1, LANES, s), lambda bb, p: (bb, p, 0)),
        out_shape=jax.ShapeDtypeStruct((b, C_HEADS * C_NOPE, s), BF16),
        compiler_params=_params("arbitrary", "arbitrary"),
        name="mla_flash",
    )(qnt, qrt, kn, kr, vt)


def _rope_tables(seq, dim):
    t = jnp.arange(seq, dtype=jnp.int32)
    pos = jnp.stack([t // GRID_W, t % GRID_W], axis=0).astype(F32)
    half = dim // 2
    q = half // 2
    inv = ROPE_THETA ** (-jnp.arange(q, dtype=F32) / q)
    j = jnp.arange(dim)
    ang = pos[j // half].T * inv[j % q][None, :]
    sign = jnp.where((j % half) < q, -1.0, 1.0).astype(F32)
    cos = jnp.cos(ang)
    sin = jnp.sin(ang) * sign[None, :]
    reps = LANES // dim
    cos = jnp.tile(cos, (1, reps))
    sin = jnp.tile(sin, (1, reps))
    cos = jnp.concatenate([jnp.ones((CTX_LEN, LANES), F32), cos], axis=0)
    sin = jnp.concatenate([jnp.zeros((CTX_LEN, LANES), F32), sin], axis=0)
    return cos, sin


def _group_ones(group):
    r = jnp.arange(MXU_DIM) // group
    return (r[:, None] == r[None, :]).astype(BF16)


def _na_bias_table(rpb):
    cq = jnp.arange(GRID_W)
    c0 = jnp.clip(cq - NA_COLS // 2, 0, GRID_W - NA_COLS)
    col_ok = (cq[None, :] >= c0[:, None]) & (cq[None, :] < c0[:, None] + NA_COLS)
    dci = jnp.clip(cq[None, :] - cq[:, None], 1 - NA_COLS, NA_COLS - 1) + NA_COLS - 1
    tm = jnp.where(col_ok[None, None], rpb.astype(F32)[:, :, dci], NEG_INF)
    nd = 2 * NA_ROWS - 2
    t2 = jnp.concatenate([tm[:, :nd], tm[:, 1:nd + 1]], axis=-1)
    t2 = t2.reshape(B_HEADS // 2, 2, nd, GRID_W, LANES).transpose(0, 2, 1, 3, 4)
    return t2.reshape(B_HEADS // 2, nd, 2 * GRID_W, LANES)


def kernel(x, c, ctx, c_ctx, ada_w, ada_b, norm_mix, norm_mlp, mlp_w1, mlp_w2, e_w_in, e_w_out, a_q_norm, a_k_norm, a_sink, b_q_norm, b_k_norm, b_rpb, o_w_in, o_qa_norm, o_kva_norm, o_w_uq, o_w_ukv, o_qn_nope, o_qn_rope, o_kn_nope, o_kn_rope, o_w_out):
    bsz, seq, d = x.shape
    assert d == D_MODEL and ctx.shape[1] == CTX_LEN and seq % TM == 0 and ada_w.shape[0] == 2
    assert bsz + 1 <= 8

    cond = jnp.zeros((8, d), F32).at[:bsz].set(c).at[bsz].set(c_ctx)
    m = _adaln(cond, ada_w, ada_b)
    mods = [jnp.stack([jnp.broadcast_to(m[i, bsz], (bsz, 6 * d)), m[i, :bsz]], axis=1).reshape(2 * bsz, 1, 6 * d)
            for i in range(2)]

    xa = jnp.concatenate([ctx, x], axis=1)
    e64 = _group_ones(HEAD_DIM)
    e32 = _group_ones(C_ROPE)

    w = e_w_in[0]
    w_ext = jnp.concatenate([w[:, 0:512], w[:, 512:640], w[:, 768:1280], w[:, 1280:1792],
                             w[:, 640:768], w[:, 1792:2304]], axis=1).astype(BF16)
    scale = HEAD_DIM ** -0.5
    log2e = math.log2(math.e)
    gains = jnp.concatenate([jnp.tile(a_q_norm[0], A_HEADS) * (scale * log2e), jnp.tile(a_k_norm[0], A_KV_HEADS),
                             jnp.tile(b_q_norm[0], B_HEADS) * scale, jnp.tile(b_k_norm[0], B_HEADS)])[None, :]
    cos64, sin64 = _rope_tables(seq, HEAD_DIM)
    qat, ka, qb, kb, vat, vb = _even_proj(xa, mods[0], norm_mix[0][None, :], w_ext, gains, e64, cos64, sin64)
    sink_row = (jnp.repeat(a_sink[0].reshape(A_KV_HEADS, GQA_R), A_BLOCK, axis=1) * log2e)[:, None, :]
    oa = _window_attn(qat, ka, vat, sink_row)
    ob = _na_attn(qb, kb, vb, _na_bias_table(b_rpb[0]))
    wo = e_w_out[0].astype(BF16)
    xa = _even_out(xa, oa, ob, mods[0], norm_mlp[0][None, :], wo[:512], wo[512:],
                   mlp_w1[0].astype(BF16), mlp_w2[0].astype(BF16))

    wi = o_w_in[0]
    win = jnp.concatenate([wi[:, :C_Q_RANK + C_KV_RANK]] + [wi[:, C_Q_RANK + C_KV_RANK:]] * 4, axis=1).astype(BF16)
    wuq = o_w_uq[0].reshape(C_Q_RANK, C_HEADS, C_NOPE + C_ROPE)
    wuq = jnp.concatenate([wuq[:, :, :C_NOPE].reshape(C_Q_RANK, -1), wuq[:, :, C_NOPE:].reshape(C_Q_RANK, -1)],
                          axis=1).astype(BF16)
    wukv = o_w_ukv[0].reshape(C_KV_RANK, C_HEADS, 2 * C_NOPE)
    wukv = jnp.concatenate([wukv[:, :, :C_NOPE].reshape(C_KV_RANK, -1), wukv[:, :, C_NOPE:].reshape(C_KV_RANK, -1)],
                           axis=1).astype(BF16)
    qscale = (C_NOPE + C_ROPE) ** -0.5 * math.log2(math.e)
    gq = (jnp.concatenate([jnp.tile(o_qn_nope[0], C_HEADS), jnp.tile(o_qn_rope[0], C_HEADS)]) * qscale)[None, :]
    gk = jnp.concatenate([jnp.tile(o_kn_nope[0], C_HEADS), jnp.tile(o_kn_rope[0], LANES // C_ROPE)])[None, :]
    cos32, sin32 = _rope_tables(seq, C_ROPE)
    qn, qr, kn, kr, vt = _odd_proj(xa, mods[1], norm_mix[1][None, :], win, o_qa_norm[0][None, :],
                                   o_kva_norm[0][None, :], wuq, wukv, gq, gk, e64, e32, cos32, sin32)
    ot = _flash_attn(qn, qr, kn, kr, vt)
    return _odd_out(xa, ot, mods[1], norm_mlp[1][None, :], o_w_out[0].astype(BF16),
                    mlp_w1[1].astype(BF16), mlp_w2[1].astype(BF16))
```

```python
import functools
import math

import jax
import jax.numpy as jnp
from jax import lax
from jax.experimental import pallas as pl
from jax.experimental.pallas import tpu as pltpu

F32 = jnp.float32
BF16 = jnp.bfloat16

D_MODEL = 1024
CTX_LEN = 256
GRID_W = 64
HEAD_DIM = 64
A_HEADS = 8
A_KV_HEADS = 2
A_WINDOW = 128
A_BLOCK = 128
B_HEADS = 8
NA_ROWS = 8
NA_COLS = 16
C_HEADS = 16
C_Q_RANK = 384
C_KV_RANK = 256
C_NOPE = 64
C_ROPE = 32
D_FF = 4 * D_MODEL
ROPE_THETA = 10000.0
NORM_EPS = 1e-6
NEG_INF = -1e30

LANES = 128
MXU_DIM = 256
TM = CTX_LEN
VMEM_LIMIT = 56 * 1024 * 1024


def _dot(a, b):
    return jnp.dot(a, b, preferred_element_type=F32)


def _dot_nt(a, b):
    return lax.dot_general(a, b, (((1,), (1,)), ((), ())), preferred_element_type=F32)


def _params(*sem):
    return pltpu.CompilerParams(dimension_semantics=sem, vmem_limit_bytes=VMEM_LIMIT)


def _norm_mod(x, g, sh, sc):
    ms = jnp.mean(x * x, axis=-1, keepdims=True)
    return (x * lax.rsqrt(ms + NORM_EPS) * g) * (1.0 + sc) + sh


def _group_rms(y, e, group):
    out = []
    for c0 in range(0, y.shape[1], MXU_DIM):
        cw = min(MXU_DIM, y.shape[1] - c0)
        yc = y[:, c0:c0 + cw]
        ss = _dot((yc * yc).astype(BF16), e[:cw, :cw])
        out.append(yc * lax.rsqrt(ss * (1.0 / group) + NORM_EPS))
    return out[0] if len(out) == 1 else jnp.concatenate(out, axis=1)


def _rope(r, cos, sin, half):
    w = r.shape[1]
    reps = w // LANES
    if reps > 1:
        cos = jnp.concatenate([cos] * reps, axis=1)
        sin = jnp.concatenate([sin] * reps, axis=1)
    lane = lax.broadcasted_iota(jnp.int32, r.shape, 1)
    up = pltpu.roll(r, w - half, axis=1)
    dn = pltpu.roll(r, half, axis=1)
    sw = jnp.where((lane & half) == 0, up, dn)
    return r * cos + sw * sin


def _stack_heads(q, lane_lo):
    z = jnp.zeros_like(q)
    return jnp.concatenate([jnp.where(lane_lo, q, z), jnp.where(lane_lo, z, q)], axis=0)


def _softmax_pv(parts, extra=None):
    m = None
    for s, _ in parts:
        mx = jnp.max(s, axis=1, keepdims=True)
        m = mx if m is None else jnp.maximum(m, mx)
    if extra is not None:
        m = jnp.maximum(m, extra)
    den = None
    o = None
    for s, v in parts:
        p = jnp.exp(s - m)
        d = jnp.sum(p, axis=1, keepdims=True)
        pv = _dot(p.astype(BF16), v)
        den = d if den is None else den + d
        o = pv if o is None else o + pv
    if extra is not None:
        den = den + jnp.exp(extra - m)
    return o / den


def _ada_kernel(cond_ref, w_ref, b_ref, o_ref):
    c = cond_ref[...]
    s = (c * jax.nn.sigmoid(c)).astype(BF16)
    o_ref[0] = _dot(s, w_ref[0].astype(BF16)) + b_ref[0]


def _adaln(cond, ada_w, ada_b):
    depth, d, n6 = ada_w.shape
    tn = 1536
    return pl.pallas_call(
        _ada_kernel,
        grid=(depth, n6 // tn),
        in_specs=[
            pl.BlockSpec((8, d), lambda l, j: (0, 0)),
            pl.BlockSpec((1, d, tn), lambda l, j: (l, 0, j)),
            pl.BlockSpec((1, 1, tn), lambda l, j: (l, 0, j)),
        ],
        out_specs=pl.BlockSpec((1, 8, tn), lambda l, j: (l, 0, j)),
        out_shape=jax.ShapeDtypeStruct((depth, 8, n6), F32),
        compiler_params=_params("arbitrary", "arbitrary"),
        name="adaln",
    )(cond, ada_w, ada_b.reshape(depth, 1, n6))


def _mod_spec(chunk, lat_only=False):
    if lat_only:
        return pl.BlockSpec((1, 1, D_MODEL), lambda b, i: (2 * b + 1, 0, chunk))
    return pl.BlockSpec((1, 1, D_MODEL), lambda b, i: (2 * b + jnp.minimum(i, 1), 0, chunk))


def _const_spec(shape):
    nd = len(shape)
    return pl.BlockSpec(shape, lambda *_: (0,) * nd)


EVEN_NORMED = 1664
EVEN_ROPED = 640
EVEN_COLS = 2304
QA_W = A_HEADS * HEAD_DIM
KVA_W = A_KV_HEADS * HEAD_DIM
QKVB_W = B_HEADS * HEAD_DIM


def _joint_specs(d):
    return [pl.BlockSpec((1, CTX_LEN, d), lambda bb, i: (bb, 0, 0)),
            pl.BlockSpec((1, TM, d), lambda bb, i: (bb, jnp.maximum(i - 1, 0), 0))]


def _joint_tile(ctx_ref, x_ref):
    return jnp.where(pl.program_id(1) == 0, ctx_ref[0], x_ref[0])


def _even_proj_kernel(ctx_ref, x_ref, sh_ref, sc_ref, g_ref, w_ref, gains_ref, e_ref, cos_ref, sin_ref,
                      qat_ref, ka_ref, qb_ref, kb_ref, vat_ref, vb_ref):
    h = _norm_mod(_joint_tile(ctx_ref, x_ref), g_ref[...], sh_ref[0], sc_ref[0]).astype(BF16)
    y = _dot(h, w_ref[...])
    yn = _group_rms(y[:, :EVEN_NORMED], e_ref[...], HEAD_DIM) * gains_ref[...]
    r = _rope(yn[:, :EVEN_ROPED], cos_ref[...], sin_ref[...], HEAD_DIM // 4)
    qat_ref[0] = r[:, 0:QA_W].T.astype(BF16)
    ka_ref[0] = r[:, QA_W:EVEN_ROPED].astype(BF16)
    qb_ref[0] = yn[:, EVEN_ROPED:EVEN_ROPED + QKVB_W].T.astype(BF16)
    kb_ref[0] = yn[:, EVEN_ROPED + QKVB_W:EVEN_NORMED].astype(BF16)
    vat_ref[0] = y[:, EVEN_NORMED:EVEN_NORMED + KVA_W].T.astype(BF16)
    vb_ref[0] = y[:, EVEN_NORMED + KVA_W:].T.astype(BF16)


def _even_proj(ctx, x, mods, g, w, gains, e64, cos, sin):
    b, s, d = x.shape
    n = CTX_LEN + s
    tok = lambda wd: (pl.BlockSpec((1, TM, wd), lambda bb, i: (bb, i, 0)),
                      jax.ShapeDtypeStruct((b, n, wd), BF16))
    tr = lambda wd: (pl.BlockSpec((1, wd, TM), lambda bb, i: (bb, 0, i)),
                     jax.ShapeDtypeStruct((b, wd, n), BF16))
    outs = (tr(QA_W), tok(KVA_W), tr(QKVB_W), tok(QKVB_W), tr(KVA_W), tr(QKVB_W))
    return pl.pallas_call(
        _even_proj_kernel,
        grid=(b, n // TM),
        in_specs=_joint_specs(d) + [
            _mod_spec(0), _mod_spec(1),
            _const_spec((1, d)),
            _const_spec((d, EVEN_COLS)),
            _const_spec((1, EVEN_NORMED)),
            _const_spec((MXU_DIM, MXU_DIM)),
            pl.BlockSpec((TM, LANES), lambda bb, i: (i, 0)),
            pl.BlockSpec((TM, LANES), lambda bb, i: (i, 0)),
        ],
        out_specs=[o[0] for o in outs],
        out_shape=[o[1] for o in outs],
        compiler_params=_params("arbitrary", "arbitrary"),
        name="even_proj",
    )(ctx, x, mods, mods, g, w, gains, e64, cos, sin)


GQA_R = A_HEADS // A_KV_HEADS
WIN_BAND = 3 * A_BLOCK
WIN_LANES = GQA_R * A_BLOCK


def _window_kernel(qt_ref, k_ref, vt_ref, sink_ref, o_ref, s_ref, mx_ref, qs_ref, *, seq):
    g = pl.program_id(1)
    nb = seq // A_BLOCK
    nkeys = WIN_BAND + CTX_LEN
    assert nb % 2 == 0
    sink = sink_ref[0]
    ones_loc = jnp.ones((SUM_ROWS, nkeys), BF16)
    ones_ctx = jnp.ones((SUM_ROWS, CTX_LEN), BF16)
    rel = (lax.broadcasted_iota(jnp.int32, (WIN_BAND, WIN_LANES), 1) % A_BLOCK
           - lax.broadcasted_iota(jnp.int32, (WIN_BAND, WIN_LANES), 0))

    def prep_q(col0, qslot):
        qt = qt_ref[0, :, pl.ds(col0, A_BLOCK)]
        zero = jnp.zeros((HEAD_DIM, A_BLOCK), BF16)
        for h in range(GQA_R):
            qh = qt[h * HEAD_DIM:(h + 1) * HEAD_DIM]
            lanes = slice(h * A_BLOCK, (h + 1) * A_BLOCK)
            qs_ref[qslot, 0:HEAD_DIM, lanes] = jnp.where(g == 0, qh, zero)
            qs_ref[qslot, HEAD_DIM:, lanes] = jnp.where(g == 0, zero, qh)

    def band_start(n):
        return pl.multiple_of(jnp.clip((n - 1) * A_BLOCK, 0, seq - WIN_BAND), A_BLOCK)

    def scores(qslot, n, sslot):
        start = band_start(n)
        keys = jnp.concatenate([k_ref[0, pl.ds(CTX_LEN + start, WIN_BAND), :], k_ref[0, 0:CTX_LEN, :]], axis=0)
        st = _dot(keys, qs_ref[qslot])
        diff = rel + (n * A_BLOCK - start)
        s_loc = jnp.where(jnp.abs(diff) <= A_WINDOW, st[:WIN_BAND], NEG_INF)
        s_ctx = st[WIN_BAND:]
        s_ref[sslot, 0:WIN_BAND] = s_loc
        s_ref[sslot, WIN_BAND:] = s_ctx
        mx_ref[sslot] = jnp.maximum(jnp.maximum(jnp.max(s_loc, axis=0, keepdims=True),
                                                jnp.max(s_ctx, axis=0, keepdims=True)), sink)

    def write(res, m, col0):
        o = res[0:HEAD_DIM] / (res[HEAD_DIM:HEAD_DIM + 1] + jnp.exp2(sink - m))
        for h in range(GQA_R):
            o_ref[0, h * HEAD_DIM:(h + 1) * HEAD_DIM, pl.ds(col0, A_BLOCK)] = (
                o[:, h * A_BLOCK:(h + 1) * A_BLOCK].astype(o_ref.dtype))

    def update(n, sslot):
        start = band_start(n)
        m = mx_ref[sslot]
        pt = jnp.exp2(s_ref[sslot] - m).astype(BF16)
        va = jnp.concatenate([vt_ref[0, :, pl.ds(CTX_LEN + start, WIN_BAND)], vt_ref[0, :, 0:CTX_LEN]], axis=1)
        res = _dot(jnp.concatenate([va, ones_loc], axis=0), pt)
        write(res, m, pl.multiple_of(CTX_LEN + n * A_BLOCK, A_BLOCK))

    for blk in range(CTX_LEN // A_BLOCK):
        prep_q(blk * A_BLOCK, 0)
        st = _dot(k_ref[0, 0:CTX_LEN, :], qs_ref[0])
        m = jnp.maximum(jnp.max(st, axis=0, keepdims=True), sink)
        pt = jnp.exp2(st - m).astype(BF16)
        res = _dot(jnp.concatenate([vt_ref[0, :, 0:CTX_LEN], ones_ctx], axis=0), pt)
        write(res, m, blk * A_BLOCK)

    def two_blocks(t2, carry):
        n1 = 2 * t2 + 1
        n2 = jnp.minimum(2 * t2 + 2, nb - 1)
        prep_q(pl.multiple_of(CTX_LEN + n1 * A_BLOCK, A_BLOCK), 1)
        scores(1, n1, 1)
        update(2 * t2, 0)
        prep_q(pl.multiple_of(CTX_LEN + n2 * A_BLOCK, A_BLOCK), 0)
        scores(0, n2, 0)
        update(n1, 1)
        return carry

    prep_q(CTX_LEN, 0)
    scores(0, 0, 0)
    lax.fori_loop(0, nb // 2, two_blocks, 0)


def _window_attn(qat, ka, vat, sink_row):
    b, _, n = qat.shape
    return pl.pallas_call(
        functools.partial(_window_kernel, seq=n - CTX_LEN),
        grid=(b, A_KV_HEADS),
        scratch_shapes=[pltpu.VMEM((2, WIN_BAND + CTX_LEN, WIN_LANES), F32),
                        pltpu.VMEM((2, 1, WIN_LANES), F32),
                        pltpu.VMEM((2, 2 * HEAD_DIM, WIN_LANES), BF16)],
        in_specs=[
            pl.BlockSpec((1, GQA_R * HEAD_DIM, n), lambda bb, g: (bb, g, 0)),
            pl.BlockSpec((1, n, KVA_W), lambda bb, g: (bb, 0, 0)),
            pl.BlockSpec((1, HEAD_DIM, n), lambda bb, g: (bb, g, 0)),
            pl.BlockSpec((1, 1, WIN_LANES), lambda bb, g: (g, 0, 0)),
        ],
        out_specs=pl.BlockSpec((1, GQA_R * HEAD_DIM, n), lambda bb, g: (bb, g, 0)),
        out_shape=jax.ShapeDtypeStruct((b, QA_W, n), BF16),
        compiler_params=_params("arbitrary", "arbitrary"),
        name="window_attn",
    )(qat, ka, vat, sink_row)


NA_ITEM_ROWS = LANES // GRID_W
NA_WIN_ROWS = NA_ROWS + NA_ITEM_ROWS
NA_HG = 4
NA_W = NA_HG * HEAD_DIM
NA_LANES = NA_HG * LANES
NA_BIAS_N = 2 * NA_ROWS


def _na_kernel(qt_ref, k_ref, vt_ref, ta_ref, tb_ref, o_ref, s_ref, mx_ref, qs_ref, *, rows_n):
    nitems = rows_n // NA_ITEM_ROWS
    nloc = NA_WIN_ROWS * GRID_W
    assert nitems % 2 == 0 and (rows_n - NA_WIN_ROWS) % 2 == 0 and nloc % LANES == 0
    ones_loc = jnp.ones((SUM_ROWS, nloc + CTX_LEN), BF16)
    ones_ctx = jnp.ones((SUM_ROWS, CTX_LEN), BF16)

    def prep_q(col0, qslot):
        qt = qt_ref[0, :, pl.ds(col0, LANES)]
        for h in range(NA_HG):
            rows = slice(h * HEAD_DIM, (h + 1) * HEAD_DIM)
            qs_ref[qslot, rows, h * LANES:(h + 1) * LANES] = qt[rows]

    def win_start(j):
        return jnp.clip(NA_ITEM_ROWS * j - NA_ROWS // 2, 0, rows_n - NA_WIN_ROWS)

    def scores(qslot, j, sslot):
        ru = win_start(j)
        k0 = pl.multiple_of(CTX_LEN + ru * GRID_W, LANES)
        keys = jnp.concatenate([k_ref[0, pl.ds(k0, nloc), :], k_ref[0, 0:CTX_LEN, :]], axis=0)
        st = _dot(keys, qs_ref[qslot])
        mx = jnp.max(st[nloc:], axis=0, keepdims=True)
        s_ref[sslot, nloc:] = st[nloc:]
        for i in range(NA_WIN_ROWS):
            idx = []
            for e in range(NA_ITEM_ROWS):
                r = NA_ITEM_ROWS * j + e
                r0 = jnp.clip(r - NA_ROWS // 2, 0, rows_n - NA_ROWS)
                seen = (ru + i >= r0) & (ru + i < r0 + NA_ROWS)
                idx.append(jnp.where(seen, ru + i - r + NA_ROWS - 1, NA_BIAS_N - 1))
            rows = slice(i * GRID_W, (i + 1) * GRID_W)
            blk = st[rows] + ta_ref[0, idx[0]] + tb_ref[0, idx[1]]
            s_ref[sslot, rows] = blk
            mx = jnp.maximum(mx, jnp.max(blk, axis=0, keepdims=True))
        mx_ref[sslot] = mx

    def write(res, col0):
        for h in range(NA_HG):
            lanes = slice(h * LANES, (h + 1) * LANES)
            o = res[h * HEAD_DIM:(h + 1) * HEAD_DIM, lanes] / res[NA_W:NA_W + 1, lanes]
            o_ref[0, h * HEAD_DIM:(h + 1) * HEAD_DIM, pl.ds(col0, LANES)] = o.astype(o_ref.dtype)

    def update(j, sslot):
        k0 = pl.multiple_of(CTX_LEN + win_start(j) * GRID_W, LANES)
        pt = jnp.exp2(s_ref[sslot] - mx_ref[sslot]).astype(BF16)
        va = jnp.concatenate([vt_ref[0, :, pl.ds(k0, nloc)], vt_ref[0, :, 0:CTX_LEN]], axis=1)
        res = _dot(jnp.concatenate([va, ones_loc], axis=0), pt)
        write(res, pl.multiple_of(CTX_LEN + j * LANES, LANES))

    qs_ref[...] = jnp.zeros(qs_ref.shape, BF16)
    for blk in range(CTX_LEN // LANES):
        prep_q(blk * LANES, 0)
        st = _dot(k_ref[0, 0:CTX_LEN, :], qs_ref[0])
        pt = jnp.exp2(st - jnp.max(st, axis=0, keepdims=True)).astype(BF16)
        write(_dot(jnp.concatenate([vt_ref[0, :, 0:CTX_LEN], ones_ctx], axis=0), pt), blk * LANES)

    def two_items(t2, carry):
        j1 = 2 * t2 + 1
        j2 = jnp.minimum(2 * t2 + 2, nitems - 1)
        prep_q(pl.multiple_of(CTX_LEN + j1 * LANES, LANES), 1)
        scores(1, j1, 1)
        update(2 * t2, 0)
        prep_q(pl.multiple_of(CTX_LEN + j2 * LANES, LANES), 0)
        scores(0, j2, 0)
        update(j1, 1)
        return carry

    prep_q(CTX_LEN, 0)
    scores(0, 0, 0)
    lax.fori_loop(0, nitems // 2, two_items, 0)


def _na_attn(qbt, kb, vbt, ta, tb):
    b, _, n = qbt.shape
    rows_n = (n - CTX_LEN) // GRID_W
    assert rows_n >= NA_WIN_ROWS
    nkeys = NA_WIN_ROWS * GRID_W + CTX_LEN
    tspec = pl.BlockSpec((1, NA_BIAS_N, GRID_W, NA_LANES), lambda bb, hg: (hg, 0, 0, 0))
    return pl.pallas_call(
        functools.partial(_na_kernel, rows_n=rows_n),
        grid=(b, B_HEADS // NA_HG),
        scratch_shapes=[pltpu.VMEM((2, nkeys, NA_LANES), F32),
                        pltpu.VMEM((2, 1, NA_LANES), F32),
                        pltpu.VMEM((2, NA_W, NA_LANES), BF16)],
        in_specs=[
            pl.BlockSpec((1, NA_W, n), lambda bb, hg: (bb, hg, 0)),
            pl.BlockSpec((1, n, NA_W), lambda bb, hg: (bb, 0, hg)),
            pl.BlockSpec((1, NA_W, n), lambda bb, hg: (bb, hg, 0)),
            tspec, tspec,
        ],
        out_specs=pl.BlockSpec((1, NA_W, n), lambda bb, hg: (bb, hg, 0)),
        out_shape=jax.ShapeDtypeStruct((b, QKVB_W, n), BF16),
        compiler_params=_params("arbitrary", "arbitrary"),
        name="na_attn",
    )(qbt, kb, vbt, ta, tb)


def _mlp_tail(x, y, g1, sh2, sc2, g2, gn, w1_ref, w2_ref):
    x1 = x + g1 * y
    h = _norm_mod(x1, gn, sh2, sc2).astype(BF16)
    mlp = None
    for j in range(D_FF // D_MODEL):
        cols = slice(j * D_MODEL, (j + 1) * D_MODEL)
        u = jnp.maximum(_dot(h, w1_ref[:, cols]), 0.0)
        part = _dot((u * u).astype(BF16), w2_ref[cols, :])
        mlp = part if mlp is None else mlp + part
    return x1 + g2 * mlp


def _even_out_kernel(ctx_ref, x_ref, oat_ref, obt_ref, g1_ref, sh2_ref, sc2_ref, g2_ref, gn_ref,
                     woa_ref, wob_ref, w1_ref, w2_ref, o_ref):
    oa = oat_ref[0].astype(F32).T.astype(BF16)
    ob = obt_ref[0].astype(F32).T.astype(BF16)
    y = _dot(oa, woa_ref[...]) + _dot(ob, wob_ref[...])
    o_ref[0] = _mlp_tail(_joint_tile(ctx_ref, x_ref), y, g1_ref[0], sh2_ref[0], sc2_ref[0], g2_ref[0],
                         gn_ref[...], w1_ref, w2_ref)


def _even_out(ctx, x, oa, ob, mods, gn, woa, wob, w1, w2):
    b, s, d = x.shape
    n = CTX_LEN + s
    return pl.pallas_call(
        _even_out_kernel,
        grid=(b, n // TM),
        in_specs=_joint_specs(d) + [
            pl.BlockSpec((1, QA_W, TM), lambda bb, i: (bb, 0, i)),
            pl.BlockSpec((1, QKVB_W, TM), lambda bb, i: (bb, 0, i)),
            _mod_spec(2), _mod_spec(3), _mod_spec(4), _mod_spec(5),
            _const_spec((1, d)),
            _const_spec((QA_W, d)), _const_spec((QKVB_W, d)),
            _const_spec((d, D_FF)), _const_spec((D_FF, d)),
        ],
        out_specs=pl.BlockSpec((1, TM, d), lambda bb, i: (bb, i, 0)),
        out_shape=jax.ShapeDtypeStruct((b, n, d), F32),
        compiler_params=_params("arbitrary", "arbitrary"),
        name="even_out_mlp",
    )(ctx, x, oa, ob, mods, mods, mods, mods, gn, woa, wob, w1, w2)


def _odd_out_kernel(x_ref, ot_ref, g1_ref, sh2_ref, sc2_ref, g2_ref, gn_ref,
                    wo_ref, w1_ref, w2_ref, o_ref):
    o = ot_ref[0].astype(F32).T.astype(BF16)
    y = _dot(o, wo_ref[...])
    o_ref[0] = _mlp_tail(x_ref[0], y, g1_ref[0], sh2_ref[0], sc2_ref[0], g2_ref[0], gn_ref[...],
                         w1_ref, w2_ref)


def _odd_out(xa, ot, mods, gn, wo, w1, w2):
    b, n, d = xa.shape
    s = n - CTX_LEN
    return pl.pallas_call(
        _odd_out_kernel,
        grid=(b, s // TM),
        in_specs=[
            pl.BlockSpec((1, TM, d), lambda bb, i: (bb, i + 1, 0)),
            pl.BlockSpec((1, d, TM), lambda bb, i: (bb, 0, i)),
            _mod_spec(2, True), _mod_spec(3, True), _mod_spec(4, True), _mod_spec(5, True),
            _const_spec((1, d)),
            _const_spec((d, d)),
            _const_spec((d, D_FF)), _const_spec((D_FF, d)),
        ],
        out_specs=pl.BlockSpec((1, TM, d), lambda bb, i: (bb, i, 0)),
        out_shape=jax.ShapeDtypeStruct((b, s, d), F32),
        compiler_params=_params("arbitrary", "arbitrary"),
        name="odd_out_mlp",
    )(xa, ot, mods, mods, mods, mods, gn, wo, w1, w2)


ODD_IN_COLS = C_Q_RANK + C_KV_RANK + LANES
Q_NOPE_W = C_HEADS * C_NOPE
Q_ROPE_W = C_HEADS * C_ROPE


def _odd_proj_kernel(x_ref, sh_ref, sc_ref, g_ref, win_ref, qag_ref, kvg_ref, wuq_ref, wukv_ref,
                     gq_ref, gk_ref, e64_ref, e32_ref, cos_ref, sin_ref,
                     qn_ref, qr_ref, kn_ref, kr_ref, vt_ref):
    h = _norm_mod(x_ref[0], g_ref[...], sh_ref[0], sc_ref[0]).astype(BF16)
    y = _dot(h, win_ref[...])
    cq = y[:, :C_Q_RANK]
    ckv = y[:, C_Q_RANK:C_Q_RANK + C_KV_RANK]
    kr = y[:, C_Q_RANK + C_KV_RANK:]

    def rms(t, g):
        return (t * lax.rsqrt(jnp.mean(t * t, axis=-1, keepdims=True) + NORM_EPS) * g).astype(BF16)

    q = _dot(rms(cq, qag_ref[...]), wuq_ref[...])
    kv = _dot(rms(ckv, kvg_ref[...]), wukv_ref[...])
    gq = gq_ref[...]
    gk = gk_ref[...]
    cos = cos_ref[...]
    sin = sin_ref[...]
    qn = _group_rms(q[:, :Q_NOPE_W], e64_ref[...], C_NOPE) * gq[:, :Q_NOPE_W]
    qn_ref[0] = qn.T.astype(BF16)
    qr = _group_rms(q[:, Q_NOPE_W:], e32_ref[...], C_ROPE) * gq[:, Q_NOPE_W:]
    qr_ref[0] = _rope(qr, cos, sin, C_ROPE // 4).T.astype(BF16)
    kn_ref[0] = (_group_rms(kv[:, :Q_NOPE_W], e64_ref[...], C_NOPE) * gk[:, :Q_NOPE_W]).astype(BF16)
    krn = _group_rms(kr, e32_ref[...], C_ROPE) * gk[:, Q_NOPE_W:]
    kr_ref[0] = _rope(krn, cos, sin, C_ROPE // 4).astype(BF16)
    vt_ref[0] = kv[:, Q_NOPE_W:].T.astype(BF16)


def _odd_proj(xa, mods, g, win, qag, kvg, wuq, wukv, gq, gk, e64, e32, cos, sin):
    b, n, d = xa.shape
    tok = lambda wd: pl.BlockSpec((1, TM, wd), lambda bb, i: (bb, i, 0))
    return pl.pallas_call(
        _odd_proj_kernel,
        grid=(b, n // TM),
        in_specs=[
            tok(d), _mod_spec(0), _mod_spec(1),
            _const_spec((1, d)),
            _const_spec((d, ODD_IN_COLS)),
            _const_spec((1, C_Q_RANK)), _const_spec((1, C_KV_RANK)),
            _const_spec((C_Q_RANK, Q_NOPE_W + Q_ROPE_W)),
            _const_spec((C_KV_RANK, 2 * Q_NOPE_W)),
            _const_spec((1, Q_NOPE_W + Q_ROPE_W)), _const_spec((1, Q_NOPE_W + LANES)),
            _const_spec((MXU_DIM, MXU_DIM)), _const_spec((MXU_DIM, MXU_DIM)),
            pl.BlockSpec((TM, LANES), lambda bb, i: (i, 0)),
            pl.BlockSpec((TM, LANES), lambda bb, i: (i, 0)),
        ],
        out_specs=[pl.BlockSpec((1, Q_NOPE_W, TM), lambda bb, i: (bb, 0, i)),
                   pl.BlockSpec((1, Q_ROPE_W, TM), lambda bb, i: (bb, 0, i)),
                   tok(Q_NOPE_W), tok(LANES),
                   pl.BlockSpec((1, Q_NOPE_W, TM), lambda bb, i: (bb, 0, i))],
        out_shape=[jax.ShapeDtypeStruct((b, Q_NOPE_W, n), BF16),
                   jax.ShapeDtypeStruct((b, Q_ROPE_W, n), BF16),
                   jax.ShapeDtypeStruct((b, n, Q_NOPE_W), BF16),
                   jax.ShapeDtypeStruct((b, n, LANES), BF16),
                   jax.ShapeDtypeStruct((b, Q_NOPE_W, n), BF16)],
        compiler_params=_params("arbitrary", "arbitrary"),
        name="odd_proj",
    )(xa, mods, mods, g, win, qag, kvg, wuq, wukv, gq, gk, e64, e32, cos, sin)


SUM_ROWS = 16


def _kv_tile(n):
    for t in (768, 256):
        if n % t == 0 and (n // t) % 2 == 1:
            return t
    raise ValueError(f"joint sequence length {n} has no odd split into 256-multiples")


def _flash_kernel(qnt_ref, qrt_ref, kn_ref, kr_ref, vt_ref, o_ref,
                  s_ref, mx_ref, m_ref, acc_ref, qt_ref, *, tk, tq):
    p = pl.program_id(1)
    n = kn_ref.shape[1]
    nq = (n - CTX_LEN) // tq
    nch = n // tk
    assert nq % 2 == 0 and nch % 2 == 1
    ones = jnp.ones((SUM_ROWS, tk), BF16)
    row = lax.broadcasted_iota(jnp.int32, (LANES, tq), 0)

    def prep_q(qi, qslot):
        c0 = pl.multiple_of(CTX_LEN + qi * tq, tq)
        qn = qnt_ref[0, :, pl.ds(c0, tq)]
        qr = qrt_ref[0, :, pl.ds(c0, tq)]
        zero = jnp.zeros_like(qn)
        for hh in range(2):
            off = ((2 * p + hh) % 4) * C_ROPE
            keep_n = (row < C_NOPE) if hh == 0 else (row >= C_NOPE)
            keep_r = (row >= off) & (row < off + C_ROPE)
            qt_ref[qslot, 0:LANES, hh * tq:(hh + 1) * tq] = jnp.where(keep_n, qn, zero)
            qt_ref[qslot, LANES:, hh * tq:(hh + 1) * tq] = jnp.where(keep_r, qr, zero)

    def scores(qslot, j, sslot):
        k0 = j * tk
        kcat = jnp.concatenate([kn_ref[0, pl.ds(k0, tk), :], kr_ref[0, pl.ds(k0, tk), :]], axis=1)
        st = _dot(kcat, qt_ref[qslot])
        s_ref[sslot] = st
        mx_ref[sslot] = jnp.max(st, axis=0, keepdims=True)

    def update(j, sslot):
        k0 = j * tk
        m_old = m_ref[...]
        m_new = jnp.maximum(m_old, mx_ref[sslot])
        alpha = jnp.exp2(m_old - m_new)
        pt = jnp.exp2(s_ref[sslot] - m_new).astype(BF16)
        va = jnp.concatenate([vt_ref[0, :, pl.ds(k0, tk)], ones], axis=0)
        m_ref[...] = m_new
        acc_ref[...] = alpha * acc_ref[...] + _dot(va, pt)

    def reset():
        m_ref[...] = jnp.full(m_ref.shape, NEG_INF, F32)
        acc_ref[...] = jnp.zeros(acc_ref.shape, F32)

    def finish(qi):
        acc = acc_ref[...]
        c0 = pl.multiple_of(qi * tq, tq)
        o_ref[0, 0:C_NOPE, pl.ds(c0, tq)] = (acc[0:C_NOPE, :tq] / acc[LANES:LANES + 1, :tq]).astype(o_ref.dtype)
        o_ref[0, C_NOPE:, pl.ds(c0, tq)] = (acc[C_NOPE:LANES, tq:] / acc[LANES:LANES + 1, tq:]).astype(o_ref.dtype)
        reset()

    def run_tile(qi, slot, next_qi):
        other = 1 - slot
        for j in range(nch - 1):
            scores(slot, j + 1, other if j % 2 == 0 else slot)
            update(j, slot if j % 2 == 0 else other)
        prep_q(next_qi, other)
        scores(other, 0, other)
        update(nch - 1, slot)
        finish(qi)

    def two_tiles(t2, carry):
        run_tile(2 * t2, 0, 2 * t2 + 1)
        run_tile(2 * t2 + 1, 1, jnp.minimum(2 * t2 + 2, nq - 1))
        return carry

    reset()
    prep_q(0, 0)
    scores(0, 0, 0)
    lax.fori_loop(0, nq // 2, two_tiles, 0)


def _flash_attn(qnt, qrt, kn, kr, vt):
    b, n, _ = kn.shape
    s = n - CTX_LEN
    tq = TM
    tk = _kv_tile(n)
    return pl.pallas_call(
        functools.partial(_flash_kernel, tk=tk, tq=tq),
        grid=(b, C_HEADS // 2),
        scratch_shapes=[pltpu.VMEM((2, tk, 2 * tq), F32),
                        pltpu.VMEM((2, 1, 2 * tq), F32),
                        pltpu.VMEM((1, 2 * tq), F32),
                        pltpu.VMEM((LANES + SUM_ROWS, 2 * tq), F32),
                        pltpu.VMEM((2, 2 * LANES, 2 * tq), BF16)],
        in_specs=[
            pl.BlockSpec((1, LANES, n), lambda bb, p: (bb, p, 0)),
            pl.BlockSpec((1, LANES, n), lambda bb, p: (bb, p // 2, 0)),
            pl.BlockSpec((1, n, LANES), lambda bb, p: (bb, 0, p)),
            pl.BlockSpec((1, n, LANES), lambda bb, p: (bb, 0, 0)),
            pl.BlockSpec((1, LANES, n), lambda bb, p: (bb, p, 0)),
        ],
        out_specs=pl.BlockSpec((1, LANES, s), lambda bb, p: (bb, p, 0)),
        out_shape=jax.ShapeDtypeStruct((b, C_HEADS * C_NOPE, s), BF16),
        compiler_params=_params("arbitrary", "arbitrary"),
        name="mla_flash",
    )(qnt, qrt, kn, kr, vt)


def _rope_tables(seq, dim):
    t = jnp.arange(seq, dtype=jnp.int32)
    pos = jnp.stack([t // GRID_W, t % GRID_W], axis=0).astype(F32)
    half = dim // 2
    q = half // 2
    inv = ROPE_THETA ** (-jnp.arange(q, dtype=F32) / q)
    j = jnp.arange(dim)
    ang = pos[j // half].T * inv[j % q][None, :]
    sign = jnp.where((j % half) < q, -1.0, 1.0).astype(F32)
    cos = jnp.cos(ang)
    sin = jnp.sin(ang) * sign[None, :]
    reps = LANES // dim
    cos = jnp.tile(cos, (1, reps))
    sin = jnp.tile(sin, (1, reps))
    cos = jnp.concatenate([jnp.ones((CTX_LEN, LANES), F32), cos], axis=0)
    sin = jnp.concatenate([jnp.zeros((CTX_LEN, LANES), F32), sin], axis=0)
    return cos, sin


def _group_ones(group):
    r = jnp.arange(MXU_DIM) // group
    return (r[:, None] == r[None, :]).astype(BF16)


def _na_bias_tables(rpb, mult):
    cq = jnp.arange(GRID_W)
    c0 = jnp.clip(cq - NA_COLS // 2, 0, GRID_W - NA_COLS)
    col_ok = (cq[:, None] >= c0[None, :]) & (cq[:, None] < c0[None, :] + NA_COLS)
    dci = jnp.clip(cq[:, None] - cq[None, :], 1 - NA_COLS, NA_COLS - 1) + NA_COLS - 1
    pick = (dci[None] == jnp.arange(2 * NA_COLS - 1)[:, None, None]).astype(F32)
    tt = jnp.einsum("hdm,mkq->hdkq", rpb.astype(F32) * mult, pick, precision=lax.Precision.HIGHEST)
    tt = jnp.where(col_ok[None, None], tt, NEG_INF)
    tt = jnp.concatenate([tt, jnp.full_like(tt[:, :1], NEG_INF)], axis=1)
    tt = tt.reshape(B_HEADS // NA_HG, NA_HG, NA_BIAS_N, GRID_W, GRID_W).transpose(0, 2, 3, 1, 4)
    zero = jnp.zeros_like(tt)
    shape = (B_HEADS // NA_HG, NA_BIAS_N, GRID_W, NA_LANES)
    return (jnp.stack([tt, zero], axis=4).reshape(shape), jnp.stack([zero, tt], axis=4).reshape(shape))


def kernel(x, c, ctx, c_ctx, ada_w, ada_b, norm_mix, norm_mlp, mlp_w1, mlp_w2, e_w_in, e_w_out, a_q_norm, a_k_norm, a_sink, b_q_norm, b_k_norm, b_rpb, o_w_in, o_qa_norm, o_kva_norm, o_w_uq, o_w_ukv, o_qn_nope, o_qn_rope, o_kn_nope, o_kn_rope, o_w_out):
    bsz, seq, d = x.shape
    assert d == D_MODEL and ctx.shape[1] == CTX_LEN and seq % TM == 0 and ada_w.shape[0] == 2
    assert bsz + 1 <= 8

    cond = jnp.zeros((8, d), F32).at[:bsz].set(c).at[bsz].set(c_ctx)
    m = _adaln(cond, ada_w, ada_b)
    mods = [jnp.stack([jnp.broadcast_to(m[i, bsz], (bsz, 6 * d)), m[i, :bsz]], axis=1).reshape(2 * bsz, 1, 6 * d)
            for i in range(2)]

    e64 = _group_ones(HEAD_DIM)
    e32 = _group_ones(C_ROPE)

    w = e_w_in[0]
    w_ext = jnp.concatenate([w[:, 0:512], w[:, 512:640], w[:, 768:1280], w[:, 1280:1792],
                             w[:, 640:768], w[:, 1792:2304]], axis=1).astype(BF16)
    scale = HEAD_DIM ** -0.5
    log2e = math.log2(math.e)
    gains = jnp.concatenate([jnp.tile(a_q_norm[0], A_HEADS) * (scale * log2e), jnp.tile(a_k_norm[0], A_KV_HEADS),
                             jnp.tile(b_q_norm[0], B_HEADS) * (scale * log2e), jnp.tile(b_k_norm[0], B_HEADS)])[None, :]
    cos64, sin64 = _rope_tables(seq, HEAD_DIM)
    qat, ka, qbt, kb, vat, vbt = _even_proj(ctx, x, mods[0], norm_mix[0][None, :], w_ext, gains, e64, cos64, sin64)
    sink_row = (jnp.repeat(a_sink[0].reshape(A_KV_HEADS, GQA_R), A_BLOCK, axis=1) * log2e)[:, None, :]
    oa = _window_attn(qat, ka, vat, sink_row)
    ob = _na_attn(qbt, kb, vbt, *_na_bias_tables(b_rpb[0], log2e))
    wo = e_w_out[0].astype(BF16)
    xa = _even_out(ctx, x, oa, ob, mods[0], norm_mlp[0][None, :], wo[:512], wo[512:],
                   mlp_w1[0].astype(BF16), mlp_w2[0].astype(BF16))

    wi = o_w_in[0]
    win = jnp.concatenate([wi[:, :C_Q_RANK + C_KV_RANK]] + [wi[:, C_Q_RANK + C_KV_RANK:]] * 4, axis=1).astype(BF16)
    wuq = o_w_uq[0].reshape(C_Q_RANK, C_HEADS, C_NOPE + C_ROPE)
    wuq = jnp.concatenate([wuq[:, :, :C_NOPE].reshape(C_Q_RANK, -1), wuq[:, :, C_NOPE:].reshape(C_Q_RANK, -1)],
                          axis=1).astype(BF16)
    wukv = o_w_ukv[0].reshape(C_KV_RANK, C_HEADS, 2 * C_NOPE)
    wukv = jnp.concatenate([wukv[:, :, :C_NOPE].reshape(C_KV_RANK, -1), wukv[:, :, C_NOPE:].reshape(C_KV_RANK, -1)],
                           axis=1).astype(BF16)
    qscale = (C_NOPE + C_ROPE) ** -0.5 * math.log2(math.e)
    gq = (jnp.concatenate([jnp.tile(o_qn_nope[0], C_HEADS), jnp.tile(o_qn_rope[0], C_HEADS)]) * qscale)[None, :]
    gk = jnp.concatenate([jnp.tile(o_kn_nope[0], C_HEADS), jnp.tile(o_kn_rope[0], LANES // C_ROPE)])[None, :]
    cos32, sin32 = _rope_tables(seq, C_ROPE)
    qn, qr, kn, kr, vt = _odd_proj(xa, mods[1], norm_mix[1][None, :], win, o_qa_norm[0][None, :],
                                   o_kva_norm[0][None, :], wuq, wukv, gq, gk, e64, e32, cos32, sin32)
    ot = _flash_attn(qn, qr, kn, kr, vt)
    return _odd_out(xa, ot, mods[1], norm_mlp[1][None, :], o_w_out[0].astype(BF16),
                    mlp_w1[1].astype(BF16), mlp_w2[1].astype(BF16))
```

```python
import functools
import math

import jax
import jax.numpy as jnp
from jax import lax
from jax.experimental import pallas as pl
from jax.experimental.pallas import tpu as pltpu

F32 = jnp.float32
BF16 = jnp.bfloat16

D_MODEL = 1024
CTX_LEN = 256
GRID_W = 64
HEAD_DIM = 64
A_HEADS = 8
A_KV_HEADS = 2
A_WINDOW = 128
A_BLOCK = 128
B_HEADS = 8
NA_ROWS = 8
NA_COLS = 16
C_HEADS = 16
C_Q_RANK = 384
C_KV_RANK = 256
C_NOPE = 64
C_ROPE = 32
D_FF = 4 * D_MODEL
ROPE_THETA = 10000.0
NORM_EPS = 1e-6
NEG_INF = -1e30

LANES = 128
MXU_DIM = 256
TM = CTX_LEN
VMEM_LIMIT = 56 * 1024 * 1024


def _dot(a, b):
    return jnp.dot(a, b, preferred_element_type=F32)


def _dot_nt(a, b):
    return lax.dot_general(a, b, (((1,), (1,)), ((), ())), preferred_element_type=F32)


def _params(*sem):
    return pltpu.CompilerParams(dimension_semantics=sem, vmem_limit_bytes=VMEM_LIMIT)


def _norm_mod(x, g, sh, sc):
    ms = jnp.mean(x * x, axis=-1, keepdims=True)
    return (x * lax.rsqrt(ms + NORM_EPS) * g) * (1.0 + sc) + sh


def _group_rms(y, e, group):
    out = []
    for c0 in range(0, y.shape[1], MXU_DIM):
        cw = min(MXU_DIM, y.shape[1] - c0)
        yc = y[:, c0:c0 + cw]
        ss = _dot((yc * yc).astype(BF16), e[:cw, :cw])
        out.append(yc * lax.rsqrt(ss * (1.0 / group) + NORM_EPS))
    return out[0] if len(out) == 1 else jnp.concatenate(out, axis=1)


def _rope(r, cos, sin, half):
    w = r.shape[1]
    reps = w // LANES
    if reps > 1:
        cos = jnp.concatenate([cos] * reps, axis=1)
        sin = jnp.concatenate([sin] * reps, axis=1)
    lane = lax.broadcasted_iota(jnp.int32, r.shape, 1)
    up = pltpu.roll(r, w - half, axis=1)
    dn = pltpu.roll(r, half, axis=1)
    sw = jnp.where((lane & half) == 0, up, dn)
    return r * cos + sw * sin


def _stack_heads(q, lane_lo):
    z = jnp.zeros_like(q)
    return jnp.concatenate([jnp.where(lane_lo, q, z), jnp.where(lane_lo, z, q)], axis=0)


def _softmax_pv(parts, extra=None):
    m = None
    for s, _ in parts:
        mx = jnp.max(s, axis=1, keepdims=True)
        m = mx if m is None else jnp.maximum(m, mx)
    if extra is not None:
        m = jnp.maximum(m, extra)
    den = None
    o = None
    for s, v in parts:
        p = jnp.exp(s - m)
        d = jnp.sum(p, axis=1, keepdims=True)
        pv = _dot(p.astype(BF16), v)
        den = d if den is None else den + d
        o = pv if o is None else o + pv
    if extra is not None:
        den = den + jnp.exp(extra - m)
    return o / den


def _ada_kernel(cond_ref, w_ref, b_ref, o_ref):
    c = cond_ref[...]
    s = (c * jax.nn.sigmoid(c)).astype(BF16)
    o_ref[0] = _dot(s, w_ref[0].astype(BF16)) + b_ref[0]


def _adaln(cond, ada_w, ada_b):
    depth, d, n6 = ada_w.shape
    tn = 1536
    return pl.pallas_call(
        _ada_kernel,
        grid=(depth, n6 // tn),
        in_specs=[
            pl.BlockSpec((8, d), lambda l, j: (0, 0)),
            pl.BlockSpec((1, d, tn), lambda l, j: (l, 0, j)),
            pl.BlockSpec((1, 1, tn), lambda l, j: (l, 0, j)),
        ],
        out_specs=pl.BlockSpec((1, 8, tn), lambda l, j: (l, 0, j)),
        out_shape=jax.ShapeDtypeStruct((depth, 8, n6), F32),
        compiler_params=_params("arbitrary", "arbitrary"),
        name="adaln",
    )(cond, ada_w, ada_b.reshape(depth, 1, n6))


NSUB = 3


def _mod_spec(chunk, lat_only=False):
    if lat_only:
        return pl.BlockSpec((1, 1, D_MODEL), lambda b, i: (2 * b + 1, 0, chunk))
    return pl.BlockSpec((2, 1, D_MODEL), lambda b, i: (b, 0, chunk))


def _pick_mod(ref, is_ctx):
    m = ref[...]
    return jnp.where(is_ctx, m[0], m[1])


def _is_ctx(k):
    return NSUB * pl.program_id(1) + k == 0


def _const_spec(shape):
    nd = len(shape)
    return pl.BlockSpec(shape, lambda *_: (0,) * nd, pipeline_mode=pl.Buffered(1))


EVEN_NORMED = 1664
EVEN_ROPED = 640
EVEN_COLS = 2304
QA_W = A_HEADS * HEAD_DIM
KVA_W = A_KV_HEADS * HEAD_DIM
QKVB_W = B_HEADS * HEAD_DIM


def _joint_specs(d):
    lat = lambda k: pl.BlockSpec((1, TM, d), lambda bb, i: (bb, jnp.maximum(NSUB * i + k - 1, 0), 0))
    return [pl.BlockSpec((1, CTX_LEN, d), lambda bb, i: (bb, 0, 0))] + [lat(k) for k in range(NSUB)]


def _joint_tile(ctx_ref, x_refs, k):
    return jnp.where(_is_ctx(k), ctx_ref[0], x_refs[k][0])


def _even_proj_kernel(ctx_ref, *refs):
    x_refs = refs[:NSUB]
    (sh_ref, sc_ref, g_ref, w_ref, gains_ref, e_ref, cos_ref, sin_ref,
     qat_ref, ka_ref, qb_ref, kb_ref, vat_ref, vb_ref) = refs[NSUB:]
    for k in range(NSUB):
        rows = slice(k * TM, (k + 1) * TM)
        h = _norm_mod(_joint_tile(ctx_ref, x_refs, k), g_ref[...],
                      _pick_mod(sh_ref, _is_ctx(k)), _pick_mod(sc_ref, _is_ctx(k))).astype(BF16)
        y = _dot(h, w_ref[...])
        yn = _group_rms(y[:, :EVEN_NORMED], e_ref[...], HEAD_DIM) * gains_ref[...]
        r = _rope(yn[:, :EVEN_ROPED], cos_ref[rows, :], sin_ref[rows, :], HEAD_DIM // 4)
        qat_ref[0, :, rows] = r[:, 0:QA_W].T.astype(BF16)
        ka_ref[0, rows, :] = r[:, QA_W:EVEN_ROPED].astype(BF16)
        qb_ref[0, :, rows] = yn[:, EVEN_ROPED:EVEN_ROPED + QKVB_W].T.astype(BF16)
        kb_ref[0, rows, :] = yn[:, EVEN_ROPED + QKVB_W:EVEN_NORMED].astype(BF16)
        vat_ref[0, :, rows] = y[:, EVEN_NORMED:EVEN_NORMED + KVA_W].T.astype(BF16)
        vb_ref[0, :, rows] = y[:, EVEN_NORMED + KVA_W:].T.astype(BF16)


def _even_proj(ctx, x, mods, g, w, gains, e64, cos, sin):
    b, s, d = x.shape
    n = CTX_LEN + s
    tmb = NSUB * TM
    assert n % tmb == 0
    tok = lambda wd: (pl.BlockSpec((1, tmb, wd), lambda bb, i: (bb, i, 0)),
                      jax.ShapeDtypeStruct((b, n, wd), BF16))
    tr = lambda wd: (pl.BlockSpec((1, wd, tmb), lambda bb, i: (bb, 0, i)),
                     jax.ShapeDtypeStruct((b, wd, n), BF16))
    outs = (tr(QA_W), tok(KVA_W), tr(QKVB_W), tok(QKVB_W), tr(KVA_W), tr(QKVB_W))
    return pl.pallas_call(
        _even_proj_kernel,
        grid=(b, n // tmb),
        in_specs=_joint_specs(d) + [
            _mod_spec(0), _mod_spec(1),
            _const_spec((1, d)),
            _const_spec((d, EVEN_COLS)),
            _const_spec((1, EVEN_NORMED)),
            _const_spec((MXU_DIM, MXU_DIM)),
            pl.BlockSpec((tmb, LANES), lambda bb, i: (i, 0)),
            pl.BlockSpec((tmb, LANES), lambda bb, i: (i, 0)),
        ],
        out_specs=[o[0] for o in outs],
        out_shape=[o[1] for o in outs],
        compiler_params=_params("arbitrary", "arbitrary"),
        name="even_proj",
    )(ctx, *([x] * NSUB), mods, mods, g, w, gains, e64, cos, sin)


GQA_R = A_HEADS // A_KV_HEADS
WIN_BAND = 3 * A_BLOCK
WIN_LANES = GQA_R * A_BLOCK


def _window_kernel(qt_ref, k_ref, vt_ref, sink_ref, o_ref, s_ref, mx_ref, qs_ref, *, seq):
    g = pl.program_id(1)
    nb = seq // A_BLOCK
    nkeys = WIN_BAND + CTX_LEN
    assert nb % 2 == 0
    sink = sink_ref[0]
    ones_loc = jnp.ones((SUM_ROWS, nkeys), BF16)
    ones_ctx = jnp.ones((SUM_ROWS, CTX_LEN), BF16)
    rel = (lax.broadcasted_iota(jnp.int32, (WIN_BAND, WIN_LANES), 1) % A_BLOCK
           - lax.broadcasted_iota(jnp.int32, (WIN_BAND, WIN_LANES), 0))

    def prep_q(col0, qslot):
        qt = qt_ref[0, :, pl.ds(col0, A_BLOCK)]
        zero = jnp.zeros((HEAD_DIM, A_BLOCK), BF16)
        for h in range(GQA_R):
            qh = qt[h * HEAD_DIM:(h + 1) * HEAD_DIM]
            lanes = slice(h * A_BLOCK, (h + 1) * A_BLOCK)
            qs_ref[qslot, 0:HEAD_DIM, lanes] = jnp.where(g == 0, qh, zero)
            qs_ref[qslot, HEAD_DIM:, lanes] = jnp.where(g == 0, zero, qh)

    def band_start(n):
        return pl.multiple_of(jnp.clip((n - 1) * A_BLOCK, 0, seq - WIN_BAND), A_BLOCK)

    def scores(qslot, n, sslot):
        start = band_start(n)
        keys = jnp.concatenate([k_ref[0, pl.ds(CTX_LEN + start, WIN_BAND), :], k_ref[0, 0:CTX_LEN, :]], axis=0)
        st = _dot(keys, qs_ref[qslot])
        diff = rel + (n * A_BLOCK - start)
        s_loc = jnp.where(jnp.abs(diff) <= A_WINDOW, st[:WIN_BAND], NEG_INF)
        s_ctx = st[WIN_BAND:]
        s_ref[sslot, 0:WIN_BAND] = s_loc
        s_ref[sslot, WIN_BAND:] = s_ctx
        mx_ref[sslot] = jnp.maximum(jnp.maximum(jnp.max(s_loc, axis=0, keepdims=True),
                                                jnp.max(s_ctx, axis=0, keepdims=True)), sink)

    def write(res, m, col0):
        o = res[0:HEAD_DIM] / (res[HEAD_DIM:HEAD_DIM + 1] + jnp.exp2(sink - m))
        for h in range(GQA_R):
            o_ref[0, h * HEAD_DIM:(h + 1) * HEAD_DIM, pl.ds(col0, A_BLOCK)] = (
                o[:, h * A_BLOCK:(h + 1) * A_BLOCK].astype(o_ref.dtype))

    def update(n, sslot):
        start = band_start(n)
        m = mx_ref[sslot]
        pt = jnp.exp2(s_ref[sslot] - m).astype(BF16)
        va = jnp.concatenate([vt_ref[0, :, pl.ds(CTX_LEN + start, WIN_BAND)], vt_ref[0, :, 0:CTX_LEN]], axis=1)
        res = _dot(jnp.concatenate([va, ones_loc], axis=0), pt)
        write(res, m, pl.multiple_of(CTX_LEN + n * A_BLOCK, A_BLOCK))

    for blk in range(CTX_LEN // A_BLOCK):
        prep_q(blk * A_BLOCK, 0)
        st = _dot(k_ref[0, 0:CTX_LEN, :], qs_ref[0])
        m = jnp.maximum(jnp.max(st, axis=0, keepdims=True), sink)
        pt = jnp.exp2(st - m).astype(BF16)
        res = _dot(jnp.concatenate([vt_ref[0, :, 0:CTX_LEN], ones_ctx], axis=0), pt)
        write(res, m, blk * A_BLOCK)

    def two_blocks(t2, carry):
        n1 = 2 * t2 + 1
        n2 = jnp.minimum(2 * t2 + 2, nb - 1)
        prep_q(pl.multiple_of(CTX_LEN + n1 * A_BLOCK, A_BLOCK), 1)
        scores(1, n1, 1)
        update(2 * t2, 0)
        prep_q(pl.multiple_of(CTX_LEN + n2 * A_BLOCK, A_BLOCK), 0)
        scores(0, n2, 0)
        update(n1, 1)
        return carry

    prep_q(CTX_LEN, 0)
    scores(0, 0, 0)
    lax.fori_loop(0, nb // 2, two_blocks, 0)


def _window_attn(qat, ka, vat, sink_row):
    b, _, n = qat.shape
    return pl.pallas_call(
        functools.partial(_window_kernel, seq=n - CTX_LEN),
        grid=(b, A_KV_HEADS),
        scratch_shapes=[pltpu.VMEM((2, WIN_BAND + CTX_LEN, WIN_LANES), F32),
                        pltpu.VMEM((2, 1, WIN_LANES), F32),
                        pltpu.VMEM((2, 2 * HEAD_DIM, WIN_LANES), BF16)],
        in_specs=[
            pl.BlockSpec((1, GQA_R * HEAD_DIM, n), lambda bb, g: (bb, g, 0)),
            pl.BlockSpec((1, n, KVA_W), lambda bb, g: (bb, 0, 0)),
            pl.BlockSpec((1, HEAD_DIM, n), lambda bb, g: (bb, g, 0)),
            pl.BlockSpec((1, 1, WIN_LANES), lambda bb, g: (g, 0, 0)),
        ],
        out_specs=pl.BlockSpec((1, GQA_R * HEAD_DIM, n), lambda bb, g: (bb, g, 0)),
        out_shape=jax.ShapeDtypeStruct((b, QA_W, n), BF16),
        compiler_params=_params("arbitrary", "arbitrary"),
        name="window_attn",
    )(qat, ka, vat, sink_row)


NA_ITEM_ROWS = LANES // GRID_W
NA_WIN_ROWS = NA_ROWS + NA_ITEM_ROWS
NA_HG = 4
NA_W = NA_HG * HEAD_DIM
NA_LANES = NA_HG * LANES
NA_BIAS_N = 2 * NA_ROWS


def _na_kernel(qt_ref, k_ref, vt_ref, ta_ref, tb_ref, o_ref, s_ref, mx_ref, qs_ref, *, rows_n):
    nitems = rows_n // NA_ITEM_ROWS
    nloc = NA_WIN_ROWS * GRID_W
    assert nitems % 2 == 0 and (rows_n - NA_WIN_ROWS) % 2 == 0 and nloc % LANES == 0
    ones_loc = jnp.ones((SUM_ROWS, nloc + CTX_LEN), BF16)
    ones_ctx = jnp.ones((SUM_ROWS, CTX_LEN), BF16)

    def prep_q(col0, qslot):
        qt = qt_ref[0, :, pl.ds(col0, LANES)]
        for h in range(NA_HG):
            rows = slice(h * HEAD_DIM, (h + 1) * HEAD_DIM)
            qs_ref[qslot, rows, h * LANES:(h + 1) * LANES] = qt[rows]

    def win_start(j):
        return jnp.clip(NA_ITEM_ROWS * j - NA_ROWS // 2, 0, rows_n - NA_WIN_ROWS)

    def scores(qslot, j, sslot):
        ru = win_start(j)
        k0 = pl.multiple_of(CTX_LEN + ru * GRID_W, LANES)
        keys = jnp.concatenate([k_ref[0, pl.ds(k0, nloc), :], k_ref[0, 0:CTX_LEN, :]], axis=0)
        st = _dot(keys, qs_ref[qslot])
        mx = jnp.max(st[nloc:], axis=0, keepdims=True)
        s_ref[sslot, nloc:] = st[nloc:]
        for i in range(NA_WIN_ROWS):
            idx = []
            for e in range(NA_ITEM_ROWS):
                r = NA_ITEM_ROWS * j + e
                r0 = jnp.clip(r - NA_ROWS // 2, 0, rows_n - NA_ROWS)
                seen = (ru + i >= r0) & (ru + i < r0 + NA_ROWS)
                idx.append(jnp.where(seen, ru + i - r + NA_ROWS - 1, NA_BIAS_N - 1))
            rows = slice(i * GRID_W, (i + 1) * GRID_W)
            bias = ta_ref[0, idx[0]] + tb_ref[0, idx[1]]
            blk = st[rows] + jnp.concatenate([bias[h] for h in range(NA_HG)], axis=1)
            s_ref[sslot, rows] = blk
            mx = jnp.maximum(mx, jnp.max(blk, axis=0, keepdims=True))
        mx_ref[sslot] = mx

    def write(res, col0):
        for h in range(NA_HG):
            lanes = slice(h * LANES, (h + 1) * LANES)
            o = res[h * HEAD_DIM:(h + 1) * HEAD_DIM, lanes] / res[NA_W:NA_W + 1, lanes]
            o_ref[0, h * HEAD_DIM:(h + 1) * HEAD_DIM, pl.ds(col0, LANES)] = o.astype(o_ref.dtype)

    def update(j, sslot):
        k0 = pl.multiple_of(CTX_LEN + win_start(j) * GRID_W, LANES)
        pt = jnp.exp2(s_ref[sslot] - mx_ref[sslot]).astype(BF16)
        va = jnp.concatenate([vt_ref[0, :, pl.ds(k0, nloc)], vt_ref[0, :, 0:CTX_LEN]], axis=1)
        res = _dot(jnp.concatenate([va, ones_loc], axis=0), pt)
        write(res, pl.multiple_of(CTX_LEN + j * LANES, LANES))

    qs_ref[...] = jnp.zeros(qs_ref.shape, BF16)
    for blk in range(CTX_LEN // LANES):
        prep_q(blk * LANES, 0)
        st = _dot(k_ref[0, 0:CTX_LEN, :], qs_ref[0])
        pt = jnp.exp2(st - jnp.max(st, axis=0, keepdims=True)).astype(BF16)
        write(_dot(jnp.concatenate([vt_ref[0, :, 0:CTX_LEN], ones_ctx], axis=0), pt), blk * LANES)

    def two_items(t2, carry):
        j1 = 2 * t2 + 1
        j2 = jnp.minimum(2 * t2 + 2, nitems - 1)
        prep_q(pl.multiple_of(CTX_LEN + j1 * LANES, LANES), 1)
        scores(1, j1, 1)
        update(2 * t2, 0)
        prep_q(pl.multiple_of(CTX_LEN + j2 * LANES, LANES), 0)
        scores(0, j2, 0)
        update(j1, 1)
        return carry

    prep_q(CTX_LEN, 0)
    scores(0, 0, 0)
    lax.fori_loop(0, nitems // 2, two_items, 0)


def _na_attn(qbt, kb, vbt, ta, tb):
    b, _, n = qbt.shape
    rows_n = (n - CTX_LEN) // GRID_W
    assert rows_n >= NA_WIN_ROWS
    nkeys = NA_WIN_ROWS * GRID_W + CTX_LEN
    tspec = pl.BlockSpec((1, NA_BIAS_N, NA_HG, GRID_W, LANES), lambda bb, hg: (hg, 0, 0, 0, 0))
    return pl.pallas_call(
        functools.partial(_na_kernel, rows_n=rows_n),
        grid=(b, B_HEADS // NA_HG),
        scratch_shapes=[pltpu.VMEM((2, nkeys, NA_LANES), F32),
                        pltpu.VMEM((2, 1, NA_LANES), F32),
                        pltpu.VMEM((2, NA_W, NA_LANES), BF16)],
        in_specs=[
            pl.BlockSpec((1, NA_W, n), lambda bb, hg: (bb, hg, 0)),
            pl.BlockSpec((1, n, NA_W), lambda bb, hg: (bb, 0, hg)),
            pl.BlockSpec((1, NA_W, n), lambda bb, hg: (bb, hg, 0)),
            tspec, tspec,
        ],
        out_specs=pl.BlockSpec((1, NA_W, n), lambda bb, hg: (bb, hg, 0)),
        out_shape=jax.ShapeDtypeStruct((b, QKVB_W, n), BF16),
        compiler_params=_params("arbitrary", "arbitrary"),
        name="na_attn",
    )(qbt, kb, vbt, ta, tb)


def _mlp_tail(xs, y, mods, gn, w1_ref, w2_ref):
    x1s = [x + m[0] * y[k * TM:(k + 1) * TM] for k, (x, m) in enumerate(zip(xs, mods))]
    h = jnp.concatenate([_norm_mod(x1, gn, m[1], m[2]).astype(BF16) for x1, m in zip(x1s, mods)], axis=0)
    mlp = None
    for j in range(D_FF // D_MODEL):
        cols = slice(j * D_MODEL, (j + 1) * D_MODEL)
        u = jnp.maximum(_dot(h, w1_ref[:, cols]), 0.0)
        part = _dot((u * u).astype(BF16), w2_ref[cols, :])
        mlp = part if mlp is None else mlp + part
    return [x1 + m[3] * mlp[k * TM:(k + 1) * TM] for k, (x1, m) in enumerate(zip(x1s, mods))]


def _even_out_kernel(ctx_ref, *refs):
    x_refs = refs[:NSUB]
    (oat_ref, obt_ref, g1_ref, sh2_ref, sc2_ref, g2_ref, gn_ref,
     woa_ref, wob_ref, w1_ref, w2_ref, o_ref) = refs[NSUB:]
    oa = oat_ref[0].astype(F32).T.astype(BF16)
    ob = obt_ref[0].astype(F32).T.astype(BF16)
    y = _dot(oa, woa_ref[...]) + _dot(ob, wob_ref[...])
    xs = [_joint_tile(ctx_ref, x_refs, k) for k in range(NSUB)]
    mods = [[_pick_mod(r, _is_ctx(k)) for r in (g1_ref, sh2_ref, sc2_ref, g2_ref)] for k in range(NSUB)]
    for k, out in enumerate(_mlp_tail(xs, y, mods, gn_ref[...], w1_ref, w2_ref)):
        o_ref[0, k * TM:(k + 1) * TM, :] = out


def _even_out(ctx, x, oa, ob, mods, gn, woa, wob, w1, w2):
    b, s, d = x.shape
    n = CTX_LEN + s
    tmb = NSUB * TM
    return pl.pallas_call(
        _even_out_kernel,
        grid=(b, n // tmb),
        in_specs=_joint_specs(d) + [
            pl.BlockSpec((1, QA_W, tmb), lambda bb, i: (bb, 0, i)),
            pl.BlockSpec((1, QKVB_W, tmb), lambda bb, i: (bb, 0, i)),
            _mod_spec(2), _mod_spec(3), _mod_spec(4), _mod_spec(5),
            _const_spec((1, d)),
            _const_spec((QA_W, d)), _const_spec((QKVB_W, d)),
            _const_spec((d, D_FF)), _const_spec((D_FF, d)),
        ],
        out_specs=pl.BlockSpec((1, tmb, d), lambda bb, i: (bb, i, 0)),
        out_shape=jax.ShapeDtypeStruct((b, n, d), F32),
        compiler_params=_params("arbitrary", "arbitrary"),
        name="even_out_mlp",
    )(ctx, *([x] * NSUB), oa, ob, mods, mods, mods, mods, gn, woa, wob, w1, w2)


NSUB_ODD = 2


def _odd_out_kernel(*refs):
    x_refs = refs[:NSUB_ODD]
    ot_ref, g1_ref, sh2_ref, sc2_ref, g2_ref, gn_ref, wo_ref, w1_ref, w2_ref, o_ref = refs[NSUB_ODD:]
    o = ot_ref[0].astype(F32).T.astype(BF16)
    y = _dot(o, wo_ref[...])
    mods = [[g1_ref[0], sh2_ref[0], sc2_ref[0], g2_ref[0]]] * NSUB_ODD
    for k, out in enumerate(_mlp_tail([r[0] for r in x_refs], y, mods, gn_ref[...], w1_ref, w2_ref)):
        o_ref[0, k * TM:(k + 1) * TM, :] = out


def _odd_out(xa, ot, mods, gn, wo, w1, w2):
    b, n, d = xa.shape
    s = n - CTX_LEN
    tmb = NSUB_ODD * TM
    assert s % tmb == 0
    lat = lambda k: pl.BlockSpec((1, TM, d), lambda bb, i: (bb, NSUB_ODD * i + k + 1, 0))
    return pl.pallas_call(
        _odd_out_kernel,
        grid=(b, s // tmb),
        in_specs=[lat(k) for k in range(NSUB_ODD)] + [
            pl.BlockSpec((1, d, tmb), lambda bb, i: (bb, 0, i)),
            _mod_spec(2, True), _mod_spec(3, True), _mod_spec(4, True), _mod_spec(5, True),
            _const_spec((1, d)),
            _const_spec((d, d)),
            _const_spec((d, D_FF)), _const_spec((D_FF, d)),
        ],
        out_specs=pl.BlockSpec((1, tmb, d), lambda bb, i: (bb, i, 0)),
        out_shape=jax.ShapeDtypeStruct((b, s, d), F32),
        compiler_params=_params("arbitrary", "arbitrary"),
        name="odd_out_mlp",
    )(*([xa] * NSUB_ODD), ot, mods, mods, mods, mods, gn, wo, w1, w2)


ODD_IN_COLS = C_Q_RANK + C_KV_RANK + LANES
Q_NOPE_W = C_HEADS * C_NOPE
Q_ROPE_W = C_HEADS * C_ROPE


def _odd_proj_kernel(x_ref, sh_ref, sc_ref, g_ref, win_ref, qag_ref, kvg_ref, wuq_ref, wukv_ref,
                     gq_ref, gk_ref, e64_ref, e32_ref, cos_ref, sin_ref,
                     qn_ref, qr_ref, kn_ref, kr_ref, vt_ref):
    def rms(t, g):
        return (t * lax.rsqrt(jnp.mean(t * t, axis=-1, keepdims=True) + NORM_EPS) * g).astype(BF16)

    gq = gq_ref[...]
    gk = gk_ref[...]
    for k in range(NSUB):
        rows = slice(k * TM, (k + 1) * TM)
        h = _norm_mod(x_ref[0, rows, :], g_ref[...],
                      _pick_mod(sh_ref, _is_ctx(k)), _pick_mod(sc_ref, _is_ctx(k))).astype(BF16)
        y = _dot(h, win_ref[...])
        cq = y[:, :C_Q_RANK]
        ckv = y[:, C_Q_RANK:C_Q_RANK + C_KV_RANK]
        kr = y[:, C_Q_RANK + C_KV_RANK:]
        q = _dot(rms(cq, qag_ref[...]), wuq_ref[...])
        kv = _dot(rms(ckv, kvg_ref[...]), wukv_ref[...])
        cos = cos_ref[rows, :]
        sin = sin_ref[rows, :]
        qn = _group_rms(q[:, :Q_NOPE_W], e64_ref[...], C_NOPE) * gq[:, :Q_NOPE_W]
        qn_ref[0, :, rows] = qn.T.astype(BF16)
        qr = _group_rms(q[:, Q_NOPE_W:], e32_ref[...], C_ROPE) * gq[:, Q_NOPE_W:]
        qr_ref[0, :, rows] = _rope(qr, cos, sin, C_ROPE // 4).T.astype(BF16)
        kn = _group_rms(kv[:, :Q_NOPE_W], e64_ref[...], C_NOPE) * gk[:, :Q_NOPE_W]
        kn_ref[0, rows, :] = kn.astype(BF16)
        krn = _group_rms(kr, e32_ref[...], C_ROPE) * gk[:, Q_NOPE_W:]
        kr_ref[0, rows, :] = _rope(krn, cos, sin, C_ROPE // 4).astype(BF16)
        vt_ref[0, :, rows] = kv[:, Q_NOPE_W:].T.astype(BF16)


def _odd_proj(xa, mods, g, win, qag, kvg, wuq, wukv, gq, gk, e64, e32, cos, sin):
    b, n, d = xa.shape
    tmb = NSUB * TM
    assert n % tmb == 0
    tok = lambda wd: pl.BlockSpec((1, tmb, wd), lambda bb, i: (bb, i, 0))
    return pl.pallas_call(
        _odd_proj_kernel,
        grid=(b, n // tmb),
        in_specs=[
            tok(d), _mod_spec(0), _mod_spec(1),
            _const_spec((1, d)),
            _const_spec((d, ODD_IN_COLS)),
            _const_spec((1, C_Q_RANK)), _const_spec((1, C_KV_RANK)),
            _const_spec((C_Q_RANK, Q_NOPE_W + Q_ROPE_W)),
            _const_spec((C_KV_RANK, 2 * Q_NOPE_W)),
            _const_spec((1, Q_NOPE_W + Q_ROPE_W)), _const_spec((1, Q_NOPE_W + LANES)),
            _const_spec((MXU_DIM, MXU_DIM)), _const_spec((MXU_DIM, MXU_DIM)),
            pl.BlockSpec((tmb, LANES), lambda bb, i: (i, 0)),
            pl.BlockSpec((tmb, LANES), lambda bb, i: (i, 0)),
        ],
        out_specs=[pl.BlockSpec((1, Q_NOPE_W, tmb), lambda bb, i: (bb, 0, i)),
                   pl.BlockSpec((1, Q_ROPE_W, tmb), lambda bb, i: (bb, 0, i)),
                   tok(Q_NOPE_W), tok(LANES),
                   pl.BlockSpec((1, Q_NOPE_W, tmb), lambda bb, i: (bb, 0, i))],
        out_shape=[jax.ShapeDtypeStruct((b, Q_NOPE_W, n), BF16),
                   jax.ShapeDtypeStruct((b, Q_ROPE_W, n), BF16),
                   jax.ShapeDtypeStruct((b, n, Q_NOPE_W), BF16),
                   jax.ShapeDtypeStruct((b, n, LANES), BF16),
                   jax.ShapeDtypeStruct((b, Q_NOPE_W, n), BF16)],
        compiler_params=_params("arbitrary", "arbitrary"),
        name="odd_proj",
    )(xa, mods, mods, g, win, qag, kvg, wuq, wukv, gq, gk, e64, e32, cos, sin)


SUM_ROWS = 16


def _kv_tile(n):
    for t in (768, 256):
        if n % t == 0 and (n // t) % 2 == 1:
            return t
    raise ValueError(f"joint sequence length {n} has no odd split into 256-multiples")


def _flash_kernel(qnt_ref, qrt_ref, kn_ref, kr_ref, vt_ref, o_ref,
                  s_ref, mx_ref, m_ref, acc_ref, qt_ref, *, tk, tq):
    p = pl.program_id(1)
    n = kn_ref.shape[1]
    nq = (n - CTX_LEN) // tq
    nch = n // tk
    assert nq % 2 == 0 and nch % 2 == 1
    ones = jnp.ones((SUM_ROWS, tk), BF16)
    row = lax.broadcasted_iota(jnp.int32, (LANES, tq), 0)

    def prep_q(qi, qslot):
        c0 = pl.multiple_of(CTX_LEN + qi * tq, tq)
        qn = qnt_ref[0, :, pl.ds(c0, tq)]
        qr = qrt_ref[0, :, pl.ds(c0, tq)]
        zero = jnp.zeros_like(qn)
        for hh in range(2):
            off = ((2 * p + hh) % 4) * C_ROPE
            keep_n = (row < C_NOPE) if hh == 0 else (row >= C_NOPE)
            keep_r = (row >= off) & (row < off + C_ROPE)
            qt_ref[qslot, 0:LANES, hh * tq:(hh + 1) * tq] = jnp.where(keep_n, qn, zero)
            qt_ref[qslot, LANES:, hh * tq:(hh + 1) * tq] = jnp.where(keep_r, qr, zero)

    def scores(qslot, j, sslot):
        k0 = j * tk
        kcat = jnp.concatenate([kn_ref[0, pl.ds(k0, tk), :], kr_ref[0, pl.ds(k0, tk), :]], axis=1)
        st = _dot(kcat, qt_ref[qslot])
        s_ref[sslot] = st
        mx_ref[sslot] = jnp.max(st, axis=0, keepdims=True)

    def update(j, sslot):
        k0 = j * tk
        m_old = m_ref[...]
        m_new = jnp.maximum(m_old, mx_ref[sslot])
        alpha = jnp.exp2(m_old - m_new)
        pt = jnp.exp2(s_ref[sslot] - m_new).astype(BF16)
        va = jnp.concatenate([vt_ref[0, :, pl.ds(k0, tk)], ones], axis=0)
        m_ref[...] = m_new
        acc_ref[...] = alpha * acc_ref[...] + _dot(va, pt)

    def reset():
        m_ref[...] = jnp.full(m_ref.shape, NEG_INF, F32)
        acc_ref[...] = jnp.zeros(acc_ref.shape, F32)

    def finish(qi):
        acc = acc_ref[...]
        c0 = pl.multiple_of(qi * tq, tq)
        o_ref[0, 0:C_NOPE, pl.ds(c0, tq)] = (acc[0:C_NOPE, :tq] / acc[LANES:LANES + 1, :tq]).astype(o_ref.dtype)
        o_ref[0, C_NOPE:, pl.ds(c0, tq)] = (acc[C_NOPE:LANES, tq:] / acc[LANES:LANES + 1, tq:]).astype(o_ref.dtype)
        reset()

    def run_tile(qi, slot, next_qi):
        other = 1 - slot
        for j in range(nch - 1):
            scores(slot, j + 1, other if j % 2 == 0 else slot)
            update(j, slot if j % 2 == 0 else other)
        prep_q(next_qi, other)
        scores(other, 0, other)
        update(nch - 1, slot)
        finish(qi)

    def two_tiles(t2, carry):
        run_tile(2 * t2, 0, 2 * t2 + 1)
        run_tile(2 * t2 + 1, 1, jnp.minimum(2 * t2 + 2, nq - 1))
        return carry

    reset()
    prep_q(0, 0)
    scores(0, 0, 0)
    lax.fori_loop(0, nq // 2, two_tiles, 0)


def _flash_attn(qnt, qrt, kn, kr, vt):
    b, n, _ = kn.shape
    s = n - CTX_LEN
    tq = TM
    tk = _kv_tile(n)
    return pl.pallas_call(
        functools.partial(_flash_kernel, tk=tk, tq=tq),
        grid=(b, C_HEADS // 2),
        scratch_shapes=[pltpu.VMEM((2, tk, 2 * tq), F32),
                        pltpu.VMEM((2, 1, 2 * tq), F32),
                        pltpu.VMEM((1, 2 * tq), F32),
                        pltpu.VMEM((LANES + SUM_ROWS, 2 * tq), F32),
                        pltpu.VMEM((2, 2 * LANES, 2 * tq), BF16)],
        in_specs=[
            pl.BlockSpec((1, LANES, n), lambda bb, p: (bb, p, 0)),
            pl.BlockSpec((1, LANES, n), lambda bb, p: (bb, p // 2, 0)),
            pl.BlockSpec((1, n, LANES), lambda bb, p: (bb, 0, p)),
            pl.BlockSpec((1, n, LANES), lambda bb, p: (bb, 0, 0)),
            pl.BlockSpec((1, LANES, n), lambda bb, p: (bb, p, 0)),
        ],
        out_specs=pl.BlockSpec((1, LANES, s), lambda bb, p: (bb, p, 0)),
        out_shape=jax.ShapeDtypeStruct((b, C_HEADS * C_NOPE, s), BF16),
        compiler_params=_params("arbitrary", "arbitrary"),
        name="mla_flash",
    )(qnt, qrt, kn, kr, vt)


def _rope_tables(seq, dim):
    t = jnp.arange(seq, dtype=jnp.int32)
    pos = jnp.stack([t // GRID_W, t % GRID_W], axis=0).astype(F32)
    half = dim // 2
    q = half // 2
    inv = ROPE_THETA ** (-jnp.arange(q, dtype=F32) / q)
    j = jnp.arange(dim)
    ang = pos[j // half].T * inv[j % q][None, :]
    sign = jnp.where((j % half) < q, -1.0, 1.0).astype(F32)
    cos = jnp.cos(ang)
    sin = jnp.sin(ang) * sign[None, :]
    reps = LANES // dim
    cos = jnp.tile(cos, (1, reps))
    sin = jnp.tile(sin, (1, reps))
    cos = jnp.concatenate([jnp.ones((CTX_LEN, LANES), F32), cos], axis=0)
    sin = jnp.concatenate([jnp.zeros((CTX_LEN, LANES), F32), sin], axis=0)
    return cos, sin


def _group_ones(group):
    r = jnp.arange(MXU_DIM) // group
    return (r[:, None] == r[None, :]).astype(BF16)


def _na_bias_tables(rpb, mult):
    cq = jnp.arange(GRID_W)
    c0 = jnp.clip(cq - NA_COLS // 2, 0, GRID_W - NA_COLS)
    col_ok = (cq[:, None] >= c0[None, :]) & (cq[:, None] < c0[None, :] + NA_COLS)
    dci = jnp.clip(cq[:, None] - cq[None, :], 1 - NA_COLS, NA_COLS - 1) + NA_COLS - 1
    pick = (dci[None] == jnp.arange(2 * NA_COLS - 1)[:, None, None]).astype(F32)
    tt = jnp.einsum("hdm,mkq->hdkq", rpb.astype(F32) * mult, pick, precision=lax.Precision.HIGHEST)
    tt = jnp.where(col_ok[None, None], tt, NEG_INF)
    tt = jnp.concatenate([tt, jnp.full_like(tt[:, :1], NEG_INF)], axis=1)
    zero = jnp.zeros_like(tt)

    def per_group(t):
        t = t.reshape(B_HEADS // NA_HG, NA_HG, NA_BIAS_N, GRID_W, LANES)
        return t.transpose(0, 2, 1, 3, 4)

    return per_group(jnp.concatenate([tt, zero], axis=-1)), per_group(jnp.concatenate([zero, tt], axis=-1))


def kernel(x, c, ctx, c_ctx, ada_w, ada_b, norm_mix, norm_mlp, mlp_w1, mlp_w2, e_w_in, e_w_out, a_q_norm, a_k_norm, a_sink, b_q_norm, b_k_norm, b_rpb, o_w_in, o_qa_norm, o_kva_norm, o_w_uq, o_w_ukv, o_qn_nope, o_qn_rope, o_kn_nope, o_kn_rope, o_w_out):
    bsz, seq, d = x.shape
    assert d == D_MODEL and ctx.shape[1] == CTX_LEN and seq % TM == 0 and ada_w.shape[0] == 2
    assert bsz + 1 <= 8

    cond = jnp.zeros((8, d), F32).at[:bsz].set(c).at[bsz].set(c_ctx)
    m = _adaln(cond, ada_w, ada_b)
    mods = [jnp.stack([jnp.broadcast_to(m[i, bsz], (bsz, 6 * d)), m[i, :bsz]], axis=1).reshape(2 * bsz, 1, 6 * d)
            for i in range(2)]

    e64 = _group_ones(HEAD_DIM)
    e32 = _group_ones(C_ROPE)

    w = e_w_in[0]
    w_ext = jnp.concatenate([w[:, 0:512], w[:, 512:640], w[:, 768:1280], w[:, 1280:1792],
                             w[:, 640:768], w[:, 1792:2304]], axis=1).astype(BF16)
    scale = HEAD_DIM ** -0.5
    log2e = math.log2(math.e)
    gains = jnp.concatenate([jnp.tile(a_q_norm[0], A_HEADS) * (scale * log2e), jnp.tile(a_k_norm[0], A_KV_HEADS),
                             jnp.tile(b_q_norm[0], B_HEADS) * (scale * log2e), jnp.tile(b_k_norm[0], B_HEADS)])[None, :]
    cos64, sin64 = _rope_tables(seq, HEAD_DIM)
    qat, ka, qbt, kb, vat, vbt = _even_proj(ctx, x, mods[0], norm_mix[0][None, :], w_ext, gains, e64, cos64, sin64)
    sink_row = (jnp.repeat(a_sink[0].reshape(A_KV_HEADS, GQA_R), A_BLOCK, axis=1) * log2e)[:, None, :]
    oa = _window_attn(qat, ka, vat, sink_row)
    ob = _na_attn(qbt, kb, vbt, *_na_bias_tables(b_rpb[0], log2e))
    wo = e_w_out[0].astype(BF16)
    xa = _even_out(ctx, x, oa, ob, mods[0], norm_mlp[0][None, :], wo[:512], wo[512:],
                   mlp_w1[0].astype(BF16), mlp_w2[0].astype(BF16))

    wi = o_w_in[0]
    win = jnp.concatenate([wi[:, :C_Q_RANK + C_KV_RANK]] + [wi[:, C_Q_RANK + C_KV_RANK:]] * 4, axis=1).astype(BF16)
    wuq = o_w_uq[0].reshape(C_Q_RANK, C_HEADS, C_NOPE + C_ROPE)
    wuq = jnp.concatenate([wuq[:, :, :C_NOPE].reshape(C_Q_RANK, -1), wuq[:, :, C_NOPE:].reshape(C_Q_RANK, -1)],
                          axis=1).astype(BF16)
    wukv = o_w_ukv[0].reshape(C_KV_RANK, C_HEADS, 2 * C_NOPE)
    wukv = jnp.concatenate([wukv[:, :, :C_NOPE].reshape(C_KV_RANK, -1), wukv[:, :, C_NOPE:].reshape(C_KV_RANK, -1)],
                           axis=1).astype(BF16)
    qscale = (C_NOPE + C_ROPE) ** -0.5 * math.log2(math.e)
    gq = (jnp.concatenate([jnp.tile(o_qn_nope[0], C_HEADS), jnp.tile(o_qn_rope[0], C_HEADS)]) * qscale)[None, :]
    gk = jnp.concatenate([jnp.tile(o_kn_nope[0], C_HEADS), jnp.tile(o_kn_rope[0], LANES // C_ROPE)])[None, :]
    cos32, sin32 = _rope_tables(seq, C_ROPE)
    qn, qr, kn, kr, vt = _odd_proj(xa, mods[1], norm_mix[1][None, :], win, o_qa_norm[0][None, :],
                                   o_kva_norm[0][None, :], wuq, wukv, gq, gk, e64, e32, cos32, sin32)
    ot = _flash_attn(qn, qr, kn, kr, vt)
    return _odd_out(xa, ot, mods[1], norm_mlp[1][None, :], o_w_out[0].astype(BF16),
                    mlp_w1[1].astype(BF16), mlp_w2[1].astype(BF16))
```

```python
import functools
import math

import jax
import jax.numpy as jnp
from jax import lax
from jax.experimental import pallas as pl
from jax.experimental.pallas import tpu as pltpu

F32 = jnp.float32
BF16 = jnp.bfloat16

D_MODEL = 1024
CTX_LEN = 256
GRID_W = 64
HEAD_DIM = 64
A_HEADS = 8
A_KV_HEADS = 2
A_WINDOW = 128
A_BLOCK = 128
B_HEADS = 8
NA_ROWS = 8
NA_COLS = 16
C_HEADS = 16
C_Q_RANK = 384
C_KV_RANK = 256
C_NOPE = 64
C_ROPE = 32
D_FF = 4 * D_MODEL
ROPE_THETA = 10000.0
NORM_EPS = 1e-6
NEG_INF = -1e30

LANES = 128
MXU_DIM = 256
TM = CTX_LEN
VMEM_LIMIT = 56 * 1024 * 1024


def _dot(a, b):
    return jnp.dot(a, b, preferred_element_type=F32)


def _dot_nt(a, b):
    return lax.dot_general(a, b, (((1,), (1,)), ((), ())), preferred_element_type=F32)


def _params(*sem):
    return pltpu.CompilerParams(dimension_semantics=sem, vmem_limit_bytes=VMEM_LIMIT)


def _norm_mod(x, g, sh, sc):
    ms = jnp.mean(x * x, axis=-1, keepdims=True)
    return (x * lax.rsqrt(ms + NORM_EPS) * g) * (1.0 + sc) + sh


def _group_rms(y, e, group):
    out = []
    for c0 in range(0, y.shape[1], MXU_DIM):
        cw = min(MXU_DIM, y.shape[1] - c0)
        yc = y[:, c0:c0 + cw]
        ss = _dot((yc * yc).astype(BF16), e[:cw, :cw])
        out.append(yc * lax.rsqrt(ss * (1.0 / group) + NORM_EPS))
    return out[0] if len(out) == 1 else jnp.concatenate(out, axis=1)


def _rope(r, cos, sin, half):
    w = r.shape[1]
    reps = w // LANES
    if reps > 1:
        cos = jnp.concatenate([cos] * reps, axis=1)
        sin = jnp.concatenate([sin] * reps, axis=1)
    lane = lax.broadcasted_iota(jnp.int32, r.shape, 1)
    up = pltpu.roll(r, w - half, axis=1)
    dn = pltpu.roll(r, half, axis=1)
    sw = jnp.where((lane & half) == 0, up, dn)
    return r * cos + sw * sin


def _stack_heads(q, lane_lo):
    z = jnp.zeros_like(q)
    return jnp.concatenate([jnp.where(lane_lo, q, z), jnp.where(lane_lo, z, q)], axis=0)


def _softmax_pv(parts, extra=None):
    m = None
    for s, _ in parts:
        mx = jnp.max(s, axis=1, keepdims=True)
        m = mx if m is None else jnp.maximum(m, mx)
    if extra is not None:
        m = jnp.maximum(m, extra)
    den = None
    o = None
    for s, v in parts:
        p = jnp.exp(s - m)
        d = jnp.sum(p, axis=1, keepdims=True)
        pv = _dot(p.astype(BF16), v)
        den = d if den is None else den + d
        o = pv if o is None else o + pv
    if extra is not None:
        den = den + jnp.exp(extra - m)
    return o / den


def _ada_kernel(cond_ref, w_ref, b_ref, o_ref):
    c = cond_ref[...]
    s = (c * jax.nn.sigmoid(c)).astype(BF16)
    o_ref[0] = _dot(s, w_ref[0].astype(BF16)) + b_ref[0]


def _adaln(cond, ada_w, ada_b):
    depth, d, n6 = ada_w.shape
    tn = 1536
    return pl.pallas_call(
        _ada_kernel,
        grid=(depth, n6 // tn),
        in_specs=[
            pl.BlockSpec((8, d), lambda l, j: (0, 0)),
            pl.BlockSpec((1, d, tn), lambda l, j: (l, 0, j)),
            pl.BlockSpec((1, 1, tn), lambda l, j: (l, 0, j)),
        ],
        out_specs=pl.BlockSpec((1, 8, tn), lambda l, j: (l, 0, j)),
        out_shape=jax.ShapeDtypeStruct((depth, 8, n6), F32),
        compiler_params=_params("arbitrary", "arbitrary"),
        name="adaln",
    )(cond, ada_w, ada_b.reshape(depth, 1, n6))


NSUB = 3


def _mod_spec(chunk, lat_only=False):
    if lat_only:
        return pl.BlockSpec((1, 1, D_MODEL), lambda b, i: (2 * b + 1, 0, chunk))
    return pl.BlockSpec((2, 1, D_MODEL), lambda b, i: (b, 0, chunk))


def _pick_mod(ref, is_ctx):
    m = ref[...]
    return jnp.where(is_ctx, m[0], m[1])


def _is_ctx(k):
    return NSUB * pl.program_id(1) + k == 0


def _const_spec(shape):
    nd = len(shape)
    return pl.BlockSpec(shape, lambda *_: (0,) * nd, pipeline_mode=pl.Buffered(1))


EVEN_NORMED = 1664
EVEN_ROPED = 640
EVEN_COLS = 2304
QA_W = A_HEADS * HEAD_DIM
KVA_W = A_KV_HEADS * HEAD_DIM
QKVB_W = B_HEADS * HEAD_DIM


def _joint_specs(d):
    lat = lambda k: pl.BlockSpec((1, TM, d), lambda bb, i: (bb, jnp.maximum(NSUB * i + k - 1, 0), 0))
    return [pl.BlockSpec((1, CTX_LEN, d), lambda bb, i: (bb, 0, 0))] + [lat(k) for k in range(NSUB)]


def _joint_tile(ctx_ref, x_refs, k):
    return jnp.where(_is_ctx(k), ctx_ref[0], x_refs[k][0])


def _even_proj_kernel(ctx_ref, *refs):
    x_refs = refs[:NSUB]
    (sh_ref, sc_ref, g_ref, w_ref, gains_ref, e_ref, cos_ref, sin_ref,
     qat_ref, ka_ref, qb_ref, kb_ref, vat_ref, vb_ref) = refs[NSUB:]
    for k in range(NSUB):
        rows = slice(k * TM, (k + 1) * TM)
        h = _norm_mod(_joint_tile(ctx_ref, x_refs, k), g_ref[...],
                      _pick_mod(sh_ref, _is_ctx(k)), _pick_mod(sc_ref, _is_ctx(k))).astype(BF16)
        y = _dot(h, w_ref[...])
        yn = _group_rms(y[:, :EVEN_NORMED], e_ref[...], HEAD_DIM) * gains_ref[...]
        r = _rope(yn[:, :EVEN_ROPED], cos_ref[rows, :], sin_ref[rows, :], HEAD_DIM // 4)
        qat_ref[0, :, rows] = r[:, 0:QA_W].T.astype(BF16)
        ka_ref[0, rows, :] = r[:, QA_W:EVEN_ROPED].astype(BF16)
        qb_ref[0, :, rows] = yn[:, EVEN_ROPED:EVEN_ROPED + QKVB_W].T.astype(BF16)
        kb_ref[0, rows, :] = yn[:, EVEN_ROPED + QKVB_W:EVEN_NORMED].astype(BF16)
        vat_ref[0, :, rows] = y[:, EVEN_NORMED:EVEN_NORMED + KVA_W].T.astype(BF16)
        vb_ref[0, :, rows] = y[:, EVEN_NORMED + KVA_W:].T.astype(BF16)


def _even_proj(ctx, x, mods, g, w, gains, e64, cos, sin):
    b, s, d = x.shape
    n = CTX_LEN + s
    tmb = NSUB * TM
    assert n % tmb == 0
    tok = lambda wd: (pl.BlockSpec((1, tmb, wd), lambda bb, i: (bb, i, 0)),
                      jax.ShapeDtypeStruct((b, n, wd), BF16))
    tr = lambda wd: (pl.BlockSpec((1, wd, tmb), lambda bb, i: (bb, 0, i)),
                     jax.ShapeDtypeStruct((b, wd, n), BF16))
    outs = (tr(QA_W), tok(KVA_W), tr(QKVB_W), tok(QKVB_W), tr(KVA_W), tr(QKVB_W))
    return pl.pallas_call(
        _even_proj_kernel,
        grid=(b, n // tmb),
        in_specs=_joint_specs(d) + [
            _mod_spec(0), _mod_spec(1),
            _const_spec((1, d)),
            _const_spec((d, EVEN_COLS)),
            _const_spec((1, EVEN_NORMED)),
            _const_spec((MXU_DIM, MXU_DIM)),
            pl.BlockSpec((tmb, LANES), lambda bb, i: (i, 0)),
            pl.BlockSpec((tmb, LANES), lambda bb, i: (i, 0)),
        ],
        out_specs=[o[0] for o in outs],
        out_shape=[o[1] for o in outs],
        compiler_params=_params("arbitrary", "arbitrary"),
        name="even_proj",
    )(ctx, *([x] * NSUB), mods, mods, g, w, gains, e64, cos, sin)


GQA_R = A_HEADS // A_KV_HEADS
WIN_BAND = 3 * A_BLOCK
WIN_LANES = GQA_R * A_BLOCK


PIPE_UNROLL = 2


def _window_kernel(qt_ref, k_ref, vt_ref, sink_ref, wm_ref, o_ref, s_ref, mx_ref, qs_ref, *, seq):
    g = pl.program_id(1)
    nb = seq // A_BLOCK
    nkeys = WIN_BAND + CTX_LEN
    assert nb % PIPE_UNROLL == 0 and PIPE_UNROLL % 2 == 0
    sink = sink_ref[0]
    ones_loc = jnp.ones((SUM_ROWS, nkeys), BF16)
    ones_ctx = jnp.ones((SUM_ROWS, CTX_LEN), BF16)

    def prep_q(col0, qslot):
        qt = qt_ref[0, :, pl.ds(col0, A_BLOCK)]
        zero = jnp.zeros((HEAD_DIM, A_BLOCK), BF16)
        for h in range(GQA_R):
            qh = qt[h * HEAD_DIM:(h + 1) * HEAD_DIM]
            lanes = slice(h * A_BLOCK, (h + 1) * A_BLOCK)
            qs_ref[qslot, 0:HEAD_DIM, lanes] = jnp.where(g == 0, qh, zero)
            qs_ref[qslot, HEAD_DIM:, lanes] = jnp.where(g == 0, zero, qh)

    def band_start(n):
        return pl.multiple_of(jnp.clip((n - 1) * A_BLOCK, 0, seq - WIN_BAND), A_BLOCK)

    def scores(qslot, n, sslot):
        start = band_start(n)
        keys = jnp.concatenate([k_ref[0, pl.ds(CTX_LEN + start, WIN_BAND), :], k_ref[0, 0:CTX_LEN, :]], axis=0)
        st = _dot(keys, qs_ref[qslot])
        mask = wm_ref[n - start // A_BLOCK]
        s_loc = st[:WIN_BAND] + jnp.concatenate([mask] * GQA_R, axis=1)
        s_ctx = st[WIN_BAND:]
        s_ref[sslot, 0:WIN_BAND] = s_loc
        s_ref[sslot, WIN_BAND:] = s_ctx
        mx_ref[sslot] = jnp.maximum(jnp.maximum(jnp.max(s_loc, axis=0, keepdims=True),
                                                jnp.max(s_ctx, axis=0, keepdims=True)), sink)

    def write(res, m, col0):
        o = res[0:HEAD_DIM] / (res[HEAD_DIM:HEAD_DIM + 1] + jnp.exp2(sink - m))
        for h in range(GQA_R):
            o_ref[0, h * HEAD_DIM:(h + 1) * HEAD_DIM, pl.ds(col0, A_BLOCK)] = (
                o[:, h * A_BLOCK:(h + 1) * A_BLOCK].astype(o_ref.dtype))

    def update(n, sslot):
        start = band_start(n)
        m = mx_ref[sslot]
        pt = jnp.exp2(s_ref[sslot] - m).astype(BF16)
        va = jnp.concatenate([vt_ref[0, :, pl.ds(CTX_LEN + start, WIN_BAND)], vt_ref[0, :, 0:CTX_LEN]], axis=1)
        res = _dot(jnp.concatenate([va, ones_loc], axis=0), pt)
        write(res, m, pl.multiple_of(CTX_LEN + n * A_BLOCK, A_BLOCK))

    for blk in range(CTX_LEN // A_BLOCK):
        prep_q(blk * A_BLOCK, 0)
        st = _dot(k_ref[0, 0:CTX_LEN, :], qs_ref[0])
        m = jnp.maximum(jnp.max(st, axis=0, keepdims=True), sink)
        pt = jnp.exp2(st - m).astype(BF16)
        res = _dot(jnp.concatenate([vt_ref[0, :, 0:CTX_LEN], ones_ctx], axis=0), pt)
        write(res, m, blk * A_BLOCK)

    def some_blocks(t, carry):
        for u in range(PIPE_UNROLL):
            nxt = jnp.minimum(PIPE_UNROLL * t + u + 1, nb - 1)
            prep_q(pl.multiple_of(CTX_LEN + nxt * A_BLOCK, A_BLOCK), (u + 1) % 2)
            scores((u + 1) % 2, nxt, (u + 1) % 2)
            update(PIPE_UNROLL * t + u, u % 2)
        return carry

    prep_q(CTX_LEN, 0)
    scores(0, 0, 0)
    lax.fori_loop(0, nb // PIPE_UNROLL, some_blocks, 0)


def _window_mask():
    o = jnp.arange(3)[:, None, None]
    key = jnp.arange(WIN_BAND)[None, :, None]
    qry = jnp.arange(A_BLOCK)[None, None, :]
    return jnp.where(jnp.abs(qry + o * A_BLOCK - key) <= A_WINDOW, 0.0, NEG_INF).astype(F32)


def _window_attn(qat, ka, vat, sink_row):
    b, _, n = qat.shape
    return pl.pallas_call(
        functools.partial(_window_kernel, seq=n - CTX_LEN),
        grid=(b, A_KV_HEADS),
        scratch_shapes=[pltpu.VMEM((2, WIN_BAND + CTX_LEN, WIN_LANES), F32),
                        pltpu.VMEM((2, 1, WIN_LANES), F32),
                        pltpu.VMEM((2, 2 * HEAD_DIM, WIN_LANES), BF16)],
        in_specs=[
            pl.BlockSpec((1, GQA_R * HEAD_DIM, n), lambda bb, g: (bb, g, 0)),
            pl.BlockSpec((1, n, KVA_W), lambda bb, g: (bb, 0, 0)),
            pl.BlockSpec((1, HEAD_DIM, n), lambda bb, g: (bb, g, 0)),
            pl.BlockSpec((1, 1, WIN_LANES), lambda bb, g: (g, 0, 0)),
            _const_spec((3, WIN_BAND, A_BLOCK)),
        ],
        out_specs=pl.BlockSpec((1, GQA_R * HEAD_DIM, n), lambda bb, g: (bb, g, 0)),
        out_shape=jax.ShapeDtypeStruct((b, QA_W, n), BF16),
        compiler_params=_params("arbitrary", "arbitrary"),
        name="window_attn",
    )(qat, ka, vat, sink_row, _window_mask())


NA_ITEM_ROWS = LANES // GRID_W
NA_WIN_ROWS = NA_ROWS + NA_ITEM_ROWS
NA_HG = 4
NA_W = NA_HG * HEAD_DIM
NA_LANES = NA_HG * LANES
NA_BIAS_N = 2 * NA_ROWS


def _na_kernel(qt_ref, k_ref, vt_ref, ta_ref, tb_ref, o_ref, s_ref, mx_ref, qs_ref, *, rows_n):
    nitems = rows_n // NA_ITEM_ROWS
    nloc = NA_WIN_ROWS * GRID_W
    assert nitems % PIPE_UNROLL == 0 and (rows_n - NA_WIN_ROWS) % 2 == 0 and nloc % LANES == 0
    ones_loc = jnp.ones((SUM_ROWS, nloc + CTX_LEN), BF16)
    ones_ctx = jnp.ones((SUM_ROWS, CTX_LEN), BF16)

    def prep_q(col0, qslot):
        qt = qt_ref[0, :, pl.ds(col0, LANES)]
        for h in range(NA_HG):
            rows = slice(h * HEAD_DIM, (h + 1) * HEAD_DIM)
            qs_ref[qslot, rows, h * LANES:(h + 1) * LANES] = qt[rows]

    def win_start(j):
        return jnp.clip(NA_ITEM_ROWS * j - NA_ROWS // 2, 0, rows_n - NA_WIN_ROWS)

    def scores(qslot, j, sslot):
        ru = win_start(j)
        k0 = pl.multiple_of(CTX_LEN + ru * GRID_W, LANES)
        keys = jnp.concatenate([k_ref[0, pl.ds(k0, nloc), :], k_ref[0, 0:CTX_LEN, :]], axis=0)
        st = _dot(keys, qs_ref[qslot])
        mx = jnp.max(st[nloc:], axis=0, keepdims=True)
        s_ref[sslot, nloc:] = st[nloc:]
        for i in range(NA_WIN_ROWS):
            idx = []
            for e in range(NA_ITEM_ROWS):
                r = NA_ITEM_ROWS * j + e
                r0 = jnp.clip(r - NA_ROWS // 2, 0, rows_n - NA_ROWS)
                seen = (ru + i >= r0) & (ru + i < r0 + NA_ROWS)
                idx.append(jnp.where(seen, ru + i - r + NA_ROWS - 1, NA_BIAS_N - 1))
            rows = slice(i * GRID_W, (i + 1) * GRID_W)
            bias = ta_ref[0, idx[0]] + tb_ref[0, idx[1]]
            blk = st[rows] + jnp.concatenate([bias[h] for h in range(NA_HG)], axis=1)
            s_ref[sslot, rows] = blk
            mx = jnp.maximum(mx, jnp.max(blk, axis=0, keepdims=True))
        mx_ref[sslot] = mx

    def pv_write(va, ones, pt, col0):
        for pair in range(NA_HG // 2):
            rows = slice(pair * LANES, (pair + 1) * LANES)
            res = _dot(jnp.concatenate([va[rows], ones], axis=0),
                       pt[:, pair * 2 * LANES:(pair + 1) * 2 * LANES])
            for hh in range(2):
                lanes = slice(hh * LANES, (hh + 1) * LANES)
                o = res[hh * HEAD_DIM:(hh + 1) * HEAD_DIM, lanes] / res[LANES:LANES + 1, lanes]
                h = 2 * pair + hh
                o_ref[0, h * HEAD_DIM:(h + 1) * HEAD_DIM, pl.ds(col0, LANES)] = o.astype(o_ref.dtype)

    def update(j, sslot):
        k0 = pl.multiple_of(CTX_LEN + win_start(j) * GRID_W, LANES)
        pt = jnp.exp2(s_ref[sslot] - mx_ref[sslot]).astype(BF16)
        va = jnp.concatenate([vt_ref[0, :, pl.ds(k0, nloc)], vt_ref[0, :, 0:CTX_LEN]], axis=1)
        pv_write(va, ones_loc, pt, pl.multiple_of(CTX_LEN + j * LANES, LANES))

    qs_ref[...] = jnp.zeros(qs_ref.shape, BF16)
    for blk in range(CTX_LEN // LANES):
        prep_q(blk * LANES, 0)
        st = _dot(k_ref[0, 0:CTX_LEN, :], qs_ref[0])
        pt = jnp.exp2(st - jnp.max(st, axis=0, keepdims=True)).astype(BF16)
        pv_write(vt_ref[0, :, 0:CTX_LEN], ones_ctx, pt, blk * LANES)

    def some_items(t, carry):
        for u in range(PIPE_UNROLL):
            nxt = jnp.minimum(PIPE_UNROLL * t + u + 1, nitems - 1)
            prep_q(pl.multiple_of(CTX_LEN + nxt * LANES, LANES), (u + 1) % 2)
            scores((u + 1) % 2, nxt, (u + 1) % 2)
            update(PIPE_UNROLL * t + u, u % 2)
        return carry

    prep_q(CTX_LEN, 0)
    scores(0, 0, 0)
    lax.fori_loop(0, nitems // PIPE_UNROLL, some_items, 0)


def _na_attn(qbt, kb, vbt, ta, tb):
    b, _, n = qbt.shape
    rows_n = (n - CTX_LEN) // GRID_W
    assert rows_n >= NA_WIN_ROWS
    nkeys = NA_WIN_ROWS * GRID_W + CTX_LEN
    tspec = pl.BlockSpec((1, NA_BIAS_N, NA_HG, GRID_W, LANES), lambda bb, hg: (hg, 0, 0, 0, 0))
    return pl.pallas_call(
        functools.partial(_na_kernel, rows_n=rows_n),
        grid=(b, B_HEADS // NA_HG),
        scratch_shapes=[pltpu.VMEM((2, nkeys, NA_LANES), F32),
                        pltpu.VMEM((2, 1, NA_LANES), F32),
                        pltpu.VMEM((2, NA_W, NA_LANES), BF16)],
        in_specs=[
            pl.BlockSpec((1, NA_W, n), lambda bb, hg: (bb, hg, 0)),
            pl.BlockSpec((1, n, NA_W), lambda bb, hg: (bb, 0, hg)),
            pl.BlockSpec((1, NA_W, n), lambda bb, hg: (bb, hg, 0)),
            tspec, tspec,
        ],
        out_specs=pl.BlockSpec((1, NA_W, n), lambda bb, hg: (bb, hg, 0)),
        out_shape=jax.ShapeDtypeStruct((b, QKVB_W, n), BF16),
        compiler_params=_params("arbitrary", "arbitrary"),
        name="na_attn",
    )(qbt, kb, vbt, ta, tb)


def _mlp_tail(xs, y, mods, gn, w1_ref, w2_ref):
    x1s = [x + m[0] * y[k * TM:(k + 1) * TM] for k, (x, m) in enumerate(zip(xs, mods))]
    h = jnp.concatenate([_norm_mod(x1, gn, m[1], m[2]).astype(BF16) for x1, m in zip(x1s, mods)], axis=0)
    mlp = None
    for j in range(D_FF // D_MODEL):
        cols = slice(j * D_MODEL, (j + 1) * D_MODEL)
        u = jnp.maximum(_dot(h, w1_ref[:, cols]), 0.0)
        part = _dot((u * u).astype(BF16), w2_ref[cols, :])
        mlp = part if mlp is None else mlp + part
    return [x1 + m[3] * mlp[k * TM:(k + 1) * TM] for k, (x1, m) in enumerate(zip(x1s, mods))]


def _even_out_kernel(ctx_ref, *refs):
    x_refs = refs[:NSUB]
    (oat_ref, obt_ref, g1_ref, sh2_ref, sc2_ref, g2_ref, gn_ref,
     woa_ref, wob_ref, w1_ref, w2_ref, o_ref) = refs[NSUB:]
    oa = oat_ref[0].astype(F32).T.astype(BF16)
    ob = obt_ref[0].astype(F32).T.astype(BF16)
    y = _dot(oa, woa_ref[...]) + _dot(ob, wob_ref[...])
    xs = [_joint_tile(ctx_ref, x_refs, k) for k in range(NSUB)]
    mods = [[_pick_mod(r, _is_ctx(k)) for r in (g1_ref, sh2_ref, sc2_ref, g2_ref)] for k in range(NSUB)]
    for k, out in enumerate(_mlp_tail(xs, y, mods, gn_ref[...], w1_ref, w2_ref)):
        o_ref[0, k * TM:(k + 1) * TM, :] = out


def _even_out(ctx, x, oa, ob, mods, gn, woa, wob, w1, w2):
    b, s, d = x.shape
    n = CTX_LEN + s
    tmb = NSUB * TM
    return pl.pallas_call(
        _even_out_kernel,
        grid=(b, n // tmb),
        in_specs=_joint_specs(d) + [
            pl.BlockSpec((1, QA_W, tmb), lambda bb, i: (bb, 0, i)),
            pl.BlockSpec((1, QKVB_W, tmb), lambda bb, i: (bb, 0, i)),
            _mod_spec(2), _mod_spec(3), _mod_spec(4), _mod_spec(5),
            _const_spec((1, d)),
            _const_spec((QA_W, d)), _const_spec((QKVB_W, d)),
            _const_spec((d, D_FF)), _const_spec((D_FF, d)),
        ],
        out_specs=pl.BlockSpec((1, tmb, d), lambda bb, i: (bb, i, 0)),
        out_shape=jax.ShapeDtypeStruct((b, n, d), F32),
        compiler_params=_params("arbitrary", "arbitrary"),
        name="even_out_mlp",
    )(ctx, *([x] * NSUB), oa, ob, mods, mods, mods, mods, gn, woa, wob, w1, w2)


NSUB_ODD = 2


def _odd_out_kernel(*refs):
    x_refs = refs[:NSUB_ODD]
    ot_ref, g1_ref, sh2_ref, sc2_ref, g2_ref, gn_ref, wo_ref, w1_ref, w2_ref, o_ref = refs[NSUB_ODD:]
    o = ot_ref[0].astype(F32).T.astype(BF16)
    y = _dot(o, wo_ref[...])
    mods = [[g1_ref[0], sh2_ref[0], sc2_ref[0], g2_ref[0]]] * NSUB_ODD
    for k, out in enumerate(_mlp_tail([r[0] for r in x_refs], y, mods, gn_ref[...], w1_ref, w2_ref)):
        o_ref[0, k * TM:(k + 1) * TM, :] = out


def _odd_out(xa, ot, mods, gn, wo, w1, w2):
    b, n, d = xa.shape
    s = n - CTX_LEN
    tmb = NSUB_ODD * TM
    assert s % tmb == 0
    lat = lambda k: pl.BlockSpec((1, TM, d), lambda bb, i: (bb, NSUB_ODD * i + k + 1, 0))
    return pl.pallas_call(
        _odd_out_kernel,
        grid=(b, s // tmb),
        in_specs=[lat(k) for k in range(NSUB_ODD)] + [
            pl.BlockSpec((1, d, tmb), lambda bb, i: (bb, 0, i)),
            _mod_spec(2, True), _mod_spec(3, True), _mod_spec(4, True), _mod_spec(5, True),
            _const_spec((1, d)),
            _const_spec((d, d)),
            _const_spec((d, D_FF)), _const_spec((D_FF, d)),
        ],
        out_specs=pl.BlockSpec((1, tmb, d), lambda bb, i: (bb, i, 0)),
        out_shape=jax.ShapeDtypeStruct((b, s, d), F32),
        compiler_params=_params("arbitrary", "arbitrary"),
        name="odd_out_mlp",
    )(*([xa] * NSUB_ODD), ot, mods, mods, mods, mods, gn, wo, w1, w2)


ODD_IN_COLS = C_Q_RANK + C_KV_RANK + LANES
Q_NOPE_W = C_HEADS * C_NOPE
Q_ROPE_W = C_HEADS * C_ROPE


def _odd_proj_kernel(x_ref, sh_ref, sc_ref, g_ref, win_ref, qag_ref, kvg_ref, wuq_ref, wukv_ref,
                     gq_ref, gk_ref, e64_ref, e32_ref, cos_ref, sin_ref,
                     qn_ref, qr_ref, kn_ref, kr_ref, vt_ref):
    def rms(t, g):
        return (t * lax.rsqrt(jnp.mean(t * t, axis=-1, keepdims=True) + NORM_EPS) * g).astype(BF16)

    gq = gq_ref[...]
    gk = gk_ref[...]
    for k in range(NSUB):
        rows = slice(k * TM, (k + 1) * TM)
        h = _norm_mod(x_ref[0, rows, :], g_ref[...],
                      _pick_mod(sh_ref, _is_ctx(k)), _pick_mod(sc_ref, _is_ctx(k))).astype(BF16)
        y = _dot(h, win_ref[...])
        cq = y[:, :C_Q_RANK]
        ckv = y[:, C_Q_RANK:C_Q_RANK + C_KV_RANK]
        kr = y[:, C_Q_RANK + C_KV_RANK:]
        q = _dot(rms(cq, qag_ref[...]), wuq_ref[...])
        kv = _dot(rms(ckv, kvg_ref[...]), wukv_ref[...])
        cos = cos_ref[rows, :]
        sin = sin_ref[rows, :]
        qn = _group_rms(q[:, :Q_NOPE_W], e64_ref[...], C_NOPE) * gq[:, :Q_NOPE_W]
        qn_ref[0, :, rows] = qn.T.astype(BF16)
        qr = _group_rms(q[:, Q_NOPE_W:], e32_ref[...], C_ROPE) * gq[:, Q_NOPE_W:]
        qr_ref[0, :, rows] = _rope(qr, cos, sin, C_ROPE // 4).T.astype(BF16)
        kn = _group_rms(kv[:, :Q_NOPE_W], e64_ref[...], C_NOPE) * gk[:, :Q_NOPE_W]
        kn_ref[0, rows, :] = kn.astype(BF16)
        krn = _group_rms(kr, e32_ref[...], C_ROPE) * gk[:, Q_NOPE_W:]
        kr_ref[0, rows, :] = _rope(krn, cos, sin, C_ROPE // 4).astype(BF16)
        vt_ref[0, :, rows] = kv[:, Q_NOPE_W:].T.astype(BF16)


def _odd_proj(xa, mods, g, win, qag, kvg, wuq, wukv, gq, gk, e64, e32, cos, sin):
    b, n, d = xa.shape
    tmb = NSUB * TM
    assert n % tmb == 0
    tok = lambda wd: pl.BlockSpec((1, tmb, wd), lambda bb, i: (bb, i, 0))
    return pl.pallas_call(
        _odd_proj_kernel,
        grid=(b, n // tmb),
        in_specs=[
            tok(d), _mod_spec(0), _mod_spec(1),
            _const_spec((1, d)),
            _const_spec((d, ODD_IN_COLS)),
            _const_spec((1, C_Q_RANK)), _const_spec((1, C_KV_RANK)),
            _const_spec((C_Q_RANK, Q_NOPE_W + Q_ROPE_W)),
            _const_spec((C_KV_RANK, 2 * Q_NOPE_W)),
            _const_spec((1, Q_NOPE_W + Q_ROPE_W)), _const_spec((1, Q_NOPE_W + LANES)),
            _const_spec((MXU_DIM, MXU_DIM)), _const_spec((MXU_DIM, MXU_DIM)),
            pl.BlockSpec((tmb, LANES), lambda bb, i: (i, 0)),
            pl.BlockSpec((tmb, LANES), lambda bb, i: (i, 0)),
        ],
        out_specs=[pl.BlockSpec((1, Q_NOPE_W, tmb), lambda bb, i: (bb, 0, i)),
                   pl.BlockSpec((1, Q_ROPE_W, tmb), lambda bb, i: (bb, 0, i)),
                   tok(Q_NOPE_W), tok(LANES),
                   pl.BlockSpec((1, Q_NOPE_W, tmb), lambda bb, i: (bb, 0, i))],
        out_shape=[jax.ShapeDtypeStruct((b, Q_NOPE_W, n), BF16),
                   jax.ShapeDtypeStruct((b, Q_ROPE_W, n), BF16),
                   jax.ShapeDtypeStruct((b, n, Q_NOPE_W), BF16),
                   jax.ShapeDtypeStruct((b, n, LANES), BF16),
                   jax.ShapeDtypeStruct((b, Q_NOPE_W, n), BF16)],
        compiler_params=_params("arbitrary", "arbitrary"),
        name="odd_proj",
    )(xa, mods, mods, g, win, qag, kvg, wuq, wukv, gq, gk, e64, e32, cos, sin)


SUM_ROWS = 16


def _kv_tile(n):
    for t in (768, 256):
        if n % t == 0 and (n // t) % 2 == 1:
            return t
    raise ValueError(f"joint sequence length {n} has no odd split into 256-multiples")


def _flash_kernel(qnt_ref, qrt_ref, kn_ref, kr_ref, vt_ref, o_ref,
                  s_ref, mx_ref, m_ref, acc_ref, qt_ref, *, tk, tq):
    p = pl.program_id(1)
    n = kn_ref.shape[1]
    nq = (n - CTX_LEN) // tq
    nch = n // tk
    assert nq % 2 == 0 and nch % 2 == 1
    ones = jnp.ones((SUM_ROWS, tk), BF16)
    row = lax.broadcasted_iota(jnp.int32, (LANES, tq), 0)

    def prep_q(qi, qslot):
        c0 = pl.multiple_of(CTX_LEN + qi * tq, tq)
        qn = qnt_ref[0, :, pl.ds(c0, tq)]
        qr = qrt_ref[0, :, pl.ds(c0, tq)]
        zero = jnp.zeros_like(qn)
        for hh in range(2):
            off = ((2 * p + hh) % 4) * C_ROPE
            keep_n = (row < C_NOPE) if hh == 0 else (row >= C_NOPE)
            keep_r = (row >= off) & (row < off + C_ROPE)
            qt_ref[qslot, 0:LANES, hh * tq:(hh + 1) * tq] = jnp.where(keep_n, qn, zero)
            qt_ref[qslot, LANES:, hh * tq:(hh + 1) * tq] = jnp.where(keep_r, qr, zero)

    def scores(qslot, j, sslot):
        k0 = j * tk
        kcat = jnp.concatenate([kn_ref[0, pl.ds(k0, tk), :], kr_ref[0, pl.ds(k0, tk), :]], axis=1)
        st = _dot(kcat, qt_ref[qslot])
        s_ref[sslot] = st
        mx_ref[sslot] = jnp.max(st, axis=0, keepdims=True)

    def update(j, sslot):
        k0 = j * tk
        m_old = m_ref[...]
        m_new = jnp.maximum(m_old, mx_ref[sslot])
        alpha = jnp.exp2(m_old - m_new)
        pt = jnp.exp2(s_ref[sslot] - m_new).astype(BF16)
        m_ref[...] = m_new
        for hh in range(2):
            lanes = slice(hh * tq, (hh + 1) * tq)
            va = jnp.concatenate([vt_ref[0, hh * C_NOPE:(hh + 1) * C_NOPE, pl.ds(k0, tk)], ones], axis=0)
            acc_ref[hh] = alpha[:, lanes] * acc_ref[hh] + _dot(va, pt[:, lanes])

    def reset():
        m_ref[...] = jnp.full(m_ref.shape, NEG_INF, F32)
        acc_ref[...] = jnp.zeros(acc_ref.shape, F32)

    def finish(qi):
        c0 = pl.multiple_of(qi * tq, tq)
        for hh in range(2):
            acc = acc_ref[hh]
            o_ref[0, hh * C_NOPE:(hh + 1) * C_NOPE, pl.ds(c0, tq)] = (
                acc[0:C_NOPE] / acc[C_NOPE:C_NOPE + 1]).astype(o_ref.dtype)
        reset()

    def run_tile(qi, slot, next_qi):
        other = 1 - slot
        for j in range(nch - 1):
            scores(slot, j + 1, other if j % 2 == 0 else slot)
            update(j, slot if j % 2 == 0 else other)
        prep_q(next_qi, other)
        scores(other, 0, other)
        update(nch - 1, slot)
        finish(qi)

    def two_tiles(t2, carry):
        run_tile(2 * t2, 0, 2 * t2 + 1)
        run_tile(2 * t2 + 1, 1, jnp.minimum(2 * t2 + 2, nq - 1))
        return carry

    reset()
    prep_q(0, 0)
    scores(0, 0, 0)
    lax.fori_loop(0, nq // 2, two_tiles, 0)


def _flash_attn(qnt, qrt, kn, kr, vt):
    b, n, _ = kn.shape
    s = n - CTX_LEN
    tq = TM
    tk = _kv_tile(n)
    return pl.pallas_call(
        functools.partial(_flash_kernel, tk=tk, tq=tq),
        grid=(b, C_HEADS // 2),
        scratch_shapes=[pltpu.VMEM((2, tk, 2 * tq), F32),
                        pltpu.VMEM((2, 1, 2 * tq), F32),
                        pltpu.VMEM((1, 2 * tq), F32),
                        pltpu.VMEM((2, C_NOPE + SUM_ROWS, tq), F32),
                        pltpu.VMEM((2, 2 * LANES, 2 * tq), BF16)],
        in_specs=[
            pl.BlockSpec((1, LANES, n), lambda bb, p: (bb, p, 0)),
            pl.BlockSpec((1, LANES, n), lambda bb, p: (bb, p // 2, 0)),
            pl.BlockSpec((1, n, LANES), lambda bb, p: (bb, 0, p)),
            pl.BlockSpec((1, n, LANES), lambda bb, p: (bb, 0, 0)),
            pl.BlockSpec((1, LANES, n), lambda bb, p: (bb, p, 0)),
        ],
        out_specs=pl.BlockSpec((1, LANES, s), lambda bb, p: (bb, p, 0)),
        out_shape=jax.ShapeDtypeStruct((b, C_HEADS * C_NOPE, s), BF16),
        compiler_params=_params("arbitrary", "arbitrary"),
        name="mla_flash",
    )(qnt, qrt, kn, kr, vt)


def _rope_tables(seq, dim):
    t = jnp.arange(seq, dtype=jnp.int32)
    pos = jnp.stack([t // GRID_W, t % GRID_W], axis=0).astype(F32)
    half = dim // 2
    q = half // 2
    inv = ROPE_THETA ** (-jnp.arange(q, dtype=F32) / q)
    j = jnp.arange(dim)
    ang = pos[j // half].T * inv[j % q][None, :]
    sign = jnp.where((j % half) < q, -1.0, 1.0).astype(F32)
    cos = jnp.cos(ang)
    sin = jnp.sin(ang) * sign[None, :]
    reps = LANES // dim
    cos = jnp.tile(cos, (1, reps))
    sin = jnp.tile(sin, (1, reps))
    cos = jnp.concatenate([jnp.ones((CTX_LEN, LANES), F32), cos], axis=0)
    sin = jnp.concatenate([jnp.zeros((CTX_LEN, LANES), F32), sin], axis=0)
    return cos, sin


def _group_ones(group):
    r = jnp.arange(MXU_DIM) // group
    return (r[:, None] == r[None, :]).astype(BF16)


def _na_bias_tables(rpb, mult):
    cq = jnp.arange(GRID_W)
    c0 = jnp.clip(cq - NA_COLS // 2, 0, GRID_W - NA_COLS)
    col_ok = (cq[:, None] >= c0[None, :]) & (cq[:, None] < c0[None, :] + NA_COLS)
    dci = jnp.clip(cq[:, None] - cq[None, :], 1 - NA_COLS, NA_COLS - 1) + NA_COLS - 1
    pick = (dci[None] == jnp.arange(2 * NA_COLS - 1)[:, None, None]).astype(F32)
    tt = jnp.einsum("hdm,mkq->hdkq", rpb.astype(F32) * mult, pick, precision=lax.Precision.HIGHEST)
    tt = jnp.where(col_ok[None, None], tt, NEG_INF)
    tt = jnp.concatenate([tt, jnp.full_like(tt[:, :1], NEG_INF)], axis=1)
    zero = jnp.zeros_like(tt)

    def per_group(t):
        t = t.reshape(B_HEADS // NA_HG, NA_HG, NA_BIAS_N, GRID_W, LANES)
        return t.transpose(0, 2, 1, 3, 4)

    return per_group(jnp.concatenate([tt, zero], axis=-1)), per_group(jnp.concatenate([zero, tt], axis=-1))


def kernel(x, c, ctx, c_ctx, ada_w, ada_b, norm_mix, norm_mlp, mlp_w1, mlp_w2, e_w_in, e_w_out, a_q_norm, a_k_norm, a_sink, b_q_norm, b_k_norm, b_rpb, o_w_in, o_qa_norm, o_kva_norm, o_w_uq, o_w_ukv, o_qn_nope, o_qn_rope, o_kn_nope, o_kn_rope, o_w_out):
    bsz, seq, d = x.shape
    assert d == D_MODEL and ctx.shape[1] == CTX_LEN and seq % TM == 0 and ada_w.shape[0] == 2
    assert bsz + 1 <= 8

    cond = jnp.zeros((8, d), F32).at[:bsz].set(c).at[bsz].set(c_ctx)
    m = _adaln(cond, ada_w, ada_b)
    mods = [jnp.stack([jnp.broadcast_to(m[i, bsz], (bsz, 6 * d)), m[i, :bsz]], axis=1).reshape(2 * bsz, 1, 6 * d)
            for i in range(2)]

    e64 = _group_ones(HEAD_DIM)
    e32 = _group_ones(C_ROPE)

    w = e_w_in[0]
    w_ext = jnp.concatenate([w[:, 0:512], w[:, 512:640], w[:, 768:1280], w[:, 1280:1792],
                             w[:, 640:768], w[:, 1792:2304]], axis=1).astype(BF16)
    scale = HEAD_DIM ** -0.5
    log2e = math.log2(math.e)
    gains = jnp.concatenate([jnp.tile(a_q_norm[0], A_HEADS) * (scale * log2e), jnp.tile(a_k_norm[0], A_KV_HEADS),
                             jnp.tile(b_q_norm[0], B_HEADS) * (scale * log2e), jnp.tile(b_k_norm[0], B_HEADS)])[None, :]
    cos64, sin64 = _rope_tables(seq, HEAD_DIM)
    qat, ka, qbt, kb, vat, vbt = _even_proj(ctx, x, mods[0], norm_mix[0][None, :], w_ext, gains, e64, cos64, sin64)
    sink_row = (jnp.repeat(a_sink[0].reshape(A_KV_HEADS, GQA_R), A_BLOCK, axis=1) * log2e)[:, None, :]
    oa = _window_attn(qat, ka, vat, sink_row)
    ob = _na_attn(qbt, kb, vbt, *_na_bias_tables(b_rpb[0], log2e))
    wo = e_w_out[0].astype(BF16)
    xa = _even_out(ctx, x, oa, ob, mods[0], norm_mlp[0][None, :], wo[:512], wo[512:],
                   mlp_w1[0].astype(BF16), mlp_w2[0].astype(BF16))

    wi = o_w_in[0]
    win = jnp.concatenate([wi[:, :C_Q_RANK + C_KV_RANK]] + [wi[:, C_Q_RANK + C_KV_RANK:]] * 4, axis=1).astype(BF16)
    wuq = o_w_uq[0].reshape(C_Q_RANK, C_HEADS, C_NOPE + C_ROPE)
    wuq = jnp.concatenate([wuq[:, :, :C_NOPE].reshape(C_Q_RANK, -1), wuq[:, :, C_NOPE:].reshape(C_Q_RANK, -1)],
                          axis=1).astype(BF16)
    wukv = o_w_ukv[0].reshape(C_KV_RANK, C_HEADS, 2 * C_NOPE)
    wukv = jnp.concatenate([wukv[:, :, :C_NOPE].reshape(C_KV_RANK, -1), wukv[:, :, C_NOPE:].reshape(C_KV_RANK, -1)],
                           axis=1).astype(BF16)
    qscale = (C_NOPE + C_ROPE) ** -0.5 * math.log2(math.e)
    gq = (jnp.concatenate([jnp.tile(o_qn_nope[0], C_HEADS), jnp.tile(o_qn_rope[0], C_HEADS)]) * qscale)[None, :]
    gk = jnp.concatenate([jnp.tile(o_kn_nope[0], C_HEADS), jnp.tile(o_kn_rope[0], LANES // C_ROPE)])[None, :]
    cos32, sin32 = _rope_tables(seq, C_ROPE)
    qn, qr, kn, kr, vt = _odd_proj(xa, mods[1], norm_mix[1][None, :], win, o_qa_norm[0][None, :],
                                   o_kva_norm[0][None, :], wuq, wukv, gq, gk, e64, e32, cos32, sin32)
    ot = _flash_attn(qn, qr, kn, kr, vt)
    return _odd_out(xa, ot, mods[1], norm_mlp[1][None, :], o_w_out[0].astype(BF16),
                    mlp_w1[1].astype(BF16), mlp_w2[1].astype(BF16))
```

```python
import functools
import math

import jax
import jax.numpy as jnp
from jax import lax
from jax.experimental import pallas as pl
from jax.experimental.pallas import tpu as pltpu

F32 = jnp.float32
BF16 = jnp.bfloat16

D_MODEL = 1024
CTX_LEN = 256
GRID_W = 64
HEAD_DIM = 64
A_HEADS = 8
A_KV_HEADS = 2
A_WINDOW = 128
A_BLOCK = 128
B_HEADS = 8
NA_ROWS = 8
NA_COLS = 16
C_HEADS = 16
C_Q_RANK = 384
C_KV_RANK = 256
C_NOPE = 64
C_ROPE = 32
D_FF = 4 * D_MODEL
ROPE_THETA = 10000.0
NORM_EPS = 1e-6
NEG_INF = -1e30

LANES = 128
MXU_DIM = 256
TM = CTX_LEN
VMEM_LIMIT = 56 * 1024 * 1024


def _dot(a, b):
    return jnp.dot(a, b, preferred_element_type=F32)


def _dot_nt(a, b):
    return lax.dot_general(a, b, (((1,), (1,)), ((), ())), preferred_element_type=F32)


def _params(*sem):
    return pltpu.CompilerParams(dimension_semantics=sem, vmem_limit_bytes=VMEM_LIMIT)


def _norm_mod(x, g, sh, sc):
    ms = jnp.mean(x * x, axis=-1, keepdims=True)
    return (x * lax.rsqrt(ms + NORM_EPS) * g) * (1.0 + sc) + sh


def _group_rms(y, e, group):
    out = []
    for c0 in range(0, y.shape[1], MXU_DIM):
        cw = min(MXU_DIM, y.shape[1] - c0)
        yc = y[:, c0:c0 + cw]
        ss = _dot((yc * yc).astype(BF16), e[:cw, :cw])
        out.append(yc * lax.rsqrt(ss * (1.0 / group) + NORM_EPS))
    return out[0] if len(out) == 1 else jnp.concatenate(out, axis=1)


def _rope(r, cos, sin, half):
    w = r.shape[1]
    reps = w // LANES
    if reps > 1:
        cos = jnp.concatenate([cos] * reps, axis=1)
        sin = jnp.concatenate([sin] * reps, axis=1)
    lane = lax.broadcasted_iota(jnp.int32, r.shape, 1)
    up = pltpu.roll(r, w - half, axis=1)
    dn = pltpu.roll(r, half, axis=1)
    sw = jnp.where((lane & half) == 0, up, dn)
    return r * cos + sw * sin


def _stack_heads(q, lane_lo):
    z = jnp.zeros_like(q)
    return jnp.concatenate([jnp.where(lane_lo, q, z), jnp.where(lane_lo, z, q)], axis=0)


def _softmax_pv(parts, extra=None):
    m = None
    for s, _ in parts:
        mx = jnp.max(s, axis=1, keepdims=True)
        m = mx if m is None else jnp.maximum(m, mx)
    if extra is not None:
        m = jnp.maximum(m, extra)
    den = None
    o = None
    for s, v in parts:
        p = jnp.exp(s - m)
        d = jnp.sum(p, axis=1, keepdims=True)
        pv = _dot(p.astype(BF16), v)
        den = d if den is None else den + d
        o = pv if o is None else o + pv
    if extra is not None:
        den = den + jnp.exp(extra - m)
    return o / den


def _ada_kernel(cond_ref, w_ref, b_ref, o_ref):
    c = cond_ref[...]
    s = (c * jax.nn.sigmoid(c)).astype(BF16)
    o_ref[0] = _dot(s, w_ref[0].astype(BF16)) + b_ref[0]


def _adaln(cond, ada_w, ada_b):
    depth, d, n6 = ada_w.shape
    tn = 1536
    return pl.pallas_call(
        _ada_kernel,
        grid=(depth, n6 // tn),
        in_specs=[
            pl.BlockSpec((8, d), lambda l, j: (0, 0)),
            pl.BlockSpec((1, d, tn), lambda l, j: (l, 0, j)),
            pl.BlockSpec((1, 1, tn), lambda l, j: (l, 0, j)),
        ],
        out_specs=pl.BlockSpec((1, 8, tn), lambda l, j: (l, 0, j)),
        out_shape=jax.ShapeDtypeStruct((depth, 8, n6), F32),
        compiler_params=_params("arbitrary", "arbitrary"),
        name="adaln",
    )(cond, ada_w, ada_b.reshape(depth, 1, n6))


NSUB = 3


def _mod_spec(chunk, lat_only=False):
    if lat_only:
        return pl.BlockSpec((1, 1, D_MODEL), lambda b, i: (2 * b + 1, 0, chunk))
    return pl.BlockSpec((2, 1, D_MODEL), lambda b, i: (b, 0, chunk))


def _pick_mod(ref, is_ctx):
    m = ref[...]
    return jnp.where(is_ctx, m[0], m[1])


def _is_ctx(k):
    return NSUB * pl.program_id(1) + k == 0


def _const_spec(shape):
    nd = len(shape)
    return pl.BlockSpec(shape, lambda *_: (0,) * nd, pipeline_mode=pl.Buffered(1))


EVEN_NORMED = 1664
EVEN_ROPED = 640
EVEN_COLS = 2304
QA_W = A_HEADS * HEAD_DIM
KVA_W = A_KV_HEADS * HEAD_DIM
QKVB_W = B_HEADS * HEAD_DIM


def _joint_specs(d):
    lat = lambda k: pl.BlockSpec((1, TM, d), lambda bb, i: (bb, jnp.maximum(NSUB * i + k - 1, 0), 0))
    return [pl.BlockSpec((1, CTX_LEN, d), lambda bb, i: (bb, 0, 0))] + [lat(k) for k in range(NSUB)]


def _joint_tile(ctx_ref, x_refs, k):
    return jnp.where(_is_ctx(k), ctx_ref[0], x_refs[k][0])


def _even_proj_kernel(ctx_ref, *refs):
    x_refs = refs[:NSUB]
    (sh_ref, sc_ref, g_ref, w_ref, gains_ref, e_ref, cos_ref, sin_ref,
     qat_ref, ka_ref, qb_ref, kb_ref, vat_ref, vb_ref) = refs[NSUB:]
    for k in range(NSUB):
        rows = slice(k * TM, (k + 1) * TM)
        h = _norm_mod(_joint_tile(ctx_ref, x_refs, k), g_ref[...],
                      _pick_mod(sh_ref, _is_ctx(k)), _pick_mod(sc_ref, _is_ctx(k))).astype(BF16)
        y = _dot(h, w_ref[...])
        yn = _group_rms(y[:, :EVEN_NORMED], e_ref[...], HEAD_DIM) * gains_ref[...]
        r = _rope(yn[:, :EVEN_ROPED], cos_ref[rows, :], sin_ref[rows, :], HEAD_DIM // 4)
        qat_ref[0, :, rows] = r[:, 0:QA_W].T.astype(BF16)
        ka_ref[0, rows, :] = r[:, QA_W:EVEN_ROPED].astype(BF16)
        qb_ref[0, :, rows] = yn[:, EVEN_ROPED:EVEN_ROPED + QKVB_W].T.astype(BF16)
        kb_ref[0, rows, :] = yn[:, EVEN_ROPED + QKVB_W:EVEN_NORMED].astype(BF16)
        vat_ref[0, :, rows] = y[:, EVEN_NORMED:EVEN_NORMED + KVA_W].T.astype(BF16)
        vb_ref[0, :, rows] = y[:, EVEN_NORMED + KVA_W:].T.astype(BF16)


def _even_proj(ctx, x, mods, g, w, gains, e64, cos, sin):
    b, s, d = x.shape
    n = CTX_LEN + s
    tmb = NSUB * TM
    assert n % tmb == 0
    tok = lambda wd: (pl.BlockSpec((1, tmb, wd), lambda bb, i: (bb, i, 0)),
                      jax.ShapeDtypeStruct((b, n, wd), BF16))
    tr = lambda wd: (pl.BlockSpec((1, wd, tmb), lambda bb, i: (bb, 0, i)),
                     jax.ShapeDtypeStruct((b, wd, n), BF16))
    outs = (tr(QA_W), tok(KVA_W), tr(QKVB_W), tok(QKVB_W), tr(KVA_W), tr(QKVB_W))
    return pl.pallas_call(
        _even_proj_kernel,
        grid=(b, n // tmb),
        in_specs=_joint_specs(d) + [
            _mod_spec(0), _mod_spec(1),
            _const_spec((1, d)),
            _const_spec((d, EVEN_COLS)),
            _const_spec((1, EVEN_NORMED)),
            _const_spec((MXU_DIM, MXU_DIM)),
            pl.BlockSpec((tmb, LANES), lambda bb, i: (i, 0)),
            pl.BlockSpec((tmb, LANES), lambda bb, i: (i, 0)),
        ],
        out_specs=[o[0] for o in outs],
        out_shape=[o[1] for o in outs],
        compiler_params=_params("arbitrary", "arbitrary"),
        name="even_proj",
    )(ctx, *([x] * NSUB), mods, mods, g, w, gains, e64, cos, sin)


GQA_R = A_HEADS // A_KV_HEADS
WIN_BAND = 3 * A_BLOCK
WIN_LANES = GQA_R * A_BLOCK


PIPE_UNROLL = 2


def _window_kernel(qt_ref, k_ref, vt_ref, sink_ref, wm_ref, o_ref, s_ref, mx_ref, qs_ref, *, seq):
    g = pl.program_id(1)
    nb = seq // A_BLOCK
    nkeys = WIN_BAND + CTX_LEN
    assert nb % PIPE_UNROLL == 0 and PIPE_UNROLL % 2 == 0
    sink = sink_ref[0]
    ones_loc = jnp.ones((SUM_ROWS, nkeys), BF16)
    ones_ctx = jnp.ones((SUM_ROWS, CTX_LEN), BF16)

    def prep_q(col0, qslot):
        qt = qt_ref[0, :, pl.ds(col0, A_BLOCK)]
        zero = jnp.zeros((HEAD_DIM, A_BLOCK), BF16)
        for h in range(GQA_R):
            qh = qt[h * HEAD_DIM:(h + 1) * HEAD_DIM]
            lanes = slice(h * A_BLOCK, (h + 1) * A_BLOCK)
            qs_ref[qslot, 0:HEAD_DIM, lanes] = jnp.where(g == 0, qh, zero)
            qs_ref[qslot, HEAD_DIM:, lanes] = jnp.where(g == 0, zero, qh)

    def band_start(n):
        return pl.multiple_of(jnp.clip((n - 1) * A_BLOCK, 0, seq - WIN_BAND), A_BLOCK)

    def scores(qslot, n, sslot):
        start = band_start(n)
        keys = jnp.concatenate([k_ref[0, pl.ds(CTX_LEN + start, WIN_BAND), :], k_ref[0, 0:CTX_LEN, :]], axis=0)
        st = _dot(keys, qs_ref[qslot])
        mask = wm_ref[n - start // A_BLOCK]
        s_loc = st[:WIN_BAND] + jnp.concatenate([mask] * GQA_R, axis=1)
        s_ctx = st[WIN_BAND:]
        s_ref[sslot, 0:WIN_BAND] = s_loc
        s_ref[sslot, WIN_BAND:] = s_ctx
        mx_ref[sslot] = jnp.maximum(jnp.maximum(jnp.max(s_loc, axis=0, keepdims=True),
                                                jnp.max(s_ctx, axis=0, keepdims=True)), sink)

    def write(res, m, col0):
        o = res[0:HEAD_DIM] / (res[HEAD_DIM:HEAD_DIM + 1] + jnp.exp2(sink - m))
        for h in range(GQA_R):
            o_ref[0, h * HEAD_DIM:(h + 1) * HEAD_DIM, pl.ds(col0, A_BLOCK)] = (
                o[:, h * A_BLOCK:(h + 1) * A_BLOCK].astype(o_ref.dtype))

    def update(n, sslot):
        start = band_start(n)
        m = mx_ref[sslot]
        pt = jnp.exp2(s_ref[sslot] - m).astype(BF16)
        va = jnp.concatenate([vt_ref[0, :, pl.ds(CTX_LEN + start, WIN_BAND)], vt_ref[0, :, 0:CTX_LEN]], axis=1)
        res = _dot(jnp.concatenate([va, ones_loc], axis=0), pt)
        write(res, m, pl.multiple_of(CTX_LEN + n * A_BLOCK, A_BLOCK))

    for blk in range(CTX_LEN // A_BLOCK):
        prep_q(blk * A_BLOCK, 0)
        st = _dot(k_ref[0, 0:CTX_LEN, :], qs_ref[0])
        m = jnp.maximum(jnp.max(st, axis=0, keepdims=True), sink)
        pt = jnp.exp2(st - m).astype(BF16)
        res = _dot(jnp.concatenate([vt_ref[0, :, 0:CTX_LEN], ones_ctx], axis=0), pt)
        write(res, m, blk * A_BLOCK)

    def some_blocks(t, carry):
        for u in range(PIPE_UNROLL):
            nxt = jnp.minimum(PIPE_UNROLL * t + u + 1, nb - 1)
            prep_q(pl.multiple_of(CTX_LEN + nxt * A_BLOCK, A_BLOCK), (u + 1) % 2)
            scores((u + 1) % 2, nxt, (u + 1) % 2)
            update(PIPE_UNROLL * t + u, u % 2)
        return carry

    prep_q(CTX_LEN, 0)
    scores(0, 0, 0)
    lax.fori_loop(0, nb // PIPE_UNROLL, some_blocks, 0)


def _window_mask():
    o = jnp.arange(3)[:, None, None]
    key = jnp.arange(WIN_BAND)[None, :, None]
    qry = jnp.arange(A_BLOCK)[None, None, :]
    return jnp.where(jnp.abs(qry + o * A_BLOCK - key) <= A_WINDOW, 0.0, NEG_INF).astype(F32)


def _window_attn(qat, ka, vat, sink_row):
    b, _, n = qat.shape
    return pl.pallas_call(
        functools.partial(_window_kernel, seq=n - CTX_LEN),
        grid=(b, A_KV_HEADS),
        scratch_shapes=[pltpu.VMEM((2, WIN_BAND + CTX_LEN, WIN_LANES), F32),
                        pltpu.VMEM((2, 1, WIN_LANES), F32),
                        pltpu.VMEM((2, 2 * HEAD_DIM, WIN_LANES), BF16)],
        in_specs=[
            pl.BlockSpec((1, GQA_R * HEAD_DIM, n), lambda bb, g: (bb, g, 0)),
            pl.BlockSpec((1, n, KVA_W), lambda bb, g: (bb, 0, 0)),
            pl.BlockSpec((1, HEAD_DIM, n), lambda bb, g: (bb, g, 0)),
            pl.BlockSpec((1, 1, WIN_LANES), lambda bb, g: (g, 0, 0)),
            _const_spec((3, WIN_BAND, A_BLOCK)),
        ],
        out_specs=pl.BlockSpec((1, GQA_R * HEAD_DIM, n), lambda bb, g: (bb, g, 0)),
        out_shape=jax.ShapeDtypeStruct((b, QA_W, n), BF16),
        compiler_params=_params("arbitrary", "arbitrary"),
        name="window_attn",
    )(qat, ka, vat, sink_row, _window_mask())


NA_ITEM_ROWS = LANES // GRID_W
NA_WIN_ROWS = NA_ROWS + NA_ITEM_ROWS
NA_HG = 4
NA_W = NA_HG * HEAD_DIM
NA_LANES = NA_HG * LANES
NA_BIAS_N = 2 * NA_ROWS


def _na_kernel(qt_ref, k_ref, vt_ref, ta_ref, tb_ref, o_ref, s_ref, mx_ref, qs_ref, *, rows_n):
    nitems = rows_n // NA_ITEM_ROWS
    nloc = NA_WIN_ROWS * GRID_W
    assert nitems % PIPE_UNROLL == 0 and (rows_n - NA_WIN_ROWS) % 2 == 0 and nloc % LANES == 0
    ones_loc = jnp.ones((SUM_ROWS, nloc + CTX_LEN), BF16)
    ones_ctx = jnp.ones((SUM_ROWS, CTX_LEN), BF16)

    def prep_q(col0, qslot):
        qt = qt_ref[0, :, pl.ds(col0, LANES)]
        for h in range(NA_HG):
            rows = slice(h * HEAD_DIM, (h + 1) * HEAD_DIM)
            qs_ref[qslot, rows, h * LANES:(h + 1) * LANES] = qt[rows]

    def win_start(j):
        return jnp.clip(NA_ITEM_ROWS * j - NA_ROWS // 2, 0, rows_n - NA_WIN_ROWS)

    def scores(qslot, j, sslot):
        ru = win_start(j)
        k0 = pl.multiple_of(CTX_LEN + ru * GRID_W, LANES)
        keys = jnp.concatenate([k_ref[0, pl.ds(k0, nloc), :], k_ref[0, 0:CTX_LEN, :]], axis=0)
        st = _dot(keys, qs_ref[qslot])
        mx = jnp.max(st[nloc:], axis=0, keepdims=True)
        s_ref[sslot, nloc:] = st[nloc:]
        for i in range(NA_WIN_ROWS):
            idx = []
            for e in range(NA_ITEM_ROWS):
                r = NA_ITEM_ROWS * j + e
                r0 = jnp.clip(r - NA_ROWS // 2, 0, rows_n - NA_ROWS)
                seen = (ru + i >= r0) & (ru + i < r0 + NA_ROWS)
                idx.append(jnp.where(seen, ru + i - r + NA_ROWS - 1, NA_BIAS_N - 1))
            rows = slice(i * GRID_W, (i + 1) * GRID_W)
            bias = ta_ref[0, idx[0]] + tb_ref[0, idx[1]]
            blk = st[rows] + jnp.concatenate([bias[h] for h in range(NA_HG)], axis=1)
            s_ref[sslot, rows] = blk
            mx = jnp.maximum(mx, jnp.max(blk, axis=0, keepdims=True))
        mx_ref[sslot] = mx

    def pv_write(va, ones, pt, col0):
        for pair in range(NA_HG // 2):
            rows = slice(pair * LANES, (pair + 1) * LANES)
            res = _dot(jnp.concatenate([va[rows], ones], axis=0),
                       pt[:, pair * 2 * LANES:(pair + 1) * 2 * LANES])
            for hh in range(2):
                lanes = slice(hh * LANES, (hh + 1) * LANES)
                o = res[hh * HEAD_DIM:(hh + 1) * HEAD_DIM, lanes] / res[LANES:LANES + 1, lanes]
                h = 2 * pair + hh
                o_ref[0, h * HEAD_DIM:(h + 1) * HEAD_DIM, pl.ds(col0, LANES)] = o.astype(o_ref.dtype)

    def update(j, sslot):
        k0 = pl.multiple_of(CTX_LEN + win_start(j) * GRID_W, LANES)
        pt = jnp.exp2(s_ref[sslot] - mx_ref[sslot]).astype(BF16)
        va = jnp.concatenate([vt_ref[0, :, pl.ds(k0, nloc)], vt_ref[0, :, 0:CTX_LEN]], axis=1)
        pv_write(va, ones_loc, pt, pl.multiple_of(CTX_LEN + j * LANES, LANES))

    qs_ref[...] = jnp.zeros(qs_ref.shape, BF16)
    for blk in range(CTX_LEN // LANES):
        prep_q(blk * LANES, 0)
        st = _dot(k_ref[0, 0:CTX_LEN, :], qs_ref[0])
        pt = jnp.exp2(st - jnp.max(st, axis=0, keepdims=True)).astype(BF16)
        pv_write(vt_ref[0, :, 0:CTX_LEN], ones_ctx, pt, blk * LANES)

    def some_items(t, carry):
        for u in range(PIPE_UNROLL):
            nxt = jnp.minimum(PIPE_UNROLL * t + u + 1, nitems - 1)
            prep_q(pl.multiple_of(CTX_LEN + nxt * LANES, LANES), (u + 1) % 2)
            scores((u + 1) % 2, nxt, (u + 1) % 2)
            update(PIPE_UNROLL * t + u, u % 2)
        return carry

    prep_q(CTX_LEN, 0)
    scores(0, 0, 0)
    lax.fori_loop(0, nitems // PIPE_UNROLL, some_items, 0)


def _na_attn(qbt, kb, vbt, ta, tb):
    b, _, n = qbt.shape
    rows_n = (n - CTX_LEN) // GRID_W
    assert rows_n >= NA_WIN_ROWS
    nkeys = NA_WIN_ROWS * GRID_W + CTX_LEN
    tspec = pl.BlockSpec((1, NA_BIAS_N, NA_HG, GRID_W, LANES), lambda bb, hg: (hg, 0, 0, 0, 0))
    return pl.pallas_call(
        functools.partial(_na_kernel, rows_n=rows_n),
        grid=(b, B_HEADS // NA_HG),
        scratch_shapes=[pltpu.VMEM((2, nkeys, NA_LANES), F32),
                        pltpu.VMEM((2, 1, NA_LANES), F32),
                        pltpu.VMEM((2, NA_W, NA_LANES), BF16)],
        in_specs=[
            pl.BlockSpec((1, NA_W, n), lambda bb, hg: (bb, hg, 0)),
            pl.BlockSpec((1, n, NA_W), lambda bb, hg: (bb, 0, hg)),
            pl.BlockSpec((1, NA_W, n), lambda bb, hg: (bb, hg, 0)),
            tspec, tspec,
        ],
        out_specs=pl.BlockSpec((1, NA_W, n), lambda bb, hg: (bb, hg, 0)),
        out_shape=jax.ShapeDtypeStruct((b, QKVB_W, n), BF16),
        compiler_params=_params("arbitrary", "arbitrary"),
        name="na_attn",
    )(qbt, kb, vbt, ta, tb)


def _mlp_tail(xs, y, mods, gn, w1_ref, w2_ref):
    x1s = [x + m[0] * y[k * TM:(k + 1) * TM] for k, (x, m) in enumerate(zip(xs, mods))]
    h = jnp.concatenate([_norm_mod(x1, gn, m[1], m[2]).astype(BF16) for x1, m in zip(x1s, mods)], axis=0)
    mlp = None
    for j in range(D_FF // D_MODEL):
        cols = slice(j * D_MODEL, (j + 1) * D_MODEL)
        u = jnp.maximum(_dot(h, w1_ref[:, cols]), 0.0)
        part = _dot((u * u).astype(BF16), w2_ref[cols, :])
        mlp = part if mlp is None else mlp + part
    return [x1 + m[3] * mlp[k * TM:(k + 1) * TM] for k, (x1, m) in enumerate(zip(x1s, mods))]


def _even_out_kernel(ctx_ref, *refs):
    x_refs = refs[:NSUB]
    (oat_ref, obt_ref, g1_ref, sh2_ref, sc2_ref, g2_ref, gn_ref,
     woa_ref, wob_ref, w1_ref, w2_ref, o_ref) = refs[NSUB:]
    oa = oat_ref[0].astype(F32).T.astype(BF16)
    ob = obt_ref[0].astype(F32).T.astype(BF16)
    y = _dot(oa, woa_ref[...]) + _dot(ob, wob_ref[...])
    xs = [_joint_tile(ctx_ref, x_refs, k) for k in range(NSUB)]
    mods = [[_pick_mod(r, _is_ctx(k)) for r in (g1_ref, sh2_ref, sc2_ref, g2_ref)] for k in range(NSUB)]
    for k, out in enumerate(_mlp_tail(xs, y, mods, gn_ref[...], w1_ref, w2_ref)):
        o_ref[0, k * TM:(k + 1) * TM, :] = out


def _even_out(ctx, x, oa, ob, mods, gn, woa, wob, w1, w2):
    b, s, d = x.shape
    n = CTX_LEN + s
    tmb = NSUB * TM
    return pl.pallas_call(
        _even_out_kernel,
        grid=(b, n // tmb),
        in_specs=_joint_specs(d) + [
            pl.BlockSpec((1, QA_W, tmb), lambda bb, i: (bb, 0, i)),
            pl.BlockSpec((1, QKVB_W, tmb), lambda bb, i: (bb, 0, i)),
            _mod_spec(2), _mod_spec(3), _mod_spec(4), _mod_spec(5),
            _const_spec((1, d)),
            _const_spec((QA_W, d)), _const_spec((QKVB_W, d)),
            _const_spec((d, D_FF)), _const_spec((D_FF, d)),
        ],
        out_specs=pl.BlockSpec((1, tmb, d), lambda bb, i: (bb, i, 0)),
        out_shape=jax.ShapeDtypeStruct((b, n, d), F32),
        compiler_params=_params("arbitrary", "arbitrary"),
        name="even_out_mlp",
    )(ctx, *([x] * NSUB), oa, ob, mods, mods, mods, mods, gn, woa, wob, w1, w2)


NSUB_ODD = 2


def _odd_out_kernel(*refs):
    x_refs = refs[:NSUB_ODD]
    ot_ref, g1_ref, sh2_ref, sc2_ref, g2_ref, gn_ref, wo_ref, w1_ref, w2_ref, o_ref = refs[NSUB_ODD:]
    o = ot_ref[0].astype(F32).T.astype(BF16)
    y = _dot(o, wo_ref[...])
    mods = [[g1_ref[0], sh2_ref[0], sc2_ref[0], g2_ref[0]]] * NSUB_ODD
    for k, out in enumerate(_mlp_tail([r[0] for r in x_refs], y, mods, gn_ref[...], w1_ref, w2_ref)):
        o_ref[0, k * TM:(k + 1) * TM, :] = out


def _odd_out(xa, ot, mods, gn, wo, w1, w2):
    b, n, d = xa.shape
    s = n - CTX_LEN
    tmb = NSUB_ODD * TM
    assert s % tmb == 0
    lat = lambda k: pl.BlockSpec((1, TM, d), lambda bb, i: (bb, NSUB_ODD * i + k + 1, 0))
    return pl.pallas_call(
        _odd_out_kernel,
        grid=(b, s // tmb),
        in_specs=[lat(k) for k in range(NSUB_ODD)] + [
            pl.BlockSpec((1, d, tmb), lambda bb, i: (bb, 0, i)),
            _mod_spec(2, True), _mod_spec(3, True), _mod_spec(4, True), _mod_spec(5, True),
            _const_spec((1, d)),
            _const_spec((d, d)),
            _const_spec((d, D_FF)), _const_spec((D_FF, d)),
        ],
        out_specs=pl.BlockSpec((1, tmb, d), lambda bb, i: (bb, i, 0)),
        out_shape=jax.ShapeDtypeStruct((b, s, d), F32),
        compiler_params=_params("arbitrary", "arbitrary"),
        name="odd_out_mlp",
    )(*([xa] * NSUB_ODD), ot, mods, mods, mods, mods, gn, wo, w1, w2)


ODD_IN_COLS = C_Q_RANK + C_KV_RANK + LANES
Q_NOPE_W = C_HEADS * C_NOPE
Q_ROPE_W = C_HEADS * C_ROPE


def _odd_proj_kernel(x_ref, sh_ref, sc_ref, g_ref, win_ref, qag_ref, kvg_ref, wuq_ref, wukv_ref,
                     gq_ref, gk_ref, e64_ref, e32_ref, cos_ref, sin_ref,
                     qn_ref, qr_ref, kn_ref, kr_ref, vt_ref):
    def rms(t, g):
        return (t * lax.rsqrt(jnp.mean(t * t, axis=-1, keepdims=True) + NORM_EPS) * g).astype(BF16)

    gq = gq_ref[...]
    gk = gk_ref[...]
    for k in range(NSUB):
        rows = slice(k * TM, (k + 1) * TM)
        h = _norm_mod(x_ref[0, rows, :], g_ref[...],
                      _pick_mod(sh_ref, _is_ctx(k)), _pick_mod(sc_ref, _is_ctx(k))).astype(BF16)
        y = _dot(h, win_ref[...])
        cq = y[:, :C_Q_RANK]
        ckv = y[:, C_Q_RANK:C_Q_RANK + C_KV_RANK]
        kr = y[:, C_Q_RANK + C_KV_RANK:]
        q = _dot(rms(cq, qag_ref[...]), wuq_ref[...])
        kv = _dot(rms(ckv, kvg_ref[...]), wukv_ref[...])
        cos = cos_ref[rows, :]
        sin = sin_ref[rows, :]
        qn = _group_rms(q[:, :Q_NOPE_W], e64_ref[...], C_NOPE) * gq[:, :Q_NOPE_W]
        qn_ref[0, :, rows] = qn.T.astype(BF16)
        qr = _group_rms(q[:, Q_NOPE_W:], e32_ref[...], C_ROPE) * gq[:, Q_NOPE_W:]
        qr_ref[0, :, rows] = _rope(qr, cos, sin, C_ROPE // 4).T.astype(BF16)
        kn = _group_rms(kv[:, :Q_NOPE_W], e64_ref[...], C_NOPE) * gk[:, :Q_NOPE_W]
        kn_ref[0, rows, :] = kn.astype(BF16)
        krn = _group_rms(kr, e32_ref[...], C_ROPE) * gk[:, Q_NOPE_W:]
        kr_ref[0, rows, :] = _rope(krn, cos, sin, C_ROPE // 4).astype(BF16)
        vt_ref[0, :, rows] = kv[:, Q_NOPE_W:].T.astype(BF16)


def _odd_proj(xa, mods, g, win, qag, kvg, wuq, wukv, gq, gk, e64, e32, cos, sin):
    b, n, d = xa.shape
    tmb = NSUB * TM
    assert n % tmb == 0
    tok = lambda wd: pl.BlockSpec((1, tmb, wd), lambda bb, i: (bb, i, 0))
    return pl.pallas_call(
        _odd_proj_kernel,
        grid=(b, n // tmb),
        in_specs=[
            tok(d), _mod_spec(0), _mod_spec(1),
            _const_spec((1, d)),
            _const_spec((d, ODD_IN_COLS)),
            _const_spec((1, C_Q_RANK)), _const_spec((1, C_KV_RANK)),
            _const_spec((C_Q_RANK, Q_NOPE_W + Q_ROPE_W)),
            _const_spec((C_KV_RANK, 2 * Q_NOPE_W)),
            _const_spec((1, Q_NOPE_W + Q_ROPE_W)), _const_spec((1, Q_NOPE_W + LANES)),
            _const_spec((MXU_DIM, MXU_DIM)), _const_spec((MXU_DIM, MXU_DIM)),
            pl.BlockSpec((tmb, LANES), lambda bb, i: (i, 0)),
            pl.BlockSpec((tmb, LANES), lambda bb, i: (i, 0)),
        ],
        out_specs=[pl.BlockSpec((1, Q_NOPE_W, tmb), lambda bb, i: (bb, 0, i)),
                   pl.BlockSpec((1, Q_ROPE_W, tmb), lambda bb, i: (bb, 0, i)),
                   tok(Q_NOPE_W), tok(LANES),
                   pl.BlockSpec((1, Q_NOPE_W, tmb), lambda bb, i: (bb, 0, i))],
        out_shape=[jax.ShapeDtypeStruct((b, Q_NOPE_W, n), BF16),
                   jax.ShapeDtypeStruct((b, Q_ROPE_W, n), BF16),
                   jax.ShapeDtypeStruct((b, n, Q_NOPE_W), BF16),
                   jax.ShapeDtypeStruct((b, n, LANES), BF16),
                   jax.ShapeDtypeStruct((b, Q_NOPE_W, n), BF16)],
        compiler_params=_params("arbitrary", "arbitrary"),
        name="odd_proj",
    )(xa, mods, mods, g, win, qag, kvg, wuq, wukv, gq, gk, e64, e32, cos, sin)


SUM_ROWS = 16


def _kv_tile(n):
    for t in (768, 256):
        if n % t == 0 and (n // t) % 2 == 1:
            return t
    raise ValueError(f"joint sequence length {n} has no odd split into 256-multiples")


def _flash_kernel(qnt_ref, qrt_ref, kn_ref, kr_ref, vt_ref, o_ref,
                  s_ref, mx_ref, m_ref, acc_ref, qt_ref, *, tk, tq):
    p = pl.program_id(1)
    n = kn_ref.shape[1]
    nq = (n - CTX_LEN) // tq
    nch = n // tk
    assert nq % 2 == 0 and nch % 2 == 1
    ones = jnp.ones((SUM_ROWS, tk), BF16)
    row = lax.broadcasted_iota(jnp.int32, (LANES, tq), 0)

    def prep_q(qi, qslot):
        c0 = pl.multiple_of(CTX_LEN + qi * tq, tq)
        qn = qnt_ref[0, :, pl.ds(c0, tq)]
        qr = qrt_ref[0, :, pl.ds(c0, tq)]
        zero = jnp.zeros_like(qn)
        for hh in range(2):
            off = ((2 * p + hh) % 4) * C_ROPE
            keep_n = (row < C_NOPE) if hh == 0 else (row >= C_NOPE)
            keep_r = (row >= off) & (row < off + C_ROPE)
            qt_ref[qslot, 0:LANES, hh * tq:(hh + 1) * tq] = jnp.where(keep_n, qn, zero)
            qt_ref[qslot, LANES:, hh * tq:(hh + 1) * tq] = jnp.where(keep_r, qr, zero)

    def scores(qslot, j, sslot):
        k0 = j * tk
        kcat = jnp.concatenate([kn_ref[0, pl.ds(k0, tk), :], kr_ref[0, pl.ds(k0, tk), :]], axis=1)
        st = _dot(kcat, qt_ref[qslot])
        s_ref[sslot] = st
        mx_ref[sslot] = jnp.max(st, axis=0, keepdims=True)

    def update(j, sslot):
        k0 = j * tk
        m_old = m_ref[...]
        m_new = jnp.maximum(m_old, mx_ref[sslot])
        alpha = jnp.exp2(m_old - m_new)
        pt = jnp.exp2(s_ref[sslot] - m_new).astype(BF16)
        va = jnp.concatenate([vt_ref[0, :, pl.ds(k0, tk)], ones], axis=0)
        m_ref[...] = m_new
        acc_ref[...] = alpha * acc_ref[...] + _dot(va, pt)

    def reset():
        m_ref[...] = jnp.full(m_ref.shape, NEG_INF, F32)
        acc_ref[...] = jnp.zeros(acc_ref.shape, F32)

    def finish(qi):
        acc = acc_ref[...]
        c0 = pl.multiple_of(qi * tq, tq)
        o_ref[0, 0:C_NOPE, pl.ds(c0, tq)] = (acc[0:C_NOPE, :tq] / acc[LANES:LANES + 1, :tq]).astype(o_ref.dtype)
        o_ref[0, C_NOPE:, pl.ds(c0, tq)] = (acc[C_NOPE:LANES, tq:] / acc[LANES:LANES + 1, tq:]).astype(o_ref.dtype)
        reset()

    def run_tile(qi, slot, next_qi):
        other = 1 - slot
        for j in range(nch - 1):
            scores(slot, j + 1, other if j % 2 == 0 else slot)
            update(j, slot if j % 2 == 0 else other)
        prep_q(next_qi, other)
        scores(other, 0, other)
        update(nch - 1, slot)
        finish(qi)

    def two_tiles(t2, carry):
        run_tile(2 * t2, 0, 2 * t2 + 1)
        run_tile(2 * t2 + 1, 1, jnp.minimum(2 * t2 + 2, nq - 1))
        return carry

    reset()
    prep_q(0, 0)
    scores(0, 0, 0)
    lax.fori_loop(0, nq // 2, two_tiles, 0)


def _flash_attn(qnt, qrt, kn, kr, vt):
    b, n, _ = kn.shape
    s = n - CTX_LEN
    tq = TM
    tk = _kv_tile(n)
    return pl.pallas_call(
        functools.partial(_flash_kernel, tk=tk, tq=tq),
        grid=(b, C_HEADS // 2),
        scratch_shapes=[pltpu.VMEM((2, tk, 2 * tq), F32),
                        pltpu.VMEM((2, 1, 2 * tq), F32),
                        pltpu.VMEM((1, 2 * tq), F32),
                        pltpu.VMEM((LANES + SUM_ROWS, 2 * tq), F32),
                        pltpu.VMEM((2, 2 * LANES, 2 * tq), BF16)],
        in_specs=[
            pl.BlockSpec((1, LANES, n), lambda bb, p: (bb, p, 0)),
            pl.BlockSpec((1, LANES, n), lambda bb, p: (bb, p // 2, 0)),
            pl.BlockSpec((1, n, LANES), lambda bb, p: (bb, 0, p)),
            pl.BlockSpec((1, n, LANES), lambda bb, p: (bb, 0, 0)),
            pl.BlockSpec((1, LANES, n), lambda bb, p: (bb, p, 0)),
        ],
        out_specs=pl.BlockSpec((1, LANES, s), lambda bb, p: (bb, p, 0)),
        out_shape=jax.ShapeDtypeStruct((b, C_HEADS * C_NOPE, s), BF16),
        compiler_params=_params("arbitrary", "arbitrary"),
        name="mla_flash",
    )(qnt, qrt, kn, kr, vt)


def _rope_tables(seq, dim):
    t = jnp.arange(seq, dtype=jnp.int32)
    pos = jnp.stack([t // GRID_W, t % GRID_W], axis=0).astype(F32)
    half = dim // 2
    q = half // 2
    inv = ROPE_THETA ** (-jnp.arange(q, dtype=F32) / q)
    j = jnp.arange(dim)
    ang = pos[j // half].T * inv[j % q][None, :]
    sign = jnp.where((j % half) < q, -1.0, 1.0).astype(F32)
    cos = jnp.cos(ang)
    sin = jnp.sin(ang) * sign[None, :]
    reps = LANES // dim
    cos = jnp.tile(cos, (1, reps))
    sin = jnp.tile(sin, (1, reps))
    cos = jnp.concatenate([jnp.ones((CTX_LEN, LANES), F32), cos], axis=0)
    sin = jnp.concatenate([jnp.zeros((CTX_LEN, LANES), F32), sin], axis=0)
    return cos, sin


def _group_ones(group):
    r = jnp.arange(MXU_DIM) // group
    return (r[:, None] == r[None, :]).astype(BF16)


def _na_bias_tables(rpb, mult):
    cq = jnp.arange(GRID_W)
    c0 = jnp.clip(cq - NA_COLS // 2, 0, GRID_W - NA_COLS)
    col_ok = (cq[:, None] >= c0[None, :]) & (cq[:, None] < c0[None, :] + NA_COLS)
    dci = jnp.clip(cq[:, None] - cq[None, :], 1 - NA_COLS, NA_COLS - 1) + NA_COLS - 1
    pick = (dci[None] == jnp.arange(2 * NA_COLS - 1)[:, None, None]).astype(F32)
    tt = jnp.einsum("hdm,mkq->hdkq", rpb.astype(F32) * mult, pick, precision=lax.Precision.HIGHEST)
    tt = jnp.where(col_ok[None, None], tt, NEG_INF)
    tt = jnp.concatenate([tt, jnp.full_like(tt[:, :1], NEG_INF)], axis=1)
    zero = jnp.zeros_like(tt)

    def per_group(t):
        t = t.reshape(B_HEADS // NA_HG, NA_HG, NA_BIAS_N, GRID_W, LANES)
        return t.transpose(0, 2, 1, 3, 4)

    return per_group(jnp.concatenate([tt, zero], axis=-1)), per_group(jnp.concatenate([zero, tt], axis=-1))


def kernel(x, c, ctx, c_ctx, ada_w, ada_b, norm_mix, norm_mlp, mlp_w1, mlp_w2, e_w_in, e_w_out, a_q_norm, a_k_norm, a_sink, b_q_norm, b_k_norm, b_rpb, o_w_in, o_qa_norm, o_kva_norm, o_w_uq, o_w_ukv, o_qn_nope, o_qn_rope, o_kn_nope, o_kn_rope, o_w_out):
    bsz, seq, d = x.shape
    assert d == D_MODEL and ctx.shape[1] == CTX_LEN and seq % TM == 0 and ada_w.shape[0] == 2
    assert bsz + 1 <= 8

    cond = jnp.zeros((8, d), F32).at[:bsz].set(c).at[bsz].set(c_ctx)
    m = _adaln(cond, ada_w, ada_b)
    mods = [jnp.stack([jnp.broadcast_to(m[i, bsz], (bsz, 6 * d)), m[i, :bsz]], axis=1).reshape(2 * bsz, 1, 6 * d)
            for i in range(2)]

    e64 = _group_ones(HEAD_DIM)
    e32 = _group_ones(C_ROPE)

    w = e_w_in[0]
    w_ext = jnp.concatenate([w[:, 0:512], w[:, 512:640], w[:, 768:1280], w[:, 1280:1792],
                             w[:, 640:768], w[:, 1792:2304]], axis=1).astype(BF16)
    scale = HEAD_DIM ** -0.5
    log2e = math.log2(math.e)
    gains = jnp.concatenate([jnp.tile(a_q_norm[0], A_HEADS) * (scale * log2e), jnp.tile(a_k_norm[0], A_KV_HEADS),
                             jnp.tile(b_q_norm[0], B_HEADS) * (scale * log2e), jnp.tile(b_k_norm[0], B_HEADS)])[None, :]
    cos64, sin64 = _rope_tables(seq, HEAD_DIM)
    qat, ka, qbt, kb, vat, vbt = _even_proj(ctx, x, mods[0], norm_mix[0][None, :], w_ext, gains, e64, cos64, sin64)
    sink_row = (jnp.repeat(a_sink[0].reshape(A_KV_HEADS, GQA_R), A_BLOCK, axis=1) * log2e)[:, None, :]
    oa = _window_attn(qat, ka, vat, sink_row)
    ob = _na_attn(qbt, kb, vbt, *_na_bias_tables(b_rpb[0], log2e))
    wo = e_w_out[0].astype(BF16)
    xa = _even_out(ctx, x, oa, ob, mods[0], norm_mlp[0][None, :], wo[:512], wo[512:],
                   mlp_w1[0].astype(BF16), mlp_w2[0].astype(BF16))

    wi = o_w_in[0]
    win = jnp.concatenate([wi[:, :C_Q_RANK + C_KV_RANK]] + [wi[:, C_Q_RANK + C_KV_RANK:]] * 4, axis=1).astype(BF16)
    wuq = o_w_uq[0].reshape(C_Q_RANK, C_HEADS, C_NOPE + C_ROPE)
    wuq = jnp.concatenate([wuq[:, :, :C_NOPE].reshape(C_Q_RANK, -1), wuq[:, :, C_NOPE:].reshape(C_Q_RANK, -1)],
                          axis=1).astype(BF16)
    wukv = o_w_ukv[0].reshape(C_KV_RANK, C_HEADS, 2 * C_NOPE)
    wukv = jnp.concatenate([wukv[:, :, :C_NOPE].reshape(C_KV_RANK, -1), wukv[:, :, C_NOPE:].reshape(C_KV_RANK, -1)],
                           axis=1).astype(BF16)
    qscale = (C_NOPE + C_ROPE) ** -0.5 * math.log2(math.e)
    gq = (jnp.concatenate([jnp.tile(o_qn_nope[0], C_HEADS), jnp.tile(o_qn_rope[0], C_HEADS)]) * qscale)[None, :]
    gk = jnp.concatenate([jnp.tile(o_kn_nope[0], C_HEADS), jnp.tile(o_kn_rope[0], LANES // C_ROPE)])[None, :]
    cos32, sin32 = _rope_tables(seq, C_ROPE)
    qn, qr, kn, kr, vt = _odd_proj(xa, mods[1], norm_mix[1][None, :], win, o_qa_norm[0][None, :],
                                   o_kva_norm[0][None, :], wuq, wukv, gq, gk, e64, e32, cos32, sin32)
    ot = _flash_attn(qn, qr, kn, kr, vt)
    return _odd_out(xa, ot, mods[1], norm_mlp[1][None, :], o_w_out[0].astype(BF16),
                    mlp_w1[1].astype(BF16), mlp_w2[1].astype(BF16))
```

```python
import functools
import math

import jax
import jax.numpy as jnp
from jax import lax
from jax.experimental import pallas as pl
from jax.experimental.pallas import tpu as pltpu

F32 = jnp.float32
BF16 = jnp.bfloat16

D_MODEL = 1024
CTX_LEN = 256
GRID_W = 64
HEAD_DIM = 64
A_HEADS = 8
A_KV_HEADS = 2
A_WINDOW = 128
A_BLOCK = 128
B_HEADS = 8
NA_ROWS = 8
NA_COLS = 16
C_HEADS = 16
C_Q_RANK = 384
C_KV_RANK = 256
C_NOPE = 64
C_ROPE = 32
D_FF = 4 * D_MODEL
ROPE_THETA = 10000.0
NORM_EPS = 1e-6
NEG_INF = -1e30

LANES = 128
MXU_DIM = 256
TM = CTX_LEN
V7X_VMEM_BYTES = 64 * 1024 * 1024
VMEM_VALUE_BYTES = 16 * 1024 * 1024


ARB2 = ("arbitrary", "arbitrary")


def _dot(a, b):
    return jnp.dot(a, b, preferred_element_type=F32)


def _nbytes(shape, dtype, buffers=2):
    return math.prod(shape) * jnp.dtype(dtype).itemsize * buffers


def _params(sem, declared_bytes):
    limit = min(declared_bytes + VMEM_VALUE_BYTES, V7X_VMEM_BYTES * 7 // 8)
    return pltpu.CompilerParams(dimension_semantics=sem, vmem_limit_bytes=limit)


def _norm_mod(x, g, sh, sc):
    ms = jnp.mean(x * x, axis=-1, keepdims=True)
    return (x * lax.rsqrt(ms + NORM_EPS) * g) * (1.0 + sc) + sh


def _group_rms(y, e, group):
    out = []
    for c0 in range(0, y.shape[1], MXU_DIM):
        cw = min(MXU_DIM, y.shape[1] - c0)
        yc = y[:, c0:c0 + cw]
        ss = _dot((yc * yc).astype(BF16), e[:cw, :cw])
        out.append(yc * lax.rsqrt(ss * (1.0 / group) + NORM_EPS))
    return out[0] if len(out) == 1 else jnp.concatenate(out, axis=1)


def _rope(r, cos, sin, half):
    w = r.shape[1]
    reps = w // LANES
    if reps > 1:
        cos = jnp.concatenate([cos] * reps, axis=1)
        sin = jnp.concatenate([sin] * reps, axis=1)
    lane = lax.broadcasted_iota(jnp.int32, r.shape, 1)
    up = pltpu.roll(r, w - half, axis=1)
    dn = pltpu.roll(r, half, axis=1)
    sw = jnp.where((lane & half) == 0, up, dn)
    return r * cos + sw * sin


def _ada_kernel(cond_ref, w_ref, b_ref, o_ref):
    c = cond_ref[...]
    s = (c * jax.nn.sigmoid(c)).astype(BF16)
    o_ref[0] = _dot(s, w_ref[0].astype(BF16)) + b_ref[0]


def _adaln(cond, ada_w, ada_b):
    depth, d, n6 = ada_w.shape
    tn = 1536
    return pl.pallas_call(
        _ada_kernel,
        grid=(depth, n6 // tn),
        in_specs=[
            pl.BlockSpec((8, d), lambda l, j: (0, 0)),
            pl.BlockSpec((1, d, tn), lambda l, j: (l, 0, j)),
            pl.BlockSpec((1, 1, tn), lambda l, j: (l, 0, j)),
        ],
        out_specs=pl.BlockSpec((1, 8, tn), lambda l, j: (l, 0, j)),
        out_shape=jax.ShapeDtypeStruct((depth, 8, n6), F32),
        compiler_params=_params(ARB2, _nbytes((d + 2 * 8, tn), F32)),
        name="adaln",
    )(cond, ada_w, ada_b.reshape(depth, 1, n6))


NSUB = 3


def _mod_spec(chunk, lat_only=False):
    if lat_only:
        return pl.BlockSpec((1, 1, D_MODEL), lambda b, i: (2 * b + 1, 0, chunk))
    return pl.BlockSpec((2, 1, D_MODEL), lambda b, i: (b, 0, chunk))


def _pick_mod(ref, is_ctx):
    m = ref[...]
    return jnp.where(is_ctx, m[0], m[1])


def _is_ctx(k):
    return NSUB * pl.program_id(1) + k == 0


def _const_spec(shape):
    nd = len(shape)
    return pl.BlockSpec(shape, lambda *_: (0,) * nd, pipeline_mode=pl.Buffered(1))


QA_W = A_HEADS * HEAD_DIM
KVA_W = A_KV_HEADS * HEAD_DIM
QKVB_W = B_HEADS * HEAD_DIM
EVEN_ROPED = QA_W + KVA_W
EVEN_QB = EVEN_ROPED + KVA_W
EVEN_VB = EVEN_QB + 2 * QKVB_W
EVEN_COLS = EVEN_VB + QKVB_W
EVEN_NORMED = EVEN_ROPED + 2 * QKVB_W


def _joint_specs(d):
    lat = lambda k: pl.BlockSpec((1, TM, d), lambda bb, i: (bb, jnp.maximum(NSUB * i + k - 1, 0), 0))
    return [pl.BlockSpec((1, CTX_LEN, d), lambda bb, i: (bb, 0, 0))] + [lat(k) for k in range(NSUB)]


def _joint_tile(ctx_ref, x_refs, k):
    return jnp.where(_is_ctx(k), ctx_ref[0], x_refs[k][0])


def _even_proj_kernel(ctx_ref, *refs):
    x_refs = refs[:NSUB]
    (sh_ref, sc_ref, g_ref, w_ref, gains_ref, e_ref, cos_ref, sin_ref,
     qat_ref, ka_ref, qb_ref, kb_ref, vat_ref, vb_ref) = refs[NSUB:]
    for k in range(NSUB):
        rows = slice(k * TM, (k + 1) * TM)
        h = _norm_mod(_joint_tile(ctx_ref, x_refs, k), g_ref[...],
                      _pick_mod(sh_ref, _is_ctx(k)), _pick_mod(sc_ref, _is_ctx(k))).astype(BF16)
        y = _dot(h, w_ref[...])
        gains = gains_ref[...]
        ya = _group_rms(y[:, :EVEN_ROPED], e_ref[...], HEAD_DIM) * gains[:, :EVEN_ROPED]
        r = _rope(ya, cos_ref[rows, :], sin_ref[rows, :], HEAD_DIM // 4)
        yb = _group_rms(y[:, EVEN_QB:EVEN_VB], e_ref[...], HEAD_DIM) * gains[:, EVEN_ROPED:]
        qat_ref[0, :, rows] = r[:, 0:QA_W].T.astype(BF16)
        ka_ref[0, rows, :] = r[:, QA_W:].astype(BF16)
        qb_ref[0, :, rows] = yb[:, :QKVB_W].T.astype(BF16)
        kb_ref[0, rows, :] = yb[:, QKVB_W:].astype(BF16)
        vat_ref[0, :, rows] = y[:, EVEN_ROPED:EVEN_QB].T.astype(BF16)
        vb_ref[0, :, rows] = y[:, EVEN_VB:].T.astype(BF16)


def _even_proj(ctx, x, mods, g, w, gains, e64, cos, sin):
    b, s, d = x.shape
    n = CTX_LEN + s
    tmb = NSUB * TM
    assert n % tmb == 0
    tok = lambda wd: (pl.BlockSpec((1, tmb, wd), lambda bb, i: (bb, i, 0)),
                      jax.ShapeDtypeStruct((b, n, wd), BF16))
    tr = lambda wd: (pl.BlockSpec((1, wd, tmb), lambda bb, i: (bb, 0, i)),
                     jax.ShapeDtypeStruct((b, wd, n), BF16))
    outs = (tr(QA_W), tok(KVA_W), tr(QKVB_W), tok(QKVB_W), tr(KVA_W), tr(QKVB_W))
    return pl.pallas_call(
        _even_proj_kernel,
        grid=(b, n // tmb),
        in_specs=_joint_specs(d) + [
            _mod_spec(0), _mod_spec(1),
            _const_spec((1, d)),
            _const_spec((d, EVEN_COLS)),
            _const_spec((1, EVEN_NORMED)),
            _const_spec((MXU_DIM, MXU_DIM)),
            pl.BlockSpec((tmb, LANES), lambda bb, i: (i, 0)),
            pl.BlockSpec((tmb, LANES), lambda bb, i: (i, 0)),
        ],
        out_specs=[o[0] for o in outs],
        out_shape=[o[1] for o in outs],
        compiler_params=_params(ARB2, _nbytes((CTX_LEN + tmb, d), F32) + _nbytes((d, EVEN_COLS), BF16, 1)
                                + _nbytes((tmb, EVEN_COLS), BF16) + 2 * _nbytes((tmb, LANES), F32)),
        name="even_proj",
    )(ctx, *([x] * NSUB), mods, mods, g, w, gains, e64, cos, sin)


GQA_R = A_HEADS // A_KV_HEADS
WIN_BAND = 3 * A_BLOCK
WIN_LANES = GQA_R * A_BLOCK


PIPE_UNROLL = 2


def _window_kernel(qt_ref, k_ref, vt_ref, sink_ref, wm_ref, o_ref, s_ref, mx_ref, qs_ref, *, seq):
    g = pl.program_id(1)
    nb = seq // A_BLOCK
    nkeys = WIN_BAND + CTX_LEN
    assert nb % PIPE_UNROLL == 0 and PIPE_UNROLL % 2 == 0
    sink = sink_ref[0]
    ones_loc = jnp.ones((SUM_ROWS, nkeys), BF16)
    ones_ctx = jnp.ones((SUM_ROWS, CTX_LEN), BF16)

    def prep_q(col0, qslot):
        qt = qt_ref[0, :, pl.ds(col0, A_BLOCK)]
        zero = jnp.zeros((HEAD_DIM, A_BLOCK), BF16)
        for h in range(GQA_R):
            qh = qt[h * HEAD_DIM:(h + 1) * HEAD_DIM]
            lanes = slice(h * A_BLOCK, (h + 1) * A_BLOCK)
            qs_ref[qslot, 0:HEAD_DIM, lanes] = jnp.where(g == 0, qh, zero)
            qs_ref[qslot, HEAD_DIM:, lanes] = jnp.where(g == 0, zero, qh)

    def band_start(n):
        return pl.multiple_of(jnp.clip((n - 1) * A_BLOCK, 0, seq - WIN_BAND), A_BLOCK)

    def scores(qslot, n, sslot):
        start = band_start(n)
        keys = jnp.concatenate([k_ref[0, pl.ds(CTX_LEN + start, WIN_BAND), :], k_ref[0, 0:CTX_LEN, :]], axis=0)
        st = _dot(keys, qs_ref[qslot])
        mask = wm_ref[n - start // A_BLOCK]
        s_loc = st[:WIN_BAND] + jnp.concatenate([mask] * GQA_R, axis=1)
        s_ctx = st[WIN_BAND:]
        s_ref[sslot, 0:WIN_BAND] = s_loc
        s_ref[sslot, WIN_BAND:] = s_ctx
        mx_ref[sslot] = jnp.maximum(jnp.maximum(jnp.max(s_loc, axis=0, keepdims=True),
                                                jnp.max(s_ctx, axis=0, keepdims=True)), sink)

    def write(res, m, col0):
        o = res[0:HEAD_DIM] / (res[HEAD_DIM:HEAD_DIM + 1] + jnp.exp2(sink - m))
        for h in range(GQA_R):
            o_ref[0, h * HEAD_DIM:(h + 1) * HEAD_DIM, pl.ds(col0, A_BLOCK)] = (
                o[:, h * A_BLOCK:(h + 1) * A_BLOCK].astype(o_ref.dtype))

    def update(n, sslot):
        start = band_start(n)
        m = mx_ref[sslot]
        pt = jnp.exp2(s_ref[sslot] - m).astype(BF16)
        va = jnp.concatenate([vt_ref[0, :, pl.ds(CTX_LEN + start, WIN_BAND)], vt_ref[0, :, 0:CTX_LEN]], axis=1)
        res = _dot(jnp.concatenate([va, ones_loc], axis=0), pt)
        write(res, m, pl.multiple_of(CTX_LEN + n * A_BLOCK, A_BLOCK))

    for blk in range(CTX_LEN // A_BLOCK):
        prep_q(blk * A_BLOCK, 0)
        st = _dot(k_ref[0, 0:CTX_LEN, :], qs_ref[0])
        m = jnp.maximum(jnp.max(st, axis=0, keepdims=True), sink)
        pt = jnp.exp2(st - m).astype(BF16)
        res = _dot(jnp.concatenate([vt_ref[0, :, 0:CTX_LEN], ones_ctx], axis=0), pt)
        write(res, m, blk * A_BLOCK)

    def some_blocks(t, carry):
        for u in range(PIPE_UNROLL):
            nxt = jnp.minimum(PIPE_UNROLL * t + u + 1, nb - 1)
            prep_q(pl.multiple_of(CTX_LEN + nxt * A_BLOCK, A_BLOCK), (u + 1) % 2)
            scores((u + 1) % 2, nxt, (u + 1) % 2)
            update(PIPE_UNROLL * t + u, u % 2)
        return carry

    prep_q(CTX_LEN, 0)
    scores(0, 0, 0)
    lax.fori_loop(0, nb // PIPE_UNROLL, some_blocks, 0)


def _window_mask():
    o = jnp.arange(3)[:, None, None]
    key = jnp.arange(WIN_BAND)[None, :, None]
    qry = jnp.arange(A_BLOCK)[None, None, :]
    return jnp.where(jnp.abs(qry + o * A_BLOCK - key) <= A_WINDOW, 0.0, NEG_INF).astype(F32)


def _window_attn(qat, ka, vat, sink_row):
    b, _, n = qat.shape
    return pl.pallas_call(
        functools.partial(_window_kernel, seq=n - CTX_LEN),
        grid=(b, A_KV_HEADS),
        scratch_shapes=[pltpu.VMEM((2, WIN_BAND + CTX_LEN, WIN_LANES), F32),
                        pltpu.VMEM((2, 1, WIN_LANES), F32),
                        pltpu.VMEM((2, 2 * HEAD_DIM, WIN_LANES), BF16)],
        in_specs=[
            pl.BlockSpec((1, GQA_R * HEAD_DIM, n), lambda bb, g: (bb, g, 0)),
            pl.BlockSpec((1, n, KVA_W), lambda bb, g: (bb, 0, 0)),
            pl.BlockSpec((1, HEAD_DIM, n), lambda bb, g: (bb, g, 0)),
            pl.BlockSpec((1, 1, WIN_LANES), lambda bb, g: (g, 0, 0)),
            _const_spec((3, WIN_BAND, A_BLOCK)),
        ],
        out_specs=pl.BlockSpec((1, GQA_R * HEAD_DIM, n), lambda bb, g: (bb, g, 0)),
        out_shape=jax.ShapeDtypeStruct((b, QA_W, n), BF16),
        compiler_params=_params(ARB2, _nbytes((2 * GQA_R * HEAD_DIM + KVA_W + HEAD_DIM, n), BF16)
                                + _nbytes((WIN_BAND + CTX_LEN + 2 * HEAD_DIM, WIN_LANES), F32)
                                + _nbytes((3, WIN_BAND, A_BLOCK), F32, 1)),
        name="window_attn",
    )(qat, ka, vat, sink_row, _window_mask())


NA_ITEM_ROWS = LANES // GRID_W
NA_WIN_ROWS = NA_ROWS + NA_ITEM_ROWS
NA_HG = 4
NA_W = NA_HG * HEAD_DIM
NA_LANES = NA_HG * LANES
NA_BIAS_N = 2 * NA_ROWS


def _na_kernel(qt_ref, k_ref, vt_ref, ta_ref, tb_ref, o_ref, s_ref, mx_ref, qs_ref, *, rows_n):
    nitems = rows_n // NA_ITEM_ROWS
    nloc = NA_WIN_ROWS * GRID_W
    assert nitems % PIPE_UNROLL == 0 and (rows_n - NA_WIN_ROWS) % 2 == 0 and nloc % LANES == 0
    ones_loc = jnp.ones((SUM_ROWS, nloc + CTX_LEN), BF16)
    ones_ctx = jnp.ones((SUM_ROWS, CTX_LEN), BF16)

    def prep_q(col0, qslot):
        qt = qt_ref[0, :, pl.ds(col0, LANES)]
        for h in range(NA_HG):
            rows = slice(h * HEAD_DIM, (h + 1) * HEAD_DIM)
            qs_ref[qslot, rows, h * LANES:(h + 1) * LANES] = qt[rows]

    def win_start(j):
        return jnp.clip(NA_ITEM_ROWS * j - NA_ROWS // 2, 0, rows_n - NA_WIN_ROWS)

    def scores(qslot, j, sslot):
        ru = win_start(j)
        k0 = pl.multiple_of(CTX_LEN + ru * GRID_W, LANES)
        keys = jnp.concatenate([k_ref[0, pl.ds(k0, nloc), :], k_ref[0, 0:CTX_LEN, :]], axis=0)
        st = _dot(keys, qs_ref[qslot])
        mx = jnp.max(st[nloc:], axis=0, keepdims=True)
        s_ref[sslot, nloc:] = st[nloc:]
        for i in range(NA_WIN_ROWS):
            idx = []
            for e in range(NA_ITEM_ROWS):
                r = NA_ITEM_ROWS * j + e
                r0 = jnp.clip(r - NA_ROWS // 2, 0, rows_n - NA_ROWS)
                seen = (ru + i >= r0) & (ru + i < r0 + NA_ROWS)
                idx.append(jnp.where(seen, ru + i - r + NA_ROWS - 1, NA_BIAS_N - 1))
            rows = slice(i * GRID_W, (i + 1) * GRID_W)
            bias = ta_ref[0, idx[0]] + tb_ref[0, idx[1]]
            blk = st[rows] + jnp.concatenate([bias[h] for h in range(NA_HG)], axis=1)
            s_ref[sslot, rows] = blk
            mx = jnp.maximum(mx, jnp.max(blk, axis=0, keepdims=True))
        mx_ref[sslot] = mx

    def pv_write(va, ones, pt, col0):
        for pair in range(NA_HG // 2):
            rows = slice(pair * LANES, (pair + 1) * LANES)
            res = _dot(jnp.concatenate([va[rows], ones], axis=0),
                       pt[:, pair * 2 * LANES:(pair + 1) * 2 * LANES])
            for hh in range(2):
                lanes = slice(hh * LANES, (hh + 1) * LANES)
                o = res[hh * HEAD_DIM:(hh + 1) * HEAD_DIM, lanes] / res[LANES:LANES + 1, lanes]
                h = 2 * pair + hh
                o_ref[0, h * HEAD_DIM:(h + 1) * HEAD_DIM, pl.ds(col0, LANES)] = o.astype(o_ref.dtype)

    def update(j, sslot):
        k0 = pl.multiple_of(CTX_LEN + win_start(j) * GRID_W, LANES)
        pt = jnp.exp2(s_ref[sslot] - mx_ref[sslot]).astype(BF16)
        va = jnp.concatenate([vt_ref[0, :, pl.ds(k0, nloc)], vt_ref[0, :, 0:CTX_LEN]], axis=1)
        pv_write(va, ones_loc, pt, pl.multiple_of(CTX_LEN + j * LANES, LANES))

    qs_ref[...] = jnp.zeros(qs_ref.shape, BF16)
    for blk in range(CTX_LEN // LANES):
        prep_q(blk * LANES, 0)
        st = _dot(k_ref[0, 0:CTX_LEN, :], qs_ref[0])
        pt = jnp.exp2(st - jnp.max(st, axis=0, keepdims=True)).astype(BF16)
        pv_write(vt_ref[0, :, 0:CTX_LEN], ones_ctx, pt, blk * LANES)

    def some_items(t, carry):
        for u in range(PIPE_UNROLL):
            nxt = jnp.minimum(PIPE_UNROLL * t + u + 1, nitems - 1)
            prep_q(pl.multiple_of(CTX_LEN + nxt * LANES, LANES), (u + 1) % 2)
            scores((u + 1) % 2, nxt, (u + 1) % 2)
            update(PIPE_UNROLL * t + u, u % 2)
        return carry

    prep_q(CTX_LEN, 0)
    scores(0, 0, 0)
    lax.fori_loop(0, nitems // PIPE_UNROLL, some_items, 0)


def _na_attn(qbt, kb, vbt, ta, tb):
    b, _, n = qbt.shape
    rows_n = (n - CTX_LEN) // GRID_W
    assert rows_n >= NA_WIN_ROWS
    nkeys = NA_WIN_ROWS * GRID_W + CTX_LEN
    tspec = pl.BlockSpec((1, NA_BIAS_N, NA_HG, GRID_W, LANES), lambda bb, hg: (hg, 0, 0, 0, 0))
    return pl.pallas_call(
        functools.partial(_na_kernel, rows_n=rows_n),
        grid=(b, B_HEADS // NA_HG),
        scratch_shapes=[pltpu.VMEM((2, nkeys, NA_LANES), F32),
                        pltpu.VMEM((2, 1, NA_LANES), F32),
                        pltpu.VMEM((2, NA_W, NA_LANES), BF16)],
        in_specs=[
            pl.BlockSpec((1, NA_W, n), lambda bb, hg: (bb, hg, 0)),
            pl.BlockSpec((1, n, NA_W), lambda bb, hg: (bb, 0, hg)),
            pl.BlockSpec((1, NA_W, n), lambda bb, hg: (bb, hg, 0)),
            tspec, tspec,
        ],
        out_specs=pl.BlockSpec((1, NA_W, n), lambda bb, hg: (bb, hg, 0)),
        out_shape=jax.ShapeDtypeStruct((b, QKVB_W, n), BF16),
        compiler_params=_params(ARB2, _nbytes((4 * NA_W, n), BF16) + _nbytes((nkeys + NA_W, NA_LANES), F32)
                                + 2 * _nbytes((NA_BIAS_N, NA_HG, GRID_W, LANES), F32)),
        name="na_attn",
    )(qbt, kb, vbt, ta, tb)


def _mlp_tail(xs, y, mods, gn, w1_ref, w2_ref):
    x1s = [x + m[0] * y[k * TM:(k + 1) * TM] for k, (x, m) in enumerate(zip(xs, mods))]
    h = jnp.concatenate([_norm_mod(x1, gn, m[1], m[2]).astype(BF16) for x1, m in zip(x1s, mods)], axis=0)
    mlp = None
    for j in range(D_FF // D_MODEL):
        cols = slice(j * D_MODEL, (j + 1) * D_MODEL)
        u = jnp.maximum(_dot(h, w1_ref[:, cols]), 0.0)
        part = _dot((u * u).astype(BF16), w2_ref[cols, :])
        mlp = part if mlp is None else mlp + part
    return [x1 + m[3] * mlp[k * TM:(k + 1) * TM] for k, (x1, m) in enumerate(zip(x1s, mods))]


def _even_out_kernel(ctx_ref, *refs):
    x_refs = refs[:NSUB]
    (oat_ref, obt_ref, g1_ref, sh2_ref, sc2_ref, g2_ref, gn_ref,
     woa_ref, wob_ref, w1_ref, w2_ref, o_ref) = refs[NSUB:]
    oa = oat_ref[0].astype(F32).T.astype(BF16)
    ob = obt_ref[0].astype(F32).T.astype(BF16)
    y = _dot(oa, woa_ref[...]) + _dot(ob, wob_ref[...])
    xs = [_joint_tile(ctx_ref, x_refs, k) for k in range(NSUB)]
    mods = [[_pick_mod(r, _is_ctx(k)) for r in (g1_ref, sh2_ref, sc2_ref, g2_ref)] for k in range(NSUB)]
    for k, out in enumerate(_mlp_tail(xs, y, mods, gn_ref[...], w1_ref, w2_ref)):
        o_ref[0, k * TM:(k + 1) * TM, :] = out


def _even_out(ctx, x, oa, ob, mods, gn, woa, wob, w1, w2):
    b, s, d = x.shape
    n = CTX_LEN + s
    tmb = NSUB * TM
    return pl.pallas_call(
        _even_out_kernel,
        grid=(b, n // tmb),
        in_specs=_joint_specs(d) + [
            pl.BlockSpec((1, QA_W, tmb), lambda bb, i: (bb, 0, i)),
            pl.BlockSpec((1, QKVB_W, tmb), lambda bb, i: (bb, 0, i)),
            _mod_spec(2), _mod_spec(3), _mod_spec(4), _mod_spec(5),
            _const_spec((1, d)),
            _const_spec((QA_W, d)), _const_spec((QKVB_W, d)),
            _const_spec((d, D_FF)), _const_spec((D_FF, d)),
        ],
        out_specs=pl.BlockSpec((1, tmb, d), lambda bb, i: (bb, i, 0)),
        out_shape=jax.ShapeDtypeStruct((b, n, d), F32),
        compiler_params=_params(ARB2, _nbytes((CTX_LEN + 2 * tmb, d), F32) + _nbytes((QA_W + QKVB_W, tmb), BF16)
                                + _nbytes((d + 2 * D_FF, d), BF16, 1)),
        name="even_out_mlp",
    )(ctx, *([x] * NSUB), oa, ob, mods, mods, mods, mods, gn, woa, wob, w1, w2)


NSUB_ODD = 2


def _odd_out_kernel(*refs):
    x_refs = refs[:NSUB_ODD]
    ot_ref, g1_ref, sh2_ref, sc2_ref, g2_ref, gn_ref, wo_ref, w1_ref, w2_ref, o_ref = refs[NSUB_ODD:]
    o = ot_ref[0].astype(F32).T.astype(BF16)
    y = _dot(o, wo_ref[...])
    mods = [[g1_ref[0], sh2_ref[0], sc2_ref[0], g2_ref[0]]] * NSUB_ODD
    for k, out in enumerate(_mlp_tail([r[0] for r in x_refs], y, mods, gn_ref[...], w1_ref, w2_ref)):
        o_ref[0, k * TM:(k + 1) * TM, :] = out


def _odd_out(xa, ot, mods, gn, wo, w1, w2):
    b, n, d = xa.shape
    s = n - CTX_LEN
    tmb = NSUB_ODD * TM
    assert s % tmb == 0
    lat = lambda k: pl.BlockSpec((1, TM, d), lambda bb, i: (bb, NSUB_ODD * i + k + 1, 0))
    return pl.pallas_call(
        _odd_out_kernel,
        grid=(b, s // tmb),
        in_specs=[lat(k) for k in range(NSUB_ODD)] + [
            pl.BlockSpec((1, d, tmb), lambda bb, i: (bb, 0, i)),
            _mod_spec(2, True), _mod_spec(3, True), _mod_spec(4, True), _mod_spec(5, True),
            _const_spec((1, d)),
            _const_spec((d, d)),
            _const_spec((d, D_FF)), _const_spec((D_FF, d)),
        ],
        out_specs=pl.BlockSpec((1, tmb, d), lambda bb, i: (bb, i, 0)),
        out_shape=jax.ShapeDtypeStruct((b, s, d), F32),
        compiler_params=_params(ARB2, _nbytes((2 * tmb, d), F32) + _nbytes((d, tmb), BF16)
                                + _nbytes((d + 2 * D_FF, d), BF16, 1)),
        name="odd_out_mlp",
    )(*([xa] * NSUB_ODD), ot, mods, mods, mods, mods, gn, wo, w1, w2)


ODD_IN_COLS = C_Q_RANK + C_KV_RANK + LANES
Q_NOPE_W = C_HEADS * C_NOPE
Q_ROPE_W = C_HEADS * C_ROPE


def _odd_proj_kernel(x_ref, sh_ref, sc_ref, g_ref, win_ref, qag_ref, kvg_ref, wuq_ref, wukv_ref,
                     gq_ref, gk_ref, e64_ref, e32_ref, cos_ref, sin_ref,
                     qn_ref, qr_ref, kn_ref, kr_ref, vt_ref):
    def rms(t, g):
        return (t * lax.rsqrt(jnp.mean(t * t, axis=-1, keepdims=True) + NORM_EPS) * g).astype(BF16)

    gq = gq_ref[...]
    gk = gk_ref[...]
    for k in range(NSUB):
        rows = slice(k * TM, (k + 1) * TM)
        h = _norm_mod(x_ref[0, rows, :], g_ref[...],
                      _pick_mod(sh_ref, _is_ctx(k)), _pick_mod(sc_ref, _is_ctx(k))).astype(BF16)
        y = _dot(h, win_ref[...])
        cq = y[:, :C_Q_RANK]
        ckv = y[:, C_Q_RANK:C_Q_RANK + C_KV_RANK]
        kr = y[:, C_Q_RANK + C_KV_RANK:]
        q = _dot(rms(cq, qag_ref[...]), wuq_ref[...])
        kv = _dot(rms(ckv, kvg_ref[...]), wukv_ref[...])
        cos = cos_ref[rows, :]
        sin = sin_ref[rows, :]
        qn = _group_rms(q[:, :Q_NOPE_W], e64_ref[...], C_NOPE) * gq[:, :Q_NOPE_W]
        qn_ref[0, :, rows] = qn.T.astype(BF16)
        qr = _group_rms(q[:, Q_NOPE_W:], e32_ref[...], C_ROPE) * gq[:, Q_NOPE_W:]
        qr_ref[0, :, rows] = _rope(qr, cos, sin, C_ROPE // 4).T.astype(BF16)
        kn = _group_rms(kv[:, :Q_NOPE_W], e64_ref[...], C_NOPE) * gk[:, :Q_NOPE_W]
        kn_ref[0, rows, :] = kn.astype(BF16)
        krn = _group_rms(kr, e32_ref[...], C_ROPE) * gk[:, Q_NOPE_W:]
        kr_ref[0, rows, :] = _rope(krn, cos, sin, C_ROPE // 4).astype(BF16)
        vt_ref[0, :, rows] = kv[:, Q_NOPE_W:].T.astype(BF16)


def _odd_proj(xa, mods, g, win, qag, kvg, wuq, wukv, gq, gk, e64, e32, cos, sin):
    b, n, d = xa.shape
    tmb = NSUB * TM
    assert n % tmb == 0
    tok = lambda wd: pl.BlockSpec((1, tmb, wd), lambda bb, i: (bb, i, 0))
    return pl.pallas_call(
        _odd_proj_kernel,
        grid=(b, n // tmb),
        in_specs=[
            tok(d), _mod_spec(0), _mod_spec(1),
            _const_spec((1, d)),
            _const_spec((d, ODD_IN_COLS)),
            _const_spec((1, C_Q_RANK)), _const_spec((1, C_KV_RANK)),
            _const_spec((C_Q_RANK, Q_NOPE_W + Q_ROPE_W)),
            _const_spec((C_KV_RANK, 2 * Q_NOPE_W)),
            _const_spec((1, Q_NOPE_W + Q_ROPE_W)), _const_spec((1, Q_NOPE_W + LANES)),
            _const_spec((MXU_DIM, MXU_DIM)), _const_spec((MXU_DIM, MXU_DIM)),
            pl.BlockSpec((tmb, LANES), lambda bb, i: (i, 0)),
            pl.BlockSpec((tmb, LANES), lambda bb, i: (i, 0)),
        ],
        out_specs=[pl.BlockSpec((1, Q_NOPE_W, tmb), lambda bb, i: (bb, 0, i)),
                   pl.BlockSpec((1, Q_ROPE_W, tmb), lambda bb, i: (bb, 0, i)),
                   tok(Q_NOPE_W), tok(LANES),
                   pl.BlockSpec((1, Q_NOPE_W, tmb), lambda bb, i: (bb, 0, i))],
        out_shape=[jax.ShapeDtypeStruct((b, Q_NOPE_W, n), BF16),
                   jax.ShapeDtypeStruct((b, Q_ROPE_W, n), BF16),
                   jax.ShapeDtypeStruct((b, n, Q_NOPE_W), BF16),
                   jax.ShapeDtypeStruct((b, n, LANES), BF16),
                   jax.ShapeDtypeStruct((b, Q_NOPE_W, n), BF16)],
        compiler_params=_params(ARB2, _nbytes((tmb, d), F32) + 2 * _nbytes((tmb, LANES), F32)
                                + _nbytes((tmb, 3 * Q_NOPE_W + Q_ROPE_W + LANES), BF16)
                                + _nbytes((d + C_Q_RANK + C_KV_RANK, 2 * Q_NOPE_W), BF16, 1)),
        name="odd_proj",
    )(xa, mods, mods, g, win, qag, kvg, wuq, wukv, gq, gk, e64, e32, cos, sin)


SUM_ROWS = 16


def _kv_tile(n):
    for t in (768, 256):
        if n % t == 0 and (n // t) % 2 == 1:
            return t
    raise ValueError(f"joint sequence length {n} has no odd split into 256-multiples")


def _flash_kernel(qnt_ref, qrt_ref, kn_ref, kr_ref, vt_ref, o_ref,
                  s_ref, mx_ref, m_ref, acc_ref, qt_ref, *, tk, tq):
    p = pl.program_id(1)
    n = kn_ref.shape[1]
    nq = (n - CTX_LEN) // tq
    nch = n // tk
    assert nq % 2 == 0 and nch % 2 == 1
    ones = jnp.ones((SUM_ROWS, tk), BF16)
    row = lax.broadcasted_iota(jnp.int32, (LANES, tq), 0)

    def prep_q(qi, qslot):
        c0 = pl.multiple_of(CTX_LEN + qi * tq, tq)
        qn = qnt_ref[0, :, pl.ds(c0, tq)]
        qr = qrt_ref[0, :, pl.ds(c0, tq)]
        zero = jnp.zeros_like(qn)
        for hh in range(2):
            off = ((2 * p + hh) % 4) * C_ROPE
            keep_n = (row < C_NOPE) if hh == 0 else (row >= C_NOPE)
            keep_r = (row >= off) & (row < off + C_ROPE)
            qt_ref[qslot, 0:LANES, hh * tq:(hh + 1) * tq] = jnp.where(keep_n, qn, zero)
            qt_ref[qslot, LANES:, hh * tq:(hh + 1) * tq] = jnp.where(keep_r, qr, zero)

    def scores(qslot, j, sslot):
        k0 = j * tk
        kcat = jnp.concatenate([kn_ref[0, pl.ds(k0, tk), :], kr_ref[0, pl.ds(k0, tk), :]], axis=1)
        st = _dot(kcat, qt_ref[qslot])
        s_ref[sslot] = st
        mx_ref[sslot] = jnp.max(st, axis=0, keepdims=True)

    def update(j, sslot):
        k0 = j * tk
        m_old = m_ref[...]
        m_new = jnp.maximum(m_old, mx_ref[sslot])
        alpha = jnp.exp2(m_old - m_new)
        pt = jnp.exp2(s_ref[sslot] - m_new).astype(BF16)
        va = jnp.concatenate([vt_ref[0, :, pl.ds(k0, tk)], ones], axis=0)
        m_ref[...] = m_new
        acc_ref[...] = alpha * acc_ref[...] + _dot(va, pt)

    def reset():
        m_ref[...] = jnp.full(m_ref.shape, NEG_INF, F32)
        acc_ref[...] = jnp.zeros(acc_ref.shape, F32)

    def finish(qi):
        acc = acc_ref[...]
        c0 = pl.multiple_of(qi * tq, tq)
        o_ref[0, 0:C_NOPE, pl.ds(c0, tq)] = (acc[0:C_NOPE, :tq] / acc[LANES:LANES + 1, :tq]).astype(o_ref.dtype)
        o_ref[0, C_NOPE:, pl.ds(c0, tq)] = (acc[C_NOPE:LANES, tq:] / acc[LANES:LANES + 1, tq:]).astype(o_ref.dtype)
        reset()

    def run_tile(qi, slot, next_qi):
        other = 1 - slot
        for j in range(nch - 1):
            scores(slot, j + 1, other if j % 2 == 0 else slot)
            update(j, slot if j % 2 == 0 else other)
        prep_q(next_qi, other)
        scores(other, 0, other)
        update(nch - 1, slot)
        finish(qi)

    def two_tiles(t2, carry):
        run_tile(2 * t2, 0, 2 * t2 + 1)
        run_tile(2 * t2 + 1, 1, jnp.minimum(2 * t2 + 2, nq - 1))
        return carry

    reset()
    prep_q(0, 0)
    scores(0, 0, 0)
    lax.fori_loop(0, nq // 2, two_tiles, 0)


def _flash_attn(qnt, qrt, kn, kr, vt):
    b, n, _ = kn.shape
    s = n - CTX_LEN
    tq = TM
    tk = _kv_tile(n)
    return pl.pallas_call(
        functools.partial(_flash_kernel, tk=tk, tq=tq),
        grid=(b, C_HEADS // 2),
        scratch_shapes=[pltpu.VMEM((2, tk, 2 * tq), F32),
                        pltpu.VMEM((2, 1, 2 * tq), F32),
                        pltpu.VMEM((1, 2 * tq), F32),
                        pltpu.VMEM((LANES + SUM_ROWS, 2 * tq), F32),
                        pltpu.VMEM((2, 2 * LANES, 2 * tq), BF16)],
        in_specs=[
            pl.BlockSpec((1, LANES, n), lambda bb, p: (bb, p, 0)),
            pl.BlockSpec((1, LANES, n), lambda bb, p: (bb, p // 2, 0)),
            pl.BlockSpec((1, n, LANES), lambda bb, p: (bb, 0, p)),
            pl.BlockSpec((1, n, LANES), lambda bb, p: (bb, 0, 0)),
            pl.BlockSpec((1, LANES, n), lambda bb, p: (bb, p, 0)),
        ],
        out_specs=pl.BlockSpec((1, LANES, s), lambda bb, p: (bb, p, 0)),
        out_shape=jax.ShapeDtypeStruct((b, C_HEADS * C_NOPE, s), BF16),
        compiler_params=_params(ARB2, _nbytes((6 * LANES, n), BF16)
                                + _nbytes((tk + LANES + SUM_ROWS + 2 * LANES, 2 * tq), F32)),
        name="mla_flash",
    )(qnt, qrt, kn, kr, vt)


def _rope_tables(seq, dim):
    t = jnp.arange(seq, dtype=jnp.int32)
    pos = jnp.stack([t // GRID_W, t % GRID_W], axis=0).astype(F32)
    half = dim // 2
    q = half // 2
    inv = ROPE_THETA ** (-jnp.arange(q, dtype=F32) / q)
    j = jnp.arange(dim)
    ang = pos[j // half].T * inv[j % q][None, :]
    sign = jnp.where((j % half) < q, -1.0, 1.0).astype(F32)
    cos = jnp.cos(ang)
    sin = jnp.sin(ang) * sign[None, :]
    reps = LANES // dim
    cos = jnp.tile(cos, (1, reps))
    sin = jnp.tile(sin, (1, reps))
    cos = jnp.concatenate([jnp.ones((CTX_LEN, LANES), F32), cos], axis=0)
    sin = jnp.concatenate([jnp.zeros((CTX_LEN, LANES), F32), sin], axis=0)
    return cos, sin


def _group_ones(group):
    r = jnp.arange(MXU_DIM) // group
    return (r[:, None] == r[None, :]).astype(BF16)


def _na_bias_tables(rpb, mult):
    cq = jnp.arange(GRID_W)
    c0 = jnp.clip(cq - NA_COLS // 2, 0, GRID_W - NA_COLS)
    col_ok = (cq[:, None] >= c0[None, :]) & (cq[:, None] < c0[None, :] + NA_COLS)
    dci = jnp.clip(cq[:, None] - cq[None, :], 1 - NA_COLS, NA_COLS - 1) + NA_COLS - 1
    pick = (dci[None] == jnp.arange(2 * NA_COLS - 1)[:, None, None]).astype(F32)
    tt = jnp.einsum("hdm,mkq->hdkq", rpb.astype(F32) * mult, pick, precision=lax.Precision.HIGHEST)
    tt = jnp.where(col_ok[None, None], tt, NEG_INF)
    tt = jnp.concatenate([tt, jnp.full_like(tt[:, :1], NEG_INF)], axis=1)
    zero = jnp.zeros_like(tt)

    def per_group(t):
        t = t.reshape(B_HEADS // NA_HG, NA_HG, NA_BIAS_N, GRID_W, LANES)
        return t.transpose(0, 2, 1, 3, 4)

    return per_group(jnp.concatenate([tt, zero], axis=-1)), per_group(jnp.concatenate([zero, tt], axis=-1))


def kernel(x, c, ctx, c_ctx, ada_w, ada_b, norm_mix, norm_mlp, mlp_w1, mlp_w2, e_w_in, e_w_out, a_q_norm, a_k_norm, a_sink, b_q_norm, b_k_norm, b_rpb, o_w_in, o_qa_norm, o_kva_norm, o_w_uq, o_w_ukv, o_qn_nope, o_qn_rope, o_kn_nope, o_kn_rope, o_w_out):
    bsz, seq, d = x.shape
    assert d == D_MODEL and ctx.shape[1] == CTX_LEN and seq % TM == 0 and ada_w.shape[0] == 2
    assert bsz + 1 <= 8

    cond = jnp.zeros((8, d), F32).at[:bsz].set(c).at[bsz].set(c_ctx)
    m = _adaln(cond, ada_w, ada_b)
    mods = [jnp.stack([jnp.broadcast_to(m[i, bsz], (bsz, 6 * d)), m[i, :bsz]], axis=1).reshape(2 * bsz, 1, 6 * d)
            for i in range(2)]

    e64 = _group_ones(HEAD_DIM)
    e32 = _group_ones(C_ROPE)

    w_ext = e_w_in[0].astype(BF16)
    scale = HEAD_DIM ** -0.5
    log2e = math.log2(math.e)
    gains = jnp.concatenate([jnp.tile(a_q_norm[0], A_HEADS) * (scale * log2e), jnp.tile(a_k_norm[0], A_KV_HEADS),
                             jnp.tile(b_q_norm[0], B_HEADS) * (scale * log2e), jnp.tile(b_k_norm[0], B_HEADS)])[None, :]
    cos64, sin64 = _rope_tables(seq, HEAD_DIM)
    qat, ka, qbt, kb, vat, vbt = _even_proj(ctx, x, mods[0], norm_mix[0][None, :], w_ext, gains, e64, cos64, sin64)
    sink_row = (jnp.repeat(a_sink[0].reshape(A_KV_HEADS, GQA_R), A_BLOCK, axis=1) * log2e)[:, None, :]
    oa = _window_attn(qat, ka, vat, sink_row)
    ob = _na_attn(qbt, kb, vbt, *_na_bias_tables(b_rpb[0], log2e))
    wo = e_w_out[0].astype(BF16)
    xa = _even_out(ctx, x, oa, ob, mods[0], norm_mlp[0][None, :], wo[:512], wo[512:],
                   mlp_w1[0].astype(BF16), mlp_w2[0].astype(BF16))

    wi = o_w_in[0]
    win = jnp.concatenate([wi[:, :C_Q_RANK + C_KV_RANK]] + [wi[:, C_Q_RANK + C_KV_RANK:]] * 4, axis=1).astype(BF16)
    wuq = o_w_uq[0].reshape(C_Q_RANK, C_HEADS, C_NOPE + C_ROPE)
    wuq = jnp.concatenate([wuq[:, :, :C_NOPE].reshape(C_Q_RANK, -1), wuq[:, :, C_NOPE:].reshape(C_Q_RANK, -1)],
                          axis=1).astype(BF16)
    wukv = o_w_ukv[0].reshape(C_KV_RANK, C_HEADS, 2 * C_NOPE)
    wukv = jnp.concatenate([wukv[:, :, :C_NOPE].reshape(C_KV_RANK, -1), wukv[:, :, C_NOPE:].reshape(C_KV_RANK, -1)],
                           axis=1).astype(BF16)
    qscale = (C_NOPE + C_ROPE) ** -0.5 * math.log2(math.e)
    gq = (jnp.concatenate([jnp.tile(o_qn_nope[0], C_HEADS), jnp.tile(o_qn_rope[0], C_HEADS)]) * qscale)[None, :]
    gk = jnp.concatenate([jnp.tile(o_kn_nope[0], C_HEADS), jnp.tile(o_kn_rope[0], LANES // C_ROPE)])[None, :]
    cos32, sin32 = _rope_tables(seq, C_ROPE)
    qn, qr, kn, kr, vt = _odd_proj(xa, mods[1], norm_mix[1][None, :], win, o_qa_norm[0][None, :],
                                   o_kva_norm[0][None, :], wuq, wukv, gq, gk, e64, e32, cos32, sin32)
    ot = _flash_attn(qn, qr, kn, kr, vt)
    return _odd_out(xa, ot, mods[1], norm_mlp[1][None, :], o_w_out[0].astype(BF16),
                    mlp_w1[1].astype(BF16), mlp_w2[1].astype(BF16))
```

```python
import functools
import math

import jax
import jax.numpy as jnp
from jax import lax
from jax.experimental import pallas as pl
from jax.experimental.pallas import tpu as pltpu

F32 = jnp.float32
BF16 = jnp.bfloat16

D_MODEL = 1024
CTX_LEN = 256
GRID_W = 64
HEAD_DIM = 64
A_HEADS = 8
A_KV_HEADS = 2
A_WINDOW = 128
A_BLOCK = 128
B_HEADS = 8
NA_ROWS = 8
NA_COLS = 16
C_HEADS = 16
C_Q_RANK = 384
C_KV_RANK = 256
C_NOPE = 64
C_ROPE = 32
D_FF = 4 * D_MODEL
ROPE_THETA = 10000.0
NORM_EPS = 1e-6
NEG_INF = -1e30

LANES = 128
MXU_DIM = 256
TM = CTX_LEN
V7X_VMEM_BYTES = 64 * 1024 * 1024
VMEM_VALUE_BYTES = 32 * 1024 * 1024


ARB2 = ("arbitrary", "arbitrary")


def _dot(a, b):
    return jnp.dot(a, b, preferred_element_type=F32)


def _nbytes(shape, dtype, buffers=2):
    return math.prod(shape) * jnp.dtype(dtype).itemsize * buffers


def _params(sem, declared_bytes):
    limit = min(declared_bytes + VMEM_VALUE_BYTES, V7X_VMEM_BYTES * 7 // 8)
    return pltpu.CompilerParams(dimension_semantics=sem, vmem_limit_bytes=limit)


def _norm_mod(x, g, sh, sc):
    ms = jnp.mean(x * x, axis=-1, keepdims=True)
    return (x * lax.rsqrt(ms + NORM_EPS) * g) * (1.0 + sc) + sh


def _group_rms(y, e, group):
    out = []
    for c0 in range(0, y.shape[1], MXU_DIM):
        cw = min(MXU_DIM, y.shape[1] - c0)
        yc = y[:, c0:c0 + cw]
        ss = _dot((yc * yc).astype(BF16), e[:cw, :cw])
        out.append(yc * lax.rsqrt(ss * (1.0 / group) + NORM_EPS))
    return out[0] if len(out) == 1 else jnp.concatenate(out, axis=1)


def _rope(r, cos, sin, half):
    w = r.shape[1]
    reps = w // LANES
    if reps > 1:
        cos = jnp.concatenate([cos] * reps, axis=1)
        sin = jnp.concatenate([sin] * reps, axis=1)
    lane = lax.broadcasted_iota(jnp.int32, r.shape, 1)
    up = pltpu.roll(r, w - half, axis=1)
    dn = pltpu.roll(r, half, axis=1)
    sw = jnp.where((lane & half) == 0, up, dn)
    return r * cos + sw * sin


def _ada_kernel(cond_ref, w_ref, b_ref, o_ref):
    c = cond_ref[...]
    s = (c * jax.nn.sigmoid(c)).astype(BF16)
    o_ref[0] = _dot(s, w_ref[0].astype(BF16)) + b_ref[0]


def _adaln(cond, ada_w, ada_b):
    depth, d, n6 = ada_w.shape
    tn = 1536
    return pl.pallas_call(
        _ada_kernel,
        grid=(depth, n6 // tn),
        in_specs=[
            pl.BlockSpec((8, d), lambda l, j: (0, 0)),
            pl.BlockSpec((1, d, tn), lambda l, j: (l, 0, j)),
            pl.BlockSpec((1, 1, tn), lambda l, j: (l, 0, j)),
        ],
        out_specs=pl.BlockSpec((1, 8, tn), lambda l, j: (l, 0, j)),
        out_shape=jax.ShapeDtypeStruct((depth, 8, n6), F32),
        compiler_params=_params(ARB2, _nbytes((d + 2 * 8, tn), F32)),
        name="adaln",
    )(cond, ada_w, ada_b.reshape(depth, 1, n6))


NSUB = 3


def _mod_spec(chunk, lat_only=False):
    if lat_only:
        return pl.BlockSpec((1, 1, D_MODEL), lambda b, i: (2 * b + 1, 0, chunk))
    return pl.BlockSpec((2, 1, D_MODEL), lambda b, i: (b, 0, chunk))


def _pick_mod(ref, is_ctx):
    m = ref[...]
    return jnp.where(is_ctx, m[0], m[1])


def _is_ctx(k):
    return NSUB * pl.program_id(1) + k == 0


def _const_spec(shape):
    nd = len(shape)
    return pl.BlockSpec(shape, lambda *_: (0,) * nd, pipeline_mode=pl.Buffered(1))


QA_W = A_HEADS * HEAD_DIM
KVA_W = A_KV_HEADS * HEAD_DIM
QKVB_W = B_HEADS * HEAD_DIM
EVEN_ROPED = QA_W + KVA_W
EVEN_QB = EVEN_ROPED + KVA_W
EVEN_VB = EVEN_QB + 2 * QKVB_W
EVEN_COLS = EVEN_VB + QKVB_W
EVEN_NORMED = EVEN_ROPED + 2 * QKVB_W


def _joint_specs(d):
    lat = lambda k: pl.BlockSpec((1, TM, d), lambda bb, i: (bb, jnp.maximum(NSUB * i + k - 1, 0), 0))
    return [pl.BlockSpec((1, CTX_LEN, d), lambda bb, i: (bb, 0, 0))] + [lat(k) for k in range(NSUB)]


def _joint_tile(ctx_ref, x_refs, k):
    return jnp.where(_is_ctx(k), ctx_ref[0], x_refs[k][0])


def _even_proj_kernel(ctx_ref, *refs):
    x_refs = refs[:NSUB]
    (sh_ref, sc_ref, g_ref, w_ref, gains_ref, e_ref, cos_ref, sin_ref,
     qat_ref, ka_ref, qb_ref, kb_ref, vat_ref, vb_ref) = refs[NSUB:]
    for k in range(NSUB):
        rows = slice(k * TM, (k + 1) * TM)
        h = _norm_mod(_joint_tile(ctx_ref, x_refs, k), g_ref[...],
                      _pick_mod(sh_ref, _is_ctx(k)), _pick_mod(sc_ref, _is_ctx(k))).astype(BF16)
        y = _dot(h, w_ref[...])
        gains = gains_ref[...]
        ya = _group_rms(y[:, :EVEN_ROPED], e_ref[...], HEAD_DIM) * gains[:, :EVEN_ROPED]
        r = _rope(ya, cos_ref[rows, :], sin_ref[rows, :], HEAD_DIM // 4)
        yb = _group_rms(y[:, EVEN_QB:EVEN_VB], e_ref[...], HEAD_DIM) * gains[:, EVEN_ROPED:]
        qat_ref[0, :, rows] = r[:, 0:QA_W].T.astype(BF16)
        ka_ref[0, rows, :] = r[:, QA_W:].astype(BF16)
        qb_ref[0, :, rows] = yb[:, :QKVB_W].T.astype(BF16)
        kb_ref[0, rows, :] = yb[:, QKVB_W:].astype(BF16)
        vat_ref[0, :, rows] = y[:, EVEN_ROPED:EVEN_QB].T.astype(BF16)
        vb_ref[0, :, rows] = y[:, EVEN_VB:].T.astype(BF16)


def _even_proj(ctx, x, mods, g, w, gains, e64, cos, sin):
    b, s, d = x.shape
    n = CTX_LEN + s
    tmb = NSUB * TM
    assert n % tmb == 0
    tok = lambda wd: (pl.BlockSpec((1, tmb, wd), lambda bb, i: (bb, i, 0)),
                      jax.ShapeDtypeStruct((b, n, wd), BF16))
    tr = lambda wd: (pl.BlockSpec((1, wd, tmb), lambda bb, i: (bb, 0, i)),
                     jax.ShapeDtypeStruct((b, wd, n), BF16))
    outs = (tr(QA_W), tok(KVA_W), tr(QKVB_W), tok(QKVB_W), tr(KVA_W), tr(QKVB_W))
    return pl.pallas_call(
        _even_proj_kernel,
        grid=(b, n // tmb),
        in_specs=_joint_specs(d) + [
            _mod_spec(0), _mod_spec(1),
            _const_spec((1, d)),
            _const_spec((d, EVEN_COLS)),
            _const_spec((1, EVEN_NORMED)),
            _const_spec((MXU_DIM, MXU_DIM)),
            pl.BlockSpec((tmb, LANES), lambda bb, i: (i, 0)),
            pl.BlockSpec((tmb, LANES), lambda bb, i: (i, 0)),
        ],
        out_specs=[o[0] for o in outs],
        out_shape=[o[1] for o in outs],
        compiler_params=_params(ARB2, _nbytes((CTX_LEN + tmb, d), F32) + _nbytes((d, EVEN_COLS), BF16, 1)
                                + _nbytes((tmb, EVEN_COLS), BF16) + 2 * _nbytes((tmb, LANES), F32)),
        name="even_proj",
    )(ctx, *([x] * NSUB), mods, mods, g, w, gains, e64, cos, sin)


GQA_R = A_HEADS // A_KV_HEADS
WIN_BAND = 3 * A_BLOCK
WIN_LANES = GQA_R * A_BLOCK


PIPE_UNROLL = 2


def _window_kernel(qt_ref, k_ref, vt_ref, sink_ref, wm_ref, o_ref, s_ref, mx_ref, qs_ref, *, seq):
    g = pl.program_id(1)
    nb = seq // A_BLOCK
    nkeys = WIN_BAND + CTX_LEN
    assert nb % PIPE_UNROLL == 0 and PIPE_UNROLL % 2 == 0
    sink = sink_ref[0]
    ones_loc = jnp.ones((SUM_ROWS, nkeys), BF16)
    ones_ctx = jnp.ones((SUM_ROWS, CTX_LEN), BF16)

    def prep_q(col0, qslot):
        qt = qt_ref[0, :, pl.ds(col0, A_BLOCK)]
        zero = jnp.zeros((HEAD_DIM, A_BLOCK), BF16)
        for h in range(GQA_R):
            qh = qt[h * HEAD_DIM:(h + 1) * HEAD_DIM]
            lanes = slice(h * A_BLOCK, (h + 1) * A_BLOCK)
            qs_ref[qslot, 0:HEAD_DIM, lanes] = jnp.where(g == 0, qh, zero)
            qs_ref[qslot, HEAD_DIM:, lanes] = jnp.where(g == 0, zero, qh)

    def band_start(n):
        return pl.multiple_of(jnp.clip((n - 1) * A_BLOCK, 0, seq - WIN_BAND), A_BLOCK)

    def scores(qslot, n, sslot):
        start = band_start(n)
        keys = jnp.concatenate([k_ref[0, pl.ds(CTX_LEN + start, WIN_BAND), :], k_ref[0, 0:CTX_LEN, :]], axis=0)
        st = _dot(keys, qs_ref[qslot])
        mask = wm_ref[n - start // A_BLOCK]
        s_loc = st[:WIN_BAND] + jnp.concatenate([mask] * GQA_R, axis=1)
        s_ctx = st[WIN_BAND:]
        s_ref[sslot, 0:WIN_BAND] = s_loc
        s_ref[sslot, WIN_BAND:] = s_ctx
        mx_ref[sslot] = jnp.maximum(jnp.maximum(jnp.max(s_loc, axis=0, keepdims=True),
                                                jnp.max(s_ctx, axis=0, keepdims=True)), sink)

    def write(res, m, col0):
        o = res[0:HEAD_DIM] / (res[HEAD_DIM:HEAD_DIM + 1] + jnp.exp2(sink - m))
        for h in range(GQA_R):
            o_ref[0, h * HEAD_DIM:(h + 1) * HEAD_DIM, pl.ds(col0, A_BLOCK)] = (
                o[:, h * A_BLOCK:(h + 1) * A_BLOCK].astype(o_ref.dtype))

    def update(n, sslot):
        start = band_start(n)
        m = mx_ref[sslot]
        pt = jnp.exp2(s_ref[sslot] - m).astype(BF16)
        va = jnp.concatenate([vt_ref[0, :, pl.ds(CTX_LEN + start, WIN_BAND)], vt_ref[0, :, 0:CTX_LEN]], axis=1)
        res = _dot(jnp.concatenate([va, ones_loc], axis=0), pt)
        write(res, m, pl.multiple_of(CTX_LEN + n * A_BLOCK, A_BLOCK))

    for blk in range(CTX_LEN // A_BLOCK):
        prep_q(blk * A_BLOCK, 0)
        st = _dot(k_ref[0, 0:CTX_LEN, :], qs_ref[0])
        m = jnp.maximum(jnp.max(st, axis=0, keepdims=True), sink)
        pt = jnp.exp2(st - m).astype(BF16)
        res = _dot(jnp.concatenate([vt_ref[0, :, 0:CTX_LEN], ones_ctx], axis=0), pt)
        write(res, m, blk * A_BLOCK)

    def some_blocks(t, carry):
        for u in range(PIPE_UNROLL):
            nxt = jnp.minimum(PIPE_UNROLL * t + u + 1, nb - 1)
            prep_q(pl.multiple_of(CTX_LEN + nxt * A_BLOCK, A_BLOCK), (u + 1) % 2)
            scores((u + 1) % 2, nxt, (u + 1) % 2)
            update(PIPE_UNROLL * t + u, u % 2)
        return carry

    prep_q(CTX_LEN, 0)
    scores(0, 0, 0)
    lax.fori_loop(0, nb // PIPE_UNROLL, some_blocks, 0)


def _window_mask():
    o = jnp.arange(3)[:, None, None]
    key = jnp.arange(WIN_BAND)[None, :, None]
    qry = jnp.arange(A_BLOCK)[None, None, :]
    return jnp.where(jnp.abs(qry + o * A_BLOCK - key) <= A_WINDOW, 0.0, NEG_INF).astype(F32)


def _window_attn(qat, ka, vat, sink_row):
    b, _, n = qat.shape
    return pl.pallas_call(
        functools.partial(_window_kernel, seq=n - CTX_LEN),
        grid=(b, A_KV_HEADS),
        scratch_shapes=[pltpu.VMEM((2, WIN_BAND + CTX_LEN, WIN_LANES), F32),
                        pltpu.VMEM((2, 1, WIN_LANES), F32),
                        pltpu.VMEM((2, 2 * HEAD_DIM, WIN_LANES), BF16)],
        in_specs=[
            pl.BlockSpec((1, GQA_R * HEAD_DIM, n), lambda bb, g: (bb, g, 0)),
            pl.BlockSpec((1, n, KVA_W), lambda bb, g: (bb, 0, 0)),
            pl.BlockSpec((1, HEAD_DIM, n), lambda bb, g: (bb, g, 0)),
            pl.BlockSpec((1, 1, WIN_LANES), lambda bb, g: (g, 0, 0)),
            _const_spec((3, WIN_BAND, A_BLOCK)),
        ],
        out_specs=pl.BlockSpec((1, GQA_R * HEAD_DIM, n), lambda bb, g: (bb, g, 0)),
        out_shape=jax.ShapeDtypeStruct((b, QA_W, n), BF16),
        compiler_params=_params(ARB2, _nbytes((2 * GQA_R * HEAD_DIM + KVA_W + HEAD_DIM, n), BF16)
                                + _nbytes((WIN_BAND + CTX_LEN + 2 * HEAD_DIM, WIN_LANES), F32)
                                + _nbytes((3, WIN_BAND, A_BLOCK), F32, 1)),
        name="window_attn",
    )(qat, ka, vat, sink_row, _window_mask())


NA_ITEM_ROWS = LANES // GRID_W
NA_WIN_ROWS = NA_ROWS + NA_ITEM_ROWS
NA_HG = 4
NA_W = NA_HG * HEAD_DIM
NA_LANES = NA_HG * LANES
NA_BIAS_N = 2 * NA_ROWS


def _na_kernel(qt_ref, k_ref, vt_ref, ta_ref, tb_ref, o_ref, s_ref, mx_ref, qs_ref, *, rows_n):
    nitems = rows_n // NA_ITEM_ROWS
    nloc = NA_WIN_ROWS * GRID_W
    assert nitems % PIPE_UNROLL == 0 and (rows_n - NA_WIN_ROWS) % 2 == 0 and nloc % LANES == 0
    ones_loc = jnp.ones((SUM_ROWS, nloc + CTX_LEN), BF16)
    ones_ctx = jnp.ones((SUM_ROWS, CTX_LEN), BF16)

    def prep_q(col0, qslot):
        qt = qt_ref[0, :, pl.ds(col0, LANES)]
        for h in range(NA_HG):
            rows = slice(h * HEAD_DIM, (h + 1) * HEAD_DIM)
            qs_ref[qslot, rows, h * LANES:(h + 1) * LANES] = qt[rows]

    def win_start(j):
        return jnp.clip(NA_ITEM_ROWS * j - NA_ROWS // 2, 0, rows_n - NA_WIN_ROWS)

    def scores(qslot, j, sslot):
        ru = win_start(j)
        k0 = pl.multiple_of(CTX_LEN + ru * GRID_W, LANES)
        keys = jnp.concatenate([k_ref[0, pl.ds(k0, nloc), :], k_ref[0, 0:CTX_LEN, :]], axis=0)
        st = _dot(keys, qs_ref[qslot])
        mx = jnp.max(st[nloc:], axis=0, keepdims=True)
        s_ref[sslot, nloc:] = st[nloc:]
        for i in range(NA_WIN_ROWS):
            idx = []
            for e in range(NA_ITEM_ROWS):
                r = NA_ITEM_ROWS * j + e
                r0 = jnp.clip(r - NA_ROWS // 2, 0, rows_n - NA_ROWS)
                seen = (ru + i >= r0) & (ru + i < r0 + NA_ROWS)
                idx.append(jnp.where(seen, ru + i - r + NA_ROWS - 1, NA_BIAS_N - 1))
            rows = slice(i * GRID_W, (i + 1) * GRID_W)
            bias = ta_ref[0, idx[0]] + tb_ref[0, idx[1]]
            blk = st[rows] + jnp.concatenate([bias[h] for h in range(NA_HG)], axis=1)
            s_ref[sslot, rows] = blk
            mx = jnp.maximum(mx, jnp.max(blk, axis=0, keepdims=True))
        mx_ref[sslot] = mx

    def pv_write(va, ones, pt, col0):
        for pair in range(NA_HG // 2):
            rows = slice(pair * LANES, (pair + 1) * LANES)
            res = _dot(jnp.concatenate([va[rows], ones], axis=0),
                       pt[:, pair * 2 * LANES:(pair + 1) * 2 * LANES])
            for hh in range(2):
                lanes = slice(hh * LANES, (hh + 1) * LANES)
                o = res[hh * HEAD_DIM:(hh + 1) * HEAD_DIM, lanes] / res[LANES:LANES + 1, lanes]
                h = 2 * pair + hh
                o_ref[0, h * HEAD_DIM:(h + 1) * HEAD_DIM, pl.ds(col0, LANES)] = o.astype(o_ref.dtype)

    def update(j, sslot):
        k0 = pl.multiple_of(CTX_LEN + win_start(j) * GRID_W, LANES)
        pt = jnp.exp2(s_ref[sslot] - mx_ref[sslot]).astype(BF16)
        va = jnp.concatenate([vt_ref[0, :, pl.ds(k0, nloc)], vt_ref[0, :, 0:CTX_LEN]], axis=1)
        pv_write(va, ones_loc, pt, pl.multiple_of(CTX_LEN + j * LANES, LANES))

    qs_ref[...] = jnp.zeros(qs_ref.shape, BF16)
    for blk in range(CTX_LEN // LANES):
        prep_q(blk * LANES, 0)
        st = _dot(k_ref[0, 0:CTX_LEN, :], qs_ref[0])
        pt = jnp.exp2(st - jnp.max(st, axis=0, keepdims=True)).astype(BF16)
        pv_write(vt_ref[0, :, 0:CTX_LEN], ones_ctx, pt, blk * LANES)

    def some_items(t, carry):
        for u in range(PIPE_UNROLL):
            nxt = jnp.minimum(PIPE_UNROLL * t + u + 1, nitems - 1)
            prep_q(pl.multiple_of(CTX_LEN + nxt * LANES, LANES), (u + 1) % 2)
            scores((u + 1) % 2, nxt, (u + 1) % 2)
            update(PIPE_UNROLL * t + u, u % 2)
        return carry

    prep_q(CTX_LEN, 0)
    scores(0, 0, 0)
    lax.fori_loop(0, nitems // PIPE_UNROLL, some_items, 0)


def _na_attn(qbt, kb, vbt, ta, tb):
    b, _, n = qbt.shape
    rows_n = (n - CTX_LEN) // GRID_W
    assert rows_n >= NA_WIN_ROWS
    nkeys = NA_WIN_ROWS * GRID_W + CTX_LEN
    tspec = pl.BlockSpec((1, NA_BIAS_N, NA_HG, GRID_W, LANES), lambda bb, hg: (hg, 0, 0, 0, 0))
    return pl.pallas_call(
        functools.partial(_na_kernel, rows_n=rows_n),
        grid=(b, B_HEADS // NA_HG),
        scratch_shapes=[pltpu.VMEM((2, nkeys, NA_LANES), F32),
                        pltpu.VMEM((2, 1, NA_LANES), F32),
                        pltpu.VMEM((2, NA_W, NA_LANES), BF16)],
        in_specs=[
            pl.BlockSpec((1, NA_W, n), lambda bb, hg: (bb, hg, 0)),
            pl.BlockSpec((1, n, NA_W), lambda bb, hg: (bb, 0, hg)),
            pl.BlockSpec((1, NA_W, n), lambda bb, hg: (bb, hg, 0)),
            tspec, tspec,
        ],
        out_specs=pl.BlockSpec((1, NA_W, n), lambda bb, hg: (bb, hg, 0)),
        out_shape=jax.ShapeDtypeStruct((b, QKVB_W, n), BF16),
        compiler_params=_params(ARB2, _nbytes((4 * NA_W, n), BF16) + _nbytes((nkeys + NA_W, NA_LANES), F32)
                                + 2 * _nbytes((NA_BIAS_N, NA_HG, GRID_W, LANES), F32)),
        name="na_attn",
    )(qbt, kb, vbt, ta, tb)


def _mlp_tail(xs, y, mods, gn, w1_ref, w2_ref):
    x1s = [x + m[0] * y[k * TM:(k + 1) * TM] for k, (x, m) in enumerate(zip(xs, mods))]
    h = jnp.concatenate([_norm_mod(x1, gn, m[1], m[2]).astype(BF16) for x1, m in zip(x1s, mods)], axis=0)
    mlp = None
    for j in range(D_FF // D_MODEL):
        cols = slice(j * D_MODEL, (j + 1) * D_MODEL)
        u = jnp.maximum(_dot(h, w1_ref[:, cols]), 0.0)
        part = _dot((u * u).astype(BF16), w2_ref[cols, :])
        mlp = part if mlp is None else mlp + part
    return [x1 + m[3] * mlp[k * TM:(k + 1) * TM] for k, (x1, m) in enumerate(zip(x1s, mods))]


def _even_out_kernel(ctx_ref, *refs):
    x_refs = refs[:NSUB]
    (oat_ref, obt_ref, g1_ref, sh2_ref, sc2_ref, g2_ref, gn_ref,
     woa_ref, wob_ref, w1_ref, w2_ref, o_ref) = refs[NSUB:]
    oa = oat_ref[0].astype(F32).T.astype(BF16)
    ob = obt_ref[0].astype(F32).T.astype(BF16)
    y = _dot(oa, woa_ref[...]) + _dot(ob, wob_ref[...])
    xs = [_joint_tile(ctx_ref, x_refs, k) for k in range(NSUB)]
    mods = [[_pick_mod(r, _is_ctx(k)) for r in (g1_ref, sh2_ref, sc2_ref, g2_ref)] for k in range(NSUB)]
    for k, out in enumerate(_mlp_tail(xs, y, mods, gn_ref[...], w1_ref, w2_ref)):
        o_ref[0, k * TM:(k + 1) * TM, :] = out


def _even_out(ctx, x, oa, ob, mods, gn, woa, wob, w1, w2):
    b, s, d = x.shape
    n = CTX_LEN + s
    tmb = NSUB * TM
    return pl.pallas_call(
        _even_out_kernel,
        grid=(b, n // tmb),
        in_specs=_joint_specs(d) + [
            pl.BlockSpec((1, QA_W, tmb), lambda bb, i: (bb, 0, i)),
            pl.BlockSpec((1, QKVB_W, tmb), lambda bb, i: (bb, 0, i)),
            _mod_spec(2), _mod_spec(3), _mod_spec(4), _mod_spec(5),
            _const_spec((1, d)),
            _const_spec((QA_W, d)), _const_spec((QKVB_W, d)),
            _const_spec((d, D_FF)), _const_spec((D_FF, d)),
        ],
        out_specs=pl.BlockSpec((1, tmb, d), lambda bb, i: (bb, i, 0)),
        out_shape=jax.ShapeDtypeStruct((b, n, d), F32),
        compiler_params=_params(ARB2, _nbytes((CTX_LEN + 2 * tmb, d), F32) + _nbytes((QA_W + QKVB_W, tmb), BF16)
                                + _nbytes((d + 2 * D_FF, d), BF16, 1)),
        name="even_out_mlp",
    )(ctx, *([x] * NSUB), oa, ob, mods, mods, mods, mods, gn, woa, wob, w1, w2)


NSUB_ODD = 2


def _odd_out_kernel(*refs):
    x_refs = refs[:NSUB_ODD]
    ot_ref, g1_ref, sh2_ref, sc2_ref, g2_ref, gn_ref, wo_ref, w1_ref, w2_ref, o_ref = refs[NSUB_ODD:]
    o = ot_ref[0].astype(F32).T.astype(BF16)
    y = _dot(o, wo_ref[...])
    mods = [[g1_ref[0], sh2_ref[0], sc2_ref[0], g2_ref[0]]] * NSUB_ODD
    for k, out in enumerate(_mlp_tail([r[0] for r in x_refs], y, mods, gn_ref[...], w1_ref, w2_ref)):
        o_ref[0, k * TM:(k + 1) * TM, :] = out


def _odd_out(xa, ot, mods, gn, wo, w1, w2):
    b, n, d = xa.shape
    s = n - CTX_LEN
    tmb = NSUB_ODD * TM
    assert s % tmb == 0
    lat = lambda k: pl.BlockSpec((1, TM, d), lambda bb, i: (bb, NSUB_ODD * i + k + 1, 0))
    return pl.pallas_call(
        _odd_out_kernel,
        grid=(b, s // tmb),
        in_specs=[lat(k) for k in range(NSUB_ODD)] + [
            pl.BlockSpec((1, d, tmb), lambda bb, i: (bb, 0, i)),
            _mod_spec(2, True), _mod_spec(3, True), _mod_spec(4, True), _mod_spec(5, True),
            _const_spec((1, d)),
            _const_spec((d, d)),
            _const_spec((d, D_FF)), _const_spec((D_FF, d)),
        ],
        out_specs=pl.BlockSpec((1, tmb, d), lambda bb, i: (bb, i, 0)),
        out_shape=jax.ShapeDtypeStruct((b, s, d), F32),
        compiler_params=_params(ARB2, _nbytes((2 * tmb, d), F32) + _nbytes((d, tmb), BF16)
                                + _nbytes((d + 2 * D_FF, d), BF16, 1)),
        name="odd_out_mlp",
    )(*([xa] * NSUB_ODD), ot, mods, mods, mods, mods, gn, wo, w1, w2)


ODD_IN_COLS = C_Q_RANK + C_KV_RANK + LANES
Q_NOPE_W = C_HEADS * C_NOPE
Q_ROPE_W = C_HEADS * C_ROPE


def _odd_proj_kernel(x_ref, sh_ref, sc_ref, g_ref, win_ref, qag_ref, kvg_ref, wuq_ref, wukv_ref,
                     gq_ref, gk_ref, e64_ref, e32_ref, cos_ref, sin_ref,
                     qn_ref, qr_ref, kn_ref, kr_ref, vt_ref):
    def rms(t, g):
        return (t * lax.rsqrt(jnp.mean(t * t, axis=-1, keepdims=True) + NORM_EPS) * g).astype(BF16)

    gq = gq_ref[...]
    gk = gk_ref[...]
    for k in range(NSUB):
        rows = slice(k * TM, (k + 1) * TM)
        h = _norm_mod(x_ref[0, rows, :], g_ref[...],
                      _pick_mod(sh_ref, _is_ctx(k)), _pick_mod(sc_ref, _is_ctx(k))).astype(BF16)
        y = _dot(h, win_ref[...])
        cq = y[:, :C_Q_RANK]
        ckv = y[:, C_Q_RANK:C_Q_RANK + C_KV_RANK]
        kr = y[:, C_Q_RANK + C_KV_RANK:]
        q = _dot(rms(cq, qag_ref[...]), wuq_ref[...])
        kv = _dot(rms(ckv, kvg_ref[...]), wukv_ref[...])
        cos = cos_ref[rows, :]
        sin = sin_ref[rows, :]
        qn = _group_rms(q[:, :Q_NOPE_W], e64_ref[...], C_NOPE) * gq[:, :Q_NOPE_W]
        qn_ref[0, :, rows] = qn.T.astype(BF16)
        qr = _group_rms(q[:, Q_NOPE_W:], e32_ref[...], C_ROPE) * gq[:, Q_NOPE_W:]
        qr_ref[0, :, rows] = _rope(qr, cos, sin, C_ROPE // 4).T.astype(BF16)
        kn = _group_rms(kv[:, :Q_NOPE_W], e64_ref[...], C_NOPE) * gk[:, :Q_NOPE_W]
        kn_ref[0, rows, :] = kn.astype(BF16)
        krn = _group_rms(kr, e32_ref[...], C_ROPE) * gk[:, Q_NOPE_W:]
        kr_ref[0, rows, :] = _rope(krn, cos, sin, C_ROPE // 4).astype(BF16)
        vt_ref[0, :, rows] = kv[:, Q_NOPE_W:].T.astype(BF16)


def _odd_proj(xa, mods, g, win, qag, kvg, wuq, wukv, gq, gk, e64, e32, cos, sin):
    b, n, d = xa.shape
    tmb = NSUB * TM
    assert n % tmb == 0
    tok = lambda wd: pl.BlockSpec((1, tmb, wd), lambda bb, i: (bb, i, 0))
    return pl.pallas_call(
        _odd_proj_kernel,
        grid=(b, n // tmb),
        in_specs=[
            tok(d), _mod_spec(0), _mod_spec(1),
            _const_spec((1, d)),
            _const_spec((d, ODD_IN_COLS)),
            _const_spec((1, C_Q_RANK)), _const_spec((1, C_KV_RANK)),
            _const_spec((C_Q_RANK, Q_NOPE_W + Q_ROPE_W)),
            _const_spec((C_KV_RANK, 2 * Q_NOPE_W)),
            _const_spec((1, Q_NOPE_W + Q_ROPE_W)), _const_spec((1, Q_NOPE_W + LANES)),
            _const_spec((MXU_DIM, MXU_DIM)), _const_spec((MXU_DIM, MXU_DIM)),
            pl.BlockSpec((tmb, LANES), lambda bb, i: (i, 0)),
            pl.BlockSpec((tmb, LANES), lambda bb, i: (i, 0)),
        ],
        out_specs=[pl.BlockSpec((1, Q_NOPE_W, tmb), lambda bb, i: (bb, 0, i)),
                   pl.BlockSpec((1, Q_ROPE_W, tmb), lambda bb, i: (bb, 0, i)),
                   tok(Q_NOPE_W), tok(LANES),
                   pl.BlockSpec((1, Q_NOPE_W, tmb), lambda bb, i: (bb, 0, i))],
        out_shape=[jax.ShapeDtypeStruct((b, Q_NOPE_W, n), BF16),
                   jax.ShapeDtypeStruct((b, Q_ROPE_W, n), BF16),
                   jax.ShapeDtypeStruct((b, n, Q_NOPE_W), BF16),
                   jax.ShapeDtypeStruct((b, n, LANES), BF16),
                   jax.ShapeDtypeStruct((b, Q_NOPE_W, n), BF16)],
        compiler_params=_params(ARB2, _nbytes((tmb, d), F32) + 2 * _nbytes((tmb, LANES), F32)
                                + _nbytes((tmb, 3 * Q_NOPE_W + Q_ROPE_W + LANES), BF16)
                                + _nbytes((d + C_Q_RANK + C_KV_RANK, 2 * Q_NOPE_W), BF16, 1)),
        name="odd_proj",
    )(xa, mods, mods, g, win, qag, kvg, wuq, wukv, gq, gk, e64, e32, cos, sin)


SUM_ROWS = 16


def _kv_tile(n):
    for t in (768, 256):
        if n % t == 0 and (n // t) % 2 == 1:
            return t
    raise ValueError(f"joint sequence length {n} has no odd split into 256-multiples")


def _flash_kernel(qnt_ref, qrt_ref, kn_ref, kr_ref, vt_ref, o_ref,
                  s_ref, mx_ref, m_ref, acc_ref, qt_ref, *, tk, tq):
    p = pl.program_id(1)
    n = kn_ref.shape[1]
    nq = (n - CTX_LEN) // tq
    nch = n // tk
    assert nq % 2 == 0 and nch % 2 == 1
    ones = jnp.ones((SUM_ROWS, tk), BF16)
    row = lax.broadcasted_iota(jnp.int32, (LANES, tq), 0)

    def prep_q(qi, qslot):
        c0 = pl.multiple_of(CTX_LEN + qi * tq, tq)
        qn = qnt_ref[0, :, pl.ds(c0, tq)]
        qr = qrt_ref[0, :, pl.ds(c0, tq)]
        zero = jnp.zeros_like(qn)
        for hh in range(2):
            off = ((2 * p + hh) % 4) * C_ROPE
            keep_n = (row < C_NOPE) if hh == 0 else (row >= C_NOPE)
            keep_r = (row >= off) & (row < off + C_ROPE)
            qt_ref[qslot, 0:LANES, hh * tq:(hh + 1) * tq] = jnp.where(keep_n, qn, zero)
            qt_ref[qslot, LANES:, hh * tq:(hh + 1) * tq] = jnp.where(keep_r, qr, zero)

    def scores(qslot, j, sslot):
        k0 = j * tk
        kcat = jnp.concatenate([kn_ref[0, pl.ds(k0, tk), :], kr_ref[0, pl.ds(k0, tk), :]], axis=1)
        st = _dot(kcat, qt_ref[qslot])
        s_ref[sslot] = st
        mx_ref[sslot] = jnp.max(st, axis=0, keepdims=True)

    def update(j, sslot):
        k0 = j * tk
        m_old = m_ref[...]
        m_new = jnp.maximum(m_old, mx_ref[sslot])
        alpha = jnp.exp2(m_old - m_new)
        pt = jnp.exp2(s_ref[sslot] - m_new).astype(BF16)
        va = jnp.concatenate([vt_ref[0, :, pl.ds(k0, tk)], ones], axis=0)
        m_ref[...] = m_new
        acc_ref[...] = alpha * acc_ref[...] + _dot(va, pt)

    def reset():
        m_ref[...] = jnp.full(m_ref.shape, NEG_INF, F32)
        acc_ref[...] = jnp.zeros(acc_ref.shape, F32)

    def finish(qi):
        acc = acc_ref[...]
        c0 = pl.multiple_of(qi * tq, tq)
        o_ref[0, 0:C_NOPE, pl.ds(c0, tq)] = (acc[0:C_NOPE, :tq] / acc[LANES:LANES + 1, :tq]).astype(o_ref.dtype)
        o_ref[0, C_NOPE:, pl.ds(c0, tq)] = (acc[C_NOPE:LANES, tq:] / acc[LANES:LANES + 1, tq:]).astype(o_ref.dtype)
        reset()

    def run_tile(qi, slot, next_qi):
        other = 1 - slot
        for j in range(nch - 1):
            scores(slot, j + 1, other if j % 2 == 0 else slot)
            update(j, slot if j % 2 == 0 else other)
        prep_q(next_qi, other)
        scores(other, 0, other)
        update(nch - 1, slot)
        finish(qi)

    def two_tiles(t2, carry):
        run_tile(2 * t2, 0, 2 * t2 + 1)
        run_tile(2 * t2 + 1, 1, jnp.minimum(2 * t2 + 2, nq - 1))
        return carry

    reset()
    prep_q(0, 0)
    scores(0, 0, 0)
    lax.fori_loop(0, nq // 2, two_tiles, 0)


def _flash_attn(qnt, qrt, kn, kr, vt):
    b, n, _ = kn.shape
    s = n - CTX_LEN
    tq = TM
    tk = _kv_tile(n)
    return pl.pallas_call(
        functools.partial(_flash_kernel, tk=tk, tq=tq),
        grid=(b, C_HEADS // 2),
        scratch_shapes=[pltpu.VMEM((2, tk, 2 * tq), F32),
                        pltpu.VMEM((2, 1, 2 * tq), F32),
                        pltpu.VMEM((1, 2 * tq), F32),
                        pltpu.VMEM((LANES + SUM_ROWS, 2 * tq), F32),
                        pltpu.VMEM((2, 2 * LANES, 2 * tq), BF16)],
        in_specs=[
            pl.BlockSpec((1, LANES, n), lambda bb, p: (bb, p, 0)),
            pl.BlockSpec((1, LANES, n), lambda bb, p: (bb, p // 2, 0)),
            pl.BlockSpec((1, n, LANES), lambda bb, p: (bb, 0, p)),
            pl.BlockSpec((1, n, LANES), lambda bb, p: (bb, 0, 0)),
            pl.BlockSpec((1, LANES, n), lambda bb, p: (bb, p, 0)),
        ],
        out_specs=pl.BlockSpec((1, LANES, s), lambda bb, p: (bb, p, 0)),
        out_shape=jax.ShapeDtypeStruct((b, C_HEADS * C_NOPE, s), BF16),
        compiler_params=_params(ARB2, _nbytes((6 * LANES, n), BF16)
                                + _nbytes((tk + LANES + SUM_ROWS + 2 * LANES, 2 * tq), F32)),
        name="mla_flash",
    )(qnt, qrt, kn, kr, vt)


def _rope_tables(seq, dim):
    t = jnp.arange(seq, dtype=jnp.int32)
    pos = jnp.stack([t // GRID_W, t % GRID_W], axis=0).astype(F32)
    half = dim // 2
    q = half // 2
    inv = ROPE_THETA ** (-jnp.arange(q, dtype=F32) / q)
    j = jnp.arange(dim)
    ang = pos[j // half].T * inv[j % q][None, :]
    sign = jnp.where((j % half) < q, -1.0, 1.0).astype(F32)
    cos = jnp.cos(ang)
    sin = jnp.sin(ang) * sign[None, :]
    reps = LANES // dim
    cos = jnp.tile(cos, (1, reps))
    sin = jnp.tile(sin, (1, reps))
    cos = jnp.concatenate([jnp.ones((CTX_LEN, LANES), F32), cos], axis=0)
    sin = jnp.concatenate([jnp.zeros((CTX_LEN, LANES), F32), sin], axis=0)
    return cos, sin


def _group_ones(group):
    r = jnp.arange(MXU_DIM) // group
    return (r[:, None] == r[None, :]).astype(BF16)


def _na_bias_tables(rpb, mult):
    cq = jnp.arange(GRID_W)
    c0 = jnp.clip(cq - NA_COLS // 2, 0, GRID_W - NA_COLS)
    col_ok = (cq[:, None] >= c0[None, :]) & (cq[:, None] < c0[None, :] + NA_COLS)
    dci = jnp.clip(cq[:, None] - cq[None, :], 1 - NA_COLS, NA_COLS - 1) + NA_COLS - 1
    pick = (dci[None] == jnp.arange(2 * NA_COLS - 1)[:, None, None]).astype(F32)
    tt = jnp.einsum("hdm,mkq->hdkq", rpb.astype(F32) * mult, pick, precision=lax.Precision.HIGHEST)
    tt = jnp.where(col_ok[None, None], tt, NEG_INF)
    tt = jnp.concatenate([tt, jnp.full_like(tt[:, :1], NEG_INF)], axis=1)
    zero = jnp.zeros_like(tt)

    def per_group(t):
        t = t.reshape(B_HEADS // NA_HG, NA_HG, NA_BIAS_N, GRID_W, LANES)
        return t.transpose(0, 2, 1, 3, 4)

    return per_group(jnp.concatenate([tt, zero], axis=-1)), per_group(jnp.concatenate([zero, tt], axis=-1))


def kernel(x, c, ctx, c_ctx, ada_w, ada_b, norm_mix, norm_mlp, mlp_w1, mlp_w2, e_w_in, e_w_out, a_q_norm, a_k_norm, a_sink, b_q_norm, b_k_norm, b_rpb, o_w_in, o_qa_norm, o_kva_norm, o_w_uq, o_w_ukv, o_qn_nope, o_qn_rope, o_kn_nope, o_kn_rope, o_w_out):
    bsz, seq, d = x.shape
    assert d == D_MODEL and ctx.shape[1] == CTX_LEN and seq % TM == 0 and ada_w.shape[0] == 2
    assert bsz + 1 <= 8

    cond = jnp.zeros((8, d), F32).at[:bsz].set(c).at[bsz].set(c_ctx)
    m = _adaln(cond, ada_w, ada_b)
    mods = [jnp.stack([jnp.broadcast_to(m[i, bsz], (bsz, 6 * d)), m[i, :bsz]], axis=1).reshape(2 * bsz, 1, 6 * d)
            for i in range(2)]

    e64 = _group_ones(HEAD_DIM)
    e32 = _group_ones(C_ROPE)

    w_ext = e_w_in[0].astype(BF16)
    scale = HEAD_DIM ** -0.5
    log2e = math.log2(math.e)
    gains = jnp.concatenate([jnp.tile(a_q_norm[0], A_HEADS) * (scale * log2e), jnp.tile(a_k_norm[0], A_KV_HEADS),
                             jnp.tile(b_q_norm[0], B_HEADS) * (scale * log2e), jnp.tile(b_k_norm[0], B_HEADS)])[None, :]
    cos64, sin64 = _rope_tables(seq, HEAD_DIM)
    qat, ka, qbt, kb, vat, vbt = _even_proj(ctx, x, mods[0], norm_mix[0][None, :], w_ext, gains, e64, cos64, sin64)
    sink_row = (jnp.repeat(a_sink[0].reshape(A_KV_HEADS, GQA_R), A_BLOCK, axis=1) * log2e)[:, None, :]
    oa = _window_attn(qat, ka, vat, sink_row)
    ob = _na_attn(qbt, kb, vbt, *_na_bias_tables(b_rpb[0], log2e))
    wo = e_w_out[0].astype(BF16)
    xa = _even_out(ctx, x, oa, ob, mods[0], norm_mlp[0][None, :], wo[:512], wo[512:],
                   mlp_w1[0].astype(BF16), mlp_w2[0].astype(BF16))

    wi = o_w_in[0]
    win = jnp.concatenate([wi[:, :C_Q_RANK + C_KV_RANK]] + [wi[:, C_Q_RANK + C_KV_RANK:]] * 4, axis=1).astype(BF16)
    wuq = o_w_uq[0].reshape(C_Q_RANK, C_HEADS, C_NOPE + C_ROPE)
    wuq = jnp.concatenate([wuq[:, :, :C_NOPE].reshape(C_Q_RANK, -1), wuq[:, :, C_NOPE:].reshape(C_Q_RANK, -1)],
                          axis=1).astype(BF16)
    wukv = o_w_ukv[0].reshape(C_KV_RANK, C_HEADS, 2 * C_NOPE)
    wukv = jnp.concatenate([wukv[:, :, :C_NOPE].reshape(C_KV_RANK, -1), wukv[:, :, C_NOPE:].reshape(C_KV_RANK, -1)],
                           axis=1).astype(BF16)
    qscale = (C_NOPE + C_ROPE) ** -0.5 * math.log2(math.e)
    gq = (jnp.concatenate([jnp.tile(o_qn_nope[0], C_HEADS), jnp.tile(o_qn_rope[0], C_HEADS)]) * qscale)[None, :]
    gk = jnp.concatenate([jnp.tile(o_kn_nope[0], C_HEADS), jnp.tile(o_kn_rope[0], LANES // C_ROPE)])[None, :]
    cos32, sin32 = _rope_tables(seq, C_ROPE)
    qn, qr, kn, kr, vt = _odd_proj(xa, mods[1], norm_mix[1][None, :], win, o_qa_norm[0][None, :],
                                   o_kva_norm[0][None, :], wuq, wukv, gq, gk, e64, e32, cos32, sin32)
    ot = _flash_attn(qn, qr, kn, kr, vt)
    return _odd_out(xa, ot, mods[1], norm_mlp[1][None, :], o_w_out[0].astype(BF16),
                    mlp_w1[1].astype(BF16), mlp_w2[1].astype(BF16))
```

```python
import functools
import math

import jax
import jax.numpy as jnp
import numpy as np
from jax import lax
from jax.experimental import pallas as pl
from jax.experimental.pallas import tpu as pltpu

F32 = jnp.float32
BF16 = jnp.bfloat16

D_MODEL = 1024
CTX_LEN = 256
GRID_W = 64
HEAD_DIM = 64
A_HEADS = 8
A_KV_HEADS = 2
A_WINDOW = 128
A_BLOCK = 128
B_HEADS = 8
NA_ROWS = 8
NA_COLS = 16
C_HEADS = 16
C_Q_RANK = 384
C_KV_RANK = 256
C_NOPE = 64
C_ROPE = 32
D_FF = 4 * D_MODEL
ROPE_THETA = 10000.0
NORM_EPS = 1e-6
NEG_INF = -1e30

LANES = 128
MXU_DIM = 256
TM = CTX_LEN
V7X_VMEM_BYTES = 64 * 1024 * 1024
VMEM_VALUE_BYTES = 32 * 1024 * 1024


ARB2 = ("arbitrary", "arbitrary")


def _dot(a, b):
    return jnp.dot(a, b, preferred_element_type=F32)


def _nbytes(shape, dtype, buffers=2):
    return math.prod(shape) * jnp.dtype(dtype).itemsize * buffers


def _params(sem, declared_bytes):
    limit = min(declared_bytes + VMEM_VALUE_BYTES, V7X_VMEM_BYTES * 7 // 8)
    return pltpu.CompilerParams(dimension_semantics=sem, vmem_limit_bytes=limit)


def _norm_mod(x, g, sh, sc):
    ms = jnp.mean(x * x, axis=-1, keepdims=True)
    return (x * lax.rsqrt(ms + NORM_EPS) * g) * (1.0 + sc) + sh


def _group_rms(y, e, group):
    out = []
    for c0 in range(0, y.shape[1], MXU_DIM):
        cw = min(MXU_DIM, y.shape[1] - c0)
        yc = y[:, c0:c0 + cw]
        ss = _dot((yc * yc).astype(BF16), e[:cw, :cw])
        out.append(yc * lax.rsqrt(ss * (1.0 / group) + NORM_EPS))
    return out[0] if len(out) == 1 else jnp.concatenate(out, axis=1)


def _rope(r, cos, sin, half):
    w = r.shape[1]
    reps = w // LANES
    if reps > 1:
        cos = jnp.concatenate([cos] * reps, axis=1)
        sin = jnp.concatenate([sin] * reps, axis=1)
    lane = lax.broadcasted_iota(jnp.int32, r.shape, 1)
    up = pltpu.roll(r, w - half, axis=1)
    dn = pltpu.roll(r, half, axis=1)
    sw = jnp.where((lane & half) == 0, up, dn)
    return r * cos + sw * sin


def _ada_kernel(cond_ref, w_ref, b_ref, o_ref):
    c = cond_ref[...]
    s = (c * jax.nn.sigmoid(c)).astype(BF16)
    o_ref[0] = _dot(s, w_ref[0].astype(BF16)) + b_ref[0]


def _adaln(cond, ada_w, ada_b):
    depth, d, n6 = ada_w.shape
    tn = 1536
    return pl.pallas_call(
        _ada_kernel,
        grid=(depth, n6 // tn),
        in_specs=[
            pl.BlockSpec((8, d), lambda l, j: (0, 0)),
            pl.BlockSpec((1, d, tn), lambda l, j: (l, 0, j)),
            pl.BlockSpec((1, 1, tn), lambda l, j: (l, 0, j)),
        ],
        out_specs=pl.BlockSpec((1, 8, tn), lambda l, j: (l, 0, j)),
        out_shape=jax.ShapeDtypeStruct((depth, 8, n6), F32),
        compiler_params=_params(ARB2, _nbytes((d + 2 * 8, tn), F32)),
        name="adaln",
    )(cond, ada_w, ada_b.reshape(depth, 1, n6))


NSUB = 3


def _mod_spec(chunk, lat_only=False):
    if lat_only:
        return pl.BlockSpec((1, 1, D_MODEL), lambda b, i: (2 * b + 1, 0, chunk))
    return pl.BlockSpec((2, 1, D_MODEL), lambda b, i: (b, 0, chunk))


def _pick_mod(ref, is_ctx):
    m = ref[...]
    return jnp.where(is_ctx, m[0], m[1])


def _is_ctx(k):
    return NSUB * pl.program_id(1) + k == 0


def _const_spec(shape):
    nd = len(shape)
    return pl.BlockSpec(shape, lambda *_: (0,) * nd, pipeline_mode=pl.Buffered(1))


QA_W = A_HEADS * HEAD_DIM
KVA_W = A_KV_HEADS * HEAD_DIM
QKVB_W = B_HEADS * HEAD_DIM
EVEN_ROPED = QA_W + KVA_W
EVEN_QB = EVEN_ROPED + KVA_W
EVEN_VB = EVEN_QB + 2 * QKVB_W
EVEN_COLS = EVEN_VB + QKVB_W
EVEN_NORMED = EVEN_ROPED + 2 * QKVB_W


def _joint_specs(d):
    lat = lambda k: pl.BlockSpec((1, TM, d), lambda bb, i: (bb, jnp.maximum(NSUB * i + k - 1, 0), 0))
    return [pl.BlockSpec((1, CTX_LEN, d), lambda bb, i: (bb, 0, 0))] + [lat(k) for k in range(NSUB)]


def _joint_tile(ctx_ref, x_refs, k):
    return jnp.where(_is_ctx(k), ctx_ref[0], x_refs[k][0])


def _even_proj_kernel(ctx_ref, *refs):
    x_refs = refs[:NSUB]
    (sh_ref, sc_ref, g_ref, w_ref, gains_ref, e_ref, cos_ref, sin_ref,
     qat_ref, ka_ref, qb_ref, kb_ref, vat_ref, vb_ref) = refs[NSUB:]
    for k in range(NSUB):
        rows = slice(k * TM, (k + 1) * TM)
        h = _norm_mod(_joint_tile(ctx_ref, x_refs, k), g_ref[...],
                      _pick_mod(sh_ref, _is_ctx(k)), _pick_mod(sc_ref, _is_ctx(k))).astype(BF16)
        y = _dot(h, w_ref[...])
        gains = gains_ref[...]
        ya = _group_rms(y[:, :EVEN_ROPED], e_ref[...], HEAD_DIM) * gains[:, :EVEN_ROPED]
        r = _rope(ya, cos_ref[rows, :], sin_ref[rows, :], HEAD_DIM // 4)
        yb = _group_rms(y[:, EVEN_QB:EVEN_VB], e_ref[...], HEAD_DIM) * gains[:, EVEN_ROPED:]
        qat_ref[0, :, rows] = r[:, 0:QA_W].T.astype(BF16)
        ka_ref[0, rows, :] = r[:, QA_W:].astype(BF16)
        qb_ref[0, :, rows] = yb[:, :QKVB_W].T.astype(BF16)
        kb_ref[0, rows, :] = yb[:, QKVB_W:].astype(BF16)
        vat_ref[0, :, rows] = y[:, EVEN_ROPED:EVEN_QB].T.astype(BF16)
        vb_ref[0, :, rows] = y[:, EVEN_VB:].T.astype(BF16)


def _even_proj(ctx, x, mods, g, w, gains, e64, cos, sin):
    b, s, d = x.shape
    n = CTX_LEN + s
    tmb = NSUB * TM
    assert n % tmb == 0
    tok = lambda wd: (pl.BlockSpec((1, tmb, wd), lambda bb, i: (bb, i, 0)),
                      jax.ShapeDtypeStruct((b, n, wd), BF16))
    tr = lambda wd: (pl.BlockSpec((1, wd, tmb), lambda bb, i: (bb, 0, i)),
                     jax.ShapeDtypeStruct((b, wd, n), BF16))
    outs = (tr(QA_W), tok(KVA_W), tr(QKVB_W), tok(QKVB_W), tr(KVA_W), tr(QKVB_W))
    return pl.pallas_call(
        _even_proj_kernel,
        grid=(b, n // tmb),
        in_specs=_joint_specs(d) + [
            _mod_spec(0), _mod_spec(1),
            _const_spec((1, d)),
            _const_spec((d, EVEN_COLS)),
            _const_spec((1, EVEN_NORMED)),
            _const_spec((MXU_DIM, MXU_DIM)),
            pl.BlockSpec((tmb, LANES), lambda bb, i: (i, 0)),
            pl.BlockSpec((tmb, LANES), lambda bb, i: (i, 0)),
        ],
        out_specs=[o[0] for o in outs],
        out_shape=[o[1] for o in outs],
        compiler_params=_params(ARB2, _nbytes((CTX_LEN + tmb, d), F32) + _nbytes((d, EVEN_COLS), BF16, 1)
                                + _nbytes((tmb, EVEN_COLS), BF16) + 2 * _nbytes((tmb, LANES), F32)),
        name="even_proj",
    )(ctx, *([x] * NSUB), mods, mods, g, w, gains, e64, cos, sin)


GQA_R = A_HEADS // A_KV_HEADS
WIN_BAND = 3 * A_BLOCK
WIN_LANES = GQA_R * A_BLOCK


PIPE_UNROLL = 2


def _window_kernel(qt_ref, k_ref, vt_ref, sink_ref, wm_ref, o_ref, s_ref, mx_ref, qs_ref, *, seq):
    g = pl.program_id(1)
    nb = seq // A_BLOCK
    nkeys = WIN_BAND + CTX_LEN
    assert nb % PIPE_UNROLL == 0 and PIPE_UNROLL % 2 == 0
    sink = sink_ref[0]
    ones_loc = jnp.ones((SUM_ROWS, nkeys), BF16)
    ones_ctx = jnp.ones((SUM_ROWS, CTX_LEN), BF16)

    def prep_q(col0, qslot):
        qt = qt_ref[0, :, pl.ds(col0, A_BLOCK)]
        zero = jnp.zeros((HEAD_DIM, A_BLOCK), BF16)
        for h in range(GQA_R):
            qh = qt[h * HEAD_DIM:(h + 1) * HEAD_DIM]
            lanes = slice(h * A_BLOCK, (h + 1) * A_BLOCK)
            qs_ref[qslot, 0:HEAD_DIM, lanes] = jnp.where(g == 0, qh, zero)
            qs_ref[qslot, HEAD_DIM:, lanes] = jnp.where(g == 0, zero, qh)

    def band_start(n):
        return pl.multiple_of(jnp.clip((n - 1) * A_BLOCK, 0, seq - WIN_BAND), A_BLOCK)

    def scores(qslot, n, sslot):
        start = band_start(n)
        keys = jnp.concatenate([k_ref[0, pl.ds(CTX_LEN + start, WIN_BAND), :], k_ref[0, 0:CTX_LEN, :]], axis=0)
        st = _dot(keys, qs_ref[qslot])
        mask = wm_ref[n - start // A_BLOCK]
        s_loc = st[:WIN_BAND] + jnp.concatenate([mask] * GQA_R, axis=1)
        s_ctx = st[WIN_BAND:]
        s_ref[sslot, 0:WIN_BAND] = s_loc
        s_ref[sslot, WIN_BAND:] = s_ctx
        mx_ref[sslot] = jnp.maximum(jnp.maximum(jnp.max(s_loc, axis=0, keepdims=True),
                                                jnp.max(s_ctx, axis=0, keepdims=True)), sink)

    def write(res, m, col0):
        o = res[0:HEAD_DIM] / (res[HEAD_DIM:HEAD_DIM + 1] + jnp.exp2(sink - m))
        for h in range(GQA_R):
            o_ref[0, h * HEAD_DIM:(h + 1) * HEAD_DIM, pl.ds(col0, A_BLOCK)] = (
                o[:, h * A_BLOCK:(h + 1) * A_BLOCK].astype(o_ref.dtype))

    def update(n, sslot):
        start = band_start(n)
        m = mx_ref[sslot]
        pt = jnp.exp2(s_ref[sslot] - m).astype(BF16)
        va = jnp.concatenate([vt_ref[0, :, pl.ds(CTX_LEN + start, WIN_BAND)], vt_ref[0, :, 0:CTX_LEN]], axis=1)
        res = _dot(jnp.concatenate([va, ones_loc], axis=0), pt)
        write(res, m, pl.multiple_of(CTX_LEN + n * A_BLOCK, A_BLOCK))

    for blk in range(CTX_LEN // A_BLOCK):
        prep_q(blk * A_BLOCK, 0)
        st = _dot(k_ref[0, 0:CTX_LEN, :], qs_ref[0])
        m = jnp.maximum(jnp.max(st, axis=0, keepdims=True), sink)
        pt = jnp.exp2(st - m).astype(BF16)
        res = _dot(jnp.concatenate([vt_ref[0, :, 0:CTX_LEN], ones_ctx], axis=0), pt)
        write(res, m, blk * A_BLOCK)

    def some_blocks(t, carry):
        for u in range(PIPE_UNROLL):
            nxt = jnp.minimum(PIPE_UNROLL * t + u + 1, nb - 1)
            prep_q(pl.multiple_of(CTX_LEN + nxt * A_BLOCK, A_BLOCK), (u + 1) % 2)
            scores((u + 1) % 2, nxt, (u + 1) % 2)
            update(PIPE_UNROLL * t + u, u % 2)
        return carry

    prep_q(CTX_LEN, 0)
    scores(0, 0, 0)
    lax.fori_loop(0, nb // PIPE_UNROLL, some_blocks, 0)


def _window_mask():
    o = jnp.arange(3)[:, None, None]
    key = jnp.arange(WIN_BAND)[None, :, None]
    qry = jnp.arange(A_BLOCK)[None, None, :]
    return jnp.where(jnp.abs(qry + o * A_BLOCK - key) <= A_WINDOW, 0.0, NEG_INF).astype(F32)


def _window_attn(qat, ka, vat, sink_row):
    b, _, n = qat.shape
    return pl.pallas_call(
        functools.partial(_window_kernel, seq=n - CTX_LEN),
        grid=(b, A_KV_HEADS),
        scratch_shapes=[pltpu.VMEM((2, WIN_BAND + CTX_LEN, WIN_LANES), F32),
                        pltpu.VMEM((2, 1, WIN_LANES), F32),
                        pltpu.VMEM((2, 2 * HEAD_DIM, WIN_LANES), BF16)],
        in_specs=[
            pl.BlockSpec((1, GQA_R * HEAD_DIM, n), lambda bb, g: (bb, g, 0)),
            pl.BlockSpec((1, n, KVA_W), lambda bb, g: (bb, 0, 0)),
            pl.BlockSpec((1, HEAD_DIM, n), lambda bb, g: (bb, g, 0)),
            pl.BlockSpec((1, 1, WIN_LANES), lambda bb, g: (g, 0, 0)),
            _const_spec((3, WIN_BAND, A_BLOCK)),
        ],
        out_specs=pl.BlockSpec((1, GQA_R * HEAD_DIM, n), lambda bb, g: (bb, g, 0)),
        out_shape=jax.ShapeDtypeStruct((b, QA_W, n), BF16),
        compiler_params=_params(ARB2, _nbytes((2 * GQA_R * HEAD_DIM + KVA_W + HEAD_DIM, n), BF16)
                                + _nbytes((WIN_BAND + CTX_LEN + 2 * HEAD_DIM, WIN_LANES), F32)
                                + _nbytes((3, WIN_BAND, A_BLOCK), F32, 1)),
        name="window_attn",
    )(qat, ka, vat, sink_row, _window_mask())


NA_ITEM_ROWS = LANES // GRID_W
NA_WIN_ROWS = NA_ROWS + NA_ITEM_ROWS
NA_HG = 4
NA_W = NA_HG * HEAD_DIM
NA_LANES = NA_HG * LANES
NA_BIAS_N = 2 * NA_ROWS


def _na_kernel(qt_ref, k_ref, vt_ref, ta_ref, tb_ref, o_ref, s_ref, mx_ref, qs_ref, *, rows_n):
    nitems = rows_n // NA_ITEM_ROWS
    nloc = NA_WIN_ROWS * GRID_W
    assert nitems % PIPE_UNROLL == 0 and (rows_n - NA_WIN_ROWS) % 2 == 0 and nloc % LANES == 0
    ones_loc = jnp.ones((SUM_ROWS, nloc + CTX_LEN), BF16)
    ones_ctx = jnp.ones((SUM_ROWS, CTX_LEN), BF16)

    def prep_q(col0, qslot):
        qt = qt_ref[0, :, pl.ds(col0, LANES)]
        for h in range(NA_HG):
            rows = slice(h * HEAD_DIM, (h + 1) * HEAD_DIM)
            qs_ref[qslot, rows, h * LANES:(h + 1) * LANES] = qt[rows]

    def win_start(j):
        return jnp.clip(NA_ITEM_ROWS * j - NA_ROWS // 2, 0, rows_n - NA_WIN_ROWS)

    def scores(qslot, j, sslot):
        ru = win_start(j)
        k0 = pl.multiple_of(CTX_LEN + ru * GRID_W, LANES)
        keys = jnp.concatenate([k_ref[0, pl.ds(k0, nloc), :], k_ref[0, 0:CTX_LEN, :]], axis=0)
        st = _dot(keys, qs_ref[qslot])
        mx = jnp.max(st[nloc:], axis=0, keepdims=True)
        s_ref[sslot, nloc:] = st[nloc:]
        for i in range(NA_WIN_ROWS):
            idx = []
            for e in range(NA_ITEM_ROWS):
                r = NA_ITEM_ROWS * j + e
                r0 = jnp.clip(r - NA_ROWS // 2, 0, rows_n - NA_ROWS)
                seen = (ru + i >= r0) & (ru + i < r0 + NA_ROWS)
                idx.append(jnp.where(seen, ru + i - r + NA_ROWS - 1, NA_BIAS_N - 1))
            rows = slice(i * GRID_W, (i + 1) * GRID_W)
            bias = ta_ref[0, idx[0]] + tb_ref[0, idx[1]]
            blk = st[rows] + jnp.concatenate([bias[h] for h in range(NA_HG)], axis=1)
            s_ref[sslot, rows] = blk
            mx = jnp.maximum(mx, jnp.max(blk, axis=0, keepdims=True))
        mx_ref[sslot] = mx

    def pv_write(va, ones, pt, col0):
        for pair in range(NA_HG // 2):
            rows = slice(pair * LANES, (pair + 1) * LANES)
            res = _dot(jnp.concatenate([va[rows], ones], axis=0),
                       pt[:, pair * 2 * LANES:(pair + 1) * 2 * LANES])
            for hh in range(2):
                lanes = slice(hh * LANES, (hh + 1) * LANES)
                o = res[hh * HEAD_DIM:(hh + 1) * HEAD_DIM, lanes] / res[LANES:LANES + 1, lanes]
                h = 2 * pair + hh
                o_ref[0, h * HEAD_DIM:(h + 1) * HEAD_DIM, pl.ds(col0, LANES)] = o.astype(o_ref.dtype)

    def update(j, sslot):
        k0 = pl.multiple_of(CTX_LEN + win_start(j) * GRID_W, LANES)
        pt = jnp.exp2(s_ref[sslot] - mx_ref[sslot]).astype(BF16)
        va = jnp.concatenate([vt_ref[0, :, pl.ds(k0, nloc)], vt_ref[0, :, 0:CTX_LEN]], axis=1)
        pv_write(va, ones_loc, pt, pl.multiple_of(CTX_LEN + j * LANES, LANES))

    qs_ref[...] = jnp.zeros(qs_ref.shape, BF16)
    for blk in range(CTX_LEN // LANES):
        prep_q(blk * LANES, 0)
        st = _dot(k_ref[0, 0:CTX_LEN, :], qs_ref[0])
        pt = jnp.exp2(st - jnp.max(st, axis=0, keepdims=True)).astype(BF16)
        pv_write(vt_ref[0, :, 0:CTX_LEN], ones_ctx, pt, blk * LANES)

    def some_items(t, carry):
        for u in range(PIPE_UNROLL):
            nxt = jnp.minimum(PIPE_UNROLL * t + u + 1, nitems - 1)
            prep_q(pl.multiple_of(CTX_LEN + nxt * LANES, LANES), (u + 1) % 2)
            scores((u + 1) % 2, nxt, (u + 1) % 2)
            update(PIPE_UNROLL * t + u, u % 2)
        return carry

    prep_q(CTX_LEN, 0)
    scores(0, 0, 0)
    lax.fori_loop(0, nitems // PIPE_UNROLL, some_items, 0)


def _na_attn(qbt, kb, vbt, ta, tb):
    b, _, n = qbt.shape
    rows_n = (n - CTX_LEN) // GRID_W
    assert rows_n >= NA_WIN_ROWS
    nkeys = NA_WIN_ROWS * GRID_W + CTX_LEN
    tspec = pl.BlockSpec((1, NA_BIAS_N, NA_HG, GRID_W, LANES), lambda bb, hg: (hg, 0, 0, 0, 0))
    return pl.pallas_call(
        functools.partial(_na_kernel, rows_n=rows_n),
        grid=(b, B_HEADS // NA_HG),
        scratch_shapes=[pltpu.VMEM((2, nkeys, NA_LANES), F32),
                        pltpu.VMEM((2, 1, NA_LANES), F32),
                        pltpu.VMEM((2, NA_W, NA_LANES), BF16)],
        in_specs=[
            pl.BlockSpec((1, NA_W, n), lambda bb, hg: (bb, hg, 0)),
            pl.BlockSpec((1, n, NA_W), lambda bb, hg: (bb, 0, hg)),
            pl.BlockSpec((1, NA_W, n), lambda bb, hg: (bb, hg, 0)),
            tspec, tspec,
        ],
        out_specs=pl.BlockSpec((1, NA_W, n), lambda bb, hg: (bb, hg, 0)),
        out_shape=jax.ShapeDtypeStruct((b, QKVB_W, n), BF16),
        compiler_params=_params(ARB2, _nbytes((4 * NA_W, n), BF16) + _nbytes((nkeys + NA_W, NA_LANES), F32)
                                + 2 * _nbytes((NA_BIAS_N, NA_HG, GRID_W, LANES), F32)),
        name="na_attn",
    )(qbt, kb, vbt, ta, tb)


def _mlp_tail(xs, y, mods, gn, w1_ref, w2_ref):
    x1s = [x + m[0] * y[k * TM:(k + 1) * TM] for k, (x, m) in enumerate(zip(xs, mods))]
    h = jnp.concatenate([_norm_mod(x1, gn, m[1], m[2]).astype(BF16) for x1, m in zip(x1s, mods)], axis=0)
    mlp = None
    for j in range(D_FF // D_MODEL):
        cols = slice(j * D_MODEL, (j + 1) * D_MODEL)
        u = jnp.maximum(_dot(h, w1_ref[:, cols]), 0.0)
        part = _dot((u * u).astype(BF16), w2_ref[cols, :])
        mlp = part if mlp is None else mlp + part
    return [x1 + m[3] * mlp[k * TM:(k + 1) * TM] for k, (x1, m) in enumerate(zip(x1s, mods))]


def _even_out_kernel(ctx_ref, *refs):
    x_refs = refs[:NSUB]
    (oat_ref, obt_ref, g1_ref, sh2_ref, sc2_ref, g2_ref, gn_ref,
     woa_ref, wob_ref, w1_ref, w2_ref, o_ref) = refs[NSUB:]
    oa = oat_ref[0].astype(F32).T.astype(BF16)
    ob = obt_ref[0].astype(F32).T.astype(BF16)
    y = _dot(oa, woa_ref[...]) + _dot(ob, wob_ref[...])
    xs = [_joint_tile(ctx_ref, x_refs, k) for k in range(NSUB)]
    mods = [[_pick_mod(r, _is_ctx(k)) for r in (g1_ref, sh2_ref, sc2_ref, g2_ref)] for k in range(NSUB)]
    for k, out in enumerate(_mlp_tail(xs, y, mods, gn_ref[...], w1_ref, w2_ref)):
        o_ref[0, k * TM:(k + 1) * TM, :] = out


def _even_out(ctx, x, oa, ob, mods, gn, woa, wob, w1, w2):
    b, s, d = x.shape
    n = CTX_LEN + s
    tmb = NSUB * TM
    return pl.pallas_call(
        _even_out_kernel,
        grid=(b, n // tmb),
        in_specs=_joint_specs(d) + [
            pl.BlockSpec((1, QA_W, tmb), lambda bb, i: (bb, 0, i)),
            pl.BlockSpec((1, QKVB_W, tmb), lambda bb, i: (bb, 0, i)),
            _mod_spec(2), _mod_spec(3), _mod_spec(4), _mod_spec(5),
            _const_spec((1, d)),
            _const_spec((QA_W, d)), _const_spec((QKVB_W, d)),
            _const_spec((d, D_FF)), _const_spec((D_FF, d)),
        ],
        out_specs=pl.BlockSpec((1, tmb, d), lambda bb, i: (bb, i, 0)),
        out_shape=jax.ShapeDtypeStruct((b, n, d), F32),
        compiler_params=_params(ARB2, _nbytes((CTX_LEN + 2 * tmb, d), F32) + _nbytes((QA_W + QKVB_W, tmb), BF16)
                                + _nbytes((d + 2 * D_FF, d), BF16, 1)),
        name="even_out_mlp",
    )(ctx, *([x] * NSUB), oa, ob, mods, mods, mods, mods, gn, woa, wob, w1, w2)


NSUB_ODD = 2


def _odd_out_kernel(*refs):
    x_refs = refs[:NSUB_ODD]
    ot_ref, g1_ref, sh2_ref, sc2_ref, g2_ref, gn_ref, wo_ref, w1_ref, w2_ref, o_ref = refs[NSUB_ODD:]
    o = ot_ref[0].astype(F32).T.astype(BF16)
    y = _dot(o, wo_ref[...])
    mods = [[g1_ref[0], sh2_ref[0], sc2_ref[0], g2_ref[0]]] * NSUB_ODD
    for k, out in enumerate(_mlp_tail([r[0] for r in x_refs], y, mods, gn_ref[...], w1_ref, w2_ref)):
        o_ref[0, k * TM:(k + 1) * TM, :] = out


def _odd_out(xa, ot, mods, gn, wo, w1, w2):
    b, n, d = xa.shape
    s = n - CTX_LEN
    tmb = NSUB_ODD * TM
    assert s % tmb == 0
    lat = lambda k: pl.BlockSpec((1, TM, d), lambda bb, i: (bb, NSUB_ODD * i + k + 1, 0))
    return pl.pallas_call(
        _odd_out_kernel,
        grid=(b, s // tmb),
        in_specs=[lat(k) for k in range(NSUB_ODD)] + [
            pl.BlockSpec((1, d, tmb), lambda bb, i: (bb, 0, i)),
            _mod_spec(2, True), _mod_spec(3, True), _mod_spec(4, True), _mod_spec(5, True),
            _const_spec((1, d)),
            _const_spec((d, d)),
            _const_spec((d, D_FF)), _const_spec((D_FF, d)),
        ],
        out_specs=pl.BlockSpec((1, tmb, d), lambda bb, i: (bb, i, 0)),
        out_shape=jax.ShapeDtypeStruct((b, s, d), F32),
        compiler_params=_params(ARB2, _nbytes((2 * tmb, d), F32) + _nbytes((d, tmb), BF16)
                                + _nbytes((d + 2 * D_FF, d), BF16, 1)),
        name="odd_out_mlp",
    )(*([xa] * NSUB_ODD), ot, mods, mods, mods, mods, gn, wo, w1, w2)


ODD_IN_COLS = C_Q_RANK + C_KV_RANK + LANES
Q_NOPE_W = C_HEADS * C_NOPE
Q_ROPE_W = C_HEADS * C_ROPE


def _odd_proj_kernel(x_ref, sh_ref, sc_ref, g_ref, win_ref, qag_ref, kvg_ref, wuq_ref, wukv_ref,
                     gq_ref, gk_ref, e64_ref, e32_ref, cos_ref, sin_ref,
                     qn_ref, qr_ref, kn_ref, kr_ref, vt_ref):
    def rms(t, g):
        return (t * lax.rsqrt(jnp.mean(t * t, axis=-1, keepdims=True) + NORM_EPS) * g).astype(BF16)

    gq = gq_ref[...]
    gk = gk_ref[...]
    for k in range(NSUB):
        rows = slice(k * TM, (k + 1) * TM)
        h = _norm_mod(x_ref[0, rows, :], g_ref[...],
                      _pick_mod(sh_ref, _is_ctx(k)), _pick_mod(sc_ref, _is_ctx(k))).astype(BF16)
        y = _dot(h, win_ref[...])
        cq = y[:, :C_Q_RANK]
        ckv = y[:, C_Q_RANK:C_Q_RANK + C_KV_RANK]
        kr = y[:, C_Q_RANK + C_KV_RANK:]
        q = _dot(rms(cq, qag_ref[...]), wuq_ref[...])
        kv = _dot(rms(ckv, kvg_ref[...]), wukv_ref[...])
        cos = cos_ref[rows, :]
        sin = sin_ref[rows, :]
        qn = _group_rms(q[:, :Q_NOPE_W], e64_ref[...], C_NOPE) * gq[:, :Q_NOPE_W]
        qn_ref[0, :, rows] = qn.T.astype(BF16)
        qr = _group_rms(q[:, Q_NOPE_W:], e32_ref[...], C_ROPE) * gq[:, Q_NOPE_W:]
        qr_ref[0, :, rows] = _rope(qr, cos, sin, C_ROPE // 4).T.astype(BF16)
        kn = _group_rms(kv[:, :Q_NOPE_W], e64_ref[...], C_NOPE) * gk[:, :Q_NOPE_W]
        kn_ref[0, rows, :] = kn.astype(BF16)
        krn = _group_rms(kr, e32_ref[...], C_ROPE) * gk[:, Q_NOPE_W:]
        kr_ref[0, rows, :] = _rope(krn, cos, sin, C_ROPE // 4).astype(BF16)
        vt_ref[0, :, rows] = kv[:, Q_NOPE_W:].T.astype(BF16)


def _odd_proj(xa, mods, g, win, qag, kvg, wuq, wukv, gq, gk, e64, e32, cos, sin):
    b, n, d = xa.shape
    tmb = NSUB * TM
    assert n % tmb == 0
    tok = lambda wd: pl.BlockSpec((1, tmb, wd), lambda bb, i: (bb, i, 0))
    return pl.pallas_call(
        _odd_proj_kernel,
        grid=(b, n // tmb),
        in_specs=[
            tok(d), _mod_spec(0), _mod_spec(1),
            _const_spec((1, d)),
            _const_spec((d, ODD_IN_COLS)),
            _const_spec((1, C_Q_RANK)), _const_spec((1, C_KV_RANK)),
            _const_spec((C_Q_RANK, Q_NOPE_W + Q_ROPE_W)),
            _const_spec((C_KV_RANK, 2 * Q_NOPE_W)),
            _const_spec((1, Q_NOPE_W + Q_ROPE_W)), _const_spec((1, Q_NOPE_W + LANES)),
            _const_spec((MXU_DIM, MXU_DIM)), _const_spec((MXU_DIM, MXU_DIM)),
            pl.BlockSpec((tmb, LANES), lambda bb, i: (i, 0)),
            pl.BlockSpec((tmb, LANES), lambda bb, i: (i, 0)),
        ],
        out_specs=[pl.BlockSpec((1, Q_NOPE_W, tmb), lambda bb, i: (bb, 0, i)),
                   pl.BlockSpec((1, Q_ROPE_W, tmb), lambda bb, i: (bb, 0, i)),
                   tok(Q_NOPE_W), tok(LANES),
                   pl.BlockSpec((1, Q_NOPE_W, tmb), lambda bb, i: (bb, 0, i))],
        out_shape=[jax.ShapeDtypeStruct((b, Q_NOPE_W, n), BF16),
                   jax.ShapeDtypeStruct((b, Q_ROPE_W, n), BF16),
                   jax.ShapeDtypeStruct((b, n, Q_NOPE_W), BF16),
                   jax.ShapeDtypeStruct((b, n, LANES), BF16),
                   jax.ShapeDtypeStruct((b, Q_NOPE_W, n), BF16)],
        compiler_params=_params(ARB2, _nbytes((tmb, d), F32) + 2 * _nbytes((tmb, LANES), F32)
                                + _nbytes((tmb, 3 * Q_NOPE_W + Q_ROPE_W + LANES), BF16)
                                + _nbytes((d + C_Q_RANK + C_KV_RANK, 2 * Q_NOPE_W), BF16, 1)),
        name="odd_proj",
    )(xa, mods, mods, g, win, qag, kvg, wuq, wukv, gq, gk, e64, e32, cos, sin)


SUM_ROWS = 16
FLASH_TILES_PER_TRIP = 4


def _kv_tile(n):
    for t in (768, 256):
        if n % t == 0 and (n // t) % 2 == 1:
            return t
    raise ValueError(f"joint sequence length {n} has no odd split into 256-multiples")


def _flash_kernel(qnt_ref, qrt_ref, kn_ref, kr_ref, vt_ref, o_ref,
                  s_ref, mx_ref, m_ref, acc_ref, qt_ref, *, tk, tq):
    p = pl.program_id(1)
    n = kn_ref.shape[1]
    nq = (n - CTX_LEN) // tq
    nch = n // tk
    assert nq % 2 == 0 and nch % 2 == 1
    ones = jnp.ones((SUM_ROWS, tk), BF16)
    row = lax.broadcasted_iota(jnp.int32, (LANES, tq), 0)

    def prep_q(qi, qslot):
        c0 = pl.multiple_of(CTX_LEN + qi * tq, tq)
        qn = qnt_ref[0, :, pl.ds(c0, tq)]
        qr = qrt_ref[0, :, pl.ds(c0, tq)]
        zero = jnp.zeros_like(qn)
        for hh in range(2):
            off = ((2 * p + hh) % 4) * C_ROPE
            keep_n = (row < C_NOPE) if hh == 0 else (row >= C_NOPE)
            keep_r = (row >= off) & (row < off + C_ROPE)
            qt_ref[qslot, 0:LANES, hh * tq:(hh + 1) * tq] = jnp.where(keep_n, qn, zero)
            qt_ref[qslot, LANES:, hh * tq:(hh + 1) * tq] = jnp.where(keep_r, qr, zero)

    def scores(qslot, j, sslot):
        k0 = j * tk
        kcat = jnp.concatenate([kn_ref[0, pl.ds(k0, tk), :], kr_ref[0, pl.ds(k0, tk), :]], axis=1)
        st = _dot(kcat, qt_ref[qslot])
        s_ref[sslot] = st
        mx_ref[sslot] = jnp.max(st, axis=0, keepdims=True)

    def update(j, sslot):
        k0 = j * tk
        m_old = m_ref[...]
        m_new = jnp.maximum(m_old, mx_ref[sslot])
        alpha = jnp.exp2(m_old - m_new)
        pt = jnp.exp2(s_ref[sslot] - m_new).astype(BF16)
        va = jnp.concatenate([vt_ref[0, :, pl.ds(k0, tk)], ones], axis=0)
        m_ref[...] = m_new
        acc_ref[...] = alpha * acc_ref[...] + _dot(va, pt)

    def reset():
        m_ref[...] = jnp.full(m_ref.shape, NEG_INF, F32)
        acc_ref[...] = jnp.zeros(acc_ref.shape, F32)

    def finish(qi):
        acc = acc_ref[...]
        c0 = pl.multiple_of(qi * tq, tq)
        o_ref[0, 0:C_NOPE, pl.ds(c0, tq)] = (acc[0:C_NOPE, :tq] / acc[LANES:LANES + 1, :tq]).astype(o_ref.dtype)
        o_ref[0, C_NOPE:, pl.ds(c0, tq)] = (acc[C_NOPE:LANES, tq:] / acc[LANES:LANES + 1, tq:]).astype(o_ref.dtype)
        reset()

    def run_tile(qi, slot, next_qi):
        other = 1 - slot
        for j in range(nch - 1):
            scores(slot, j + 1, other if j % 2 == 0 else slot)
            update(j, slot if j % 2 == 0 else other)
        prep_q(next_qi, other)
        scores(other, 0, other)
        update(nch - 1, slot)
        finish(qi)

    def some_tiles(t, carry):
        for u in range(FLASH_TILES_PER_TRIP):
            qi = FLASH_TILES_PER_TRIP * t + u
            run_tile(qi, u % 2, jnp.minimum(qi + 1, nq - 1))
        return carry

    assert nq % FLASH_TILES_PER_TRIP == 0 and FLASH_TILES_PER_TRIP % 2 == 0
    reset()
    prep_q(0, 0)
    scores(0, 0, 0)
    lax.fori_loop(0, nq // FLASH_TILES_PER_TRIP, some_tiles, 0)


def _flash_attn(qnt, qrt, kn, kr, vt):
    b, n, _ = kn.shape
    s = n - CTX_LEN
    tq = TM
    tk = _kv_tile(n)
    return pl.pallas_call(
        functools.partial(_flash_kernel, tk=tk, tq=tq),
        grid=(b, C_HEADS // 2),
        scratch_shapes=[pltpu.VMEM((2, tk, 2 * tq), F32),
                        pltpu.VMEM((2, 1, 2 * tq), F32),
                        pltpu.VMEM((1, 2 * tq), F32),
                        pltpu.VMEM((LANES + SUM_ROWS, 2 * tq), F32),
                        pltpu.VMEM((2, 2 * LANES, 2 * tq), BF16)],
        in_specs=[
            pl.BlockSpec((1, LANES, n), lambda bb, p: (bb, p, 0)),
            pl.BlockSpec((1, LANES, n), lambda bb, p: (bb, p // 2, 0)),
            pl.BlockSpec((1, n, LANES), lambda bb, p: (bb, 0, p)),
            pl.BlockSpec((1, n, LANES), lambda bb, p: (bb, 0, 0)),
            pl.BlockSpec((1, LANES, n), lambda bb, p: (bb, p, 0)),
        ],
        out_specs=pl.BlockSpec((1, LANES, s), lambda bb, p: (bb, p, 0)),
        out_shape=jax.ShapeDtypeStruct((b, C_HEADS * C_NOPE, s), BF16),
        compiler_params=_params(ARB2, _nbytes((6 * LANES, n), BF16)
                                + _nbytes((tk + LANES + SUM_ROWS + 2 * LANES, 2 * tq), F32)),
        name="mla_flash",
    )(qnt, qrt, kn, kr, vt)


def _rope_tables(seq, dim):
    t = np.arange(seq)
    pos = np.stack([t // GRID_W, t % GRID_W], axis=0).astype(np.float64)
    half = dim // 2
    q = half // 2
    inv = ROPE_THETA ** (-np.arange(q, dtype=np.float64) / q)
    j = np.arange(dim)
    ang = pos[j // half].T * inv[j % q][None, :]
    sign = np.where((j % half) < q, -1.0, 1.0)
    reps = LANES // dim
    cos = np.tile(np.cos(ang), (1, reps))
    sin = np.tile(np.sin(ang) * sign[None, :], (1, reps))
    cos = np.concatenate([np.ones((CTX_LEN, LANES)), cos], axis=0)
    sin = np.concatenate([np.zeros((CTX_LEN, LANES)), sin], axis=0)
    return jnp.asarray(cos, F32), jnp.asarray(sin, F32)


def _group_ones(group):
    r = jnp.arange(MXU_DIM) // group
    return (r[:, None] == r[None, :]).astype(BF16)


def _na_bias_tables(rpb, mult):
    cq = jnp.arange(GRID_W)
    c0 = jnp.clip(cq - NA_COLS // 2, 0, GRID_W - NA_COLS)
    col_ok = (cq[:, None] >= c0[None, :]) & (cq[:, None] < c0[None, :] + NA_COLS)
    dci = jnp.clip(cq[:, None] - cq[None, :], 1 - NA_COLS, NA_COLS - 1) + NA_COLS - 1
    pick = (dci[None] == jnp.arange(2 * NA_COLS - 1)[:, None, None]).astype(F32)
    tt = jnp.einsum("hdm,mkq->hdkq", rpb.astype(F32) * mult, pick, precision=lax.Precision.HIGHEST)
    tt = jnp.where(col_ok[None, None], tt, NEG_INF)
    tt = jnp.concatenate([tt, jnp.full_like(tt[:, :1], NEG_INF)], axis=1)
    zero = jnp.zeros_like(tt)

    def per_group(t):
        t = t.reshape(B_HEADS // NA_HG, NA_HG, NA_BIAS_N, GRID_W, LANES)
        return t.transpose(0, 2, 1, 3, 4)

    return per_group(jnp.concatenate([tt, zero], axis=-1)), per_group(jnp.concatenate([zero, tt], axis=-1))


def kernel(x, c, ctx, c_ctx, ada_w, ada_b, norm_mix, norm_mlp, mlp_w1, mlp_w2, e_w_in, e_w_out, a_q_norm, a_k_norm, a_sink, b_q_norm, b_k_norm, b_rpb, o_w_in, o_qa_norm, o_kva_norm, o_w_uq, o_w_ukv, o_qn_nope, o_qn_rope, o_kn_nope, o_kn_rope, o_w_out):
    bsz, seq, d = x.shape
    assert d == D_MODEL and ctx.shape[1] == CTX_LEN and seq % TM == 0 and ada_w.shape[0] == 2
    assert bsz + 1 <= 8

    cond = jnp.zeros((8, d), F32).at[:bsz].set(c).at[bsz].set(c_ctx)
    m = _adaln(cond, ada_w, ada_b)
    mods = [jnp.stack([jnp.broadcast_to(m[i, bsz], (bsz, 6 * d)), m[i, :bsz]], axis=1).reshape(2 * bsz, 1, 6 * d)
            for i in range(2)]

    e64 = _group_ones(HEAD_DIM)
    e32 = _group_ones(C_ROPE)

    w_ext = e_w_in[0].astype(BF16)
    scale = HEAD_DIM ** -0.5
    log2e = math.log2(math.e)
    gains = jnp.concatenate([jnp.tile(a_q_norm[0], A_HEADS) * (scale * log2e), jnp.tile(a_k_norm[0], A_KV_HEADS),
                             jnp.tile(b_q_norm[0], B_HEADS) * (scale * log2e), jnp.tile(b_k_norm[0], B_HEADS)])[None, :]
    cos64, sin64 = _rope_tables(seq, HEAD_DIM)
    qat, ka, qbt, kb, vat, vbt = _even_proj(ctx, x, mods[0], norm_mix[0][None, :], w_ext, gains, e64, cos64, sin64)
    sink_row = (jnp.repeat(a_sink[0].reshape(A_KV_HEADS, GQA_R), A_BLOCK, axis=1) * log2e)[:, None, :]
    oa = _window_attn(qat, ka, vat, sink_row)
    ob = _na_attn(qbt, kb, vbt, *_na_bias_tables(b_rpb[0], log2e))
    wo = e_w_out[0].astype(BF16)
    xa = _even_out(ctx, x, oa, ob, mods[0], norm_mlp[0][None, :], wo[:512], wo[512:],
                   mlp_w1[0].astype(BF16), mlp_w2[0].astype(BF16))

    wi = o_w_in[0]
    win = jnp.concatenate([wi[:, :C_Q_RANK + C_KV_RANK]] + [wi[:, C_Q_RANK + C_KV_RANK:]] * 4, axis=1).astype(BF16)
    wuq = o_w_uq[0].reshape(C_Q_RANK, C_HEADS, C_NOPE + C_ROPE)
    wuq = jnp.concatenate([wuq[:, :, :C_NOPE].reshape(C_Q_RANK, -1), wuq[:, :, C_NOPE:].reshape(C_Q_RANK, -1)],
                          axis=1).astype(BF16)
    wukv = o_w_ukv[0].reshape(C_KV_RANK, C_HEADS, 2 * C_NOPE)
    wukv = jnp.concatenate([wukv[:, :, :C_NOPE].reshape(C_KV_RANK, -1), wukv[:, :, C_NOPE:].reshape(C_KV_RANK, -1)],
                           axis=1).astype(BF16)
    qscale = (C_NOPE + C_ROPE) ** -0.5 * math.log2(math.e)
    gq = (jnp.concatenate([jnp.tile(o_qn_nope[0], C_HEADS), jnp.tile(o_qn_rope[0], C_HEADS)]) * qscale)[None, :]
    gk = jnp.concatenate([jnp.tile(o_kn_nope[0], C_HEADS), jnp.tile(o_kn_rope[0], LANES // C_ROPE)])[None, :]
    cos32, sin32 = _rope_tables(seq, C_ROPE)
    qn, qr, kn, kr, vt = _odd_proj(xa, mods[1], norm_mix[1][None, :], win, o_qa_norm[0][None, :],
                                   o_kva_norm[0][None, :], wuq, wukv, gq, gk, e64, e32, cos32, sin32)
    ot = _flash_attn(qn, qr, kn, kr, vt)
    return _odd_out(xa, ot, mods[1], norm_mlp[1][None, :], o_w_out[0].astype(BF16),
                    mlp_w1[1].astype(BF16), mlp_w2[1].astype(BF16))
```

```python
import functools
import math

import jax
import jax.numpy as jnp
import numpy as np
from jax import lax
from jax.experimental import pallas as pl
from jax.experimental.pallas import tpu as pltpu

F32 = jnp.float32
BF16 = jnp.bfloat16

D_MODEL = 1024
CTX_LEN = 256
GRID_W = 64
HEAD_DIM = 64
A_HEADS = 8
A_KV_HEADS = 2
A_WINDOW = 128
A_BLOCK = 128
B_HEADS = 8
NA_ROWS = 8
NA_COLS = 16
C_HEADS = 16
C_Q_RANK = 384
C_KV_RANK = 256
C_NOPE = 64
C_ROPE = 32
D_FF = 4 * D_MODEL
ROPE_THETA = 10000.0
NORM_EPS = 1e-6
NEG_INF = -1e30

LANES = 128
MXU_DIM = 256
TM = CTX_LEN
V7X_VMEM_BYTES = 64 * 1024 * 1024
VMEM_VALUE_BYTES = 32 * 1024 * 1024


ARB2 = ("arbitrary", "arbitrary")


def _dot(a, b):
    return jnp.dot(a, b, preferred_element_type=F32)


def _nbytes(shape, dtype, buffers=2):
    return math.prod(shape) * jnp.dtype(dtype).itemsize * buffers


def _params(sem, declared_bytes):
    limit = min(declared_bytes + VMEM_VALUE_BYTES, V7X_VMEM_BYTES * 7 // 8)
    return pltpu.CompilerParams(dimension_semantics=sem, vmem_limit_bytes=limit)


def _norm_mod(x, g, sh, sc):
    ms = jnp.mean(x * x, axis=-1, keepdims=True)
    return (x * lax.rsqrt(ms + NORM_EPS) * g) * (1.0 + sc) + sh


def _group_rms(y, e, group):
    out = []
    for c0 in range(0, y.shape[1], MXU_DIM):
        cw = min(MXU_DIM, y.shape[1] - c0)
        yc = y[:, c0:c0 + cw]
        ss = _dot((yc * yc).astype(BF16), e[:cw, :cw])
        out.append(yc * lax.rsqrt(ss * (1.0 / group) + NORM_EPS))
    return out[0] if len(out) == 1 else jnp.concatenate(out, axis=1)


def _rope(r, cos, sin, half):
    w = r.shape[1]
    reps = w // LANES
    if reps > 1:
        cos = jnp.concatenate([cos] * reps, axis=1)
        sin = jnp.concatenate([sin] * reps, axis=1)
    lane = lax.broadcasted_iota(jnp.int32, r.shape, 1)
    up = pltpu.roll(r, w - half, axis=1)
    dn = pltpu.roll(r, half, axis=1)
    sw = jnp.where((lane & half) == 0, up, dn)
    return r * cos + sw * sin


def _ada_kernel(cond_ref, w_ref, b_ref, o_ref):
    c = cond_ref[...]
    s = (c * jax.nn.sigmoid(c)).astype(BF16)
    o_ref[0] = _dot(s, w_ref[0].astype(BF16)) + b_ref[0]


def _adaln(cond, ada_w, ada_b):
    depth, d, n6 = ada_w.shape
    tn = 1536
    return pl.pallas_call(
        _ada_kernel,
        grid=(depth, n6 // tn),
        in_specs=[
            pl.BlockSpec((8, d), lambda l, j: (0, 0)),
            pl.BlockSpec((1, d, tn), lambda l, j: (l, 0, j)),
            pl.BlockSpec((1, 1, tn), lambda l, j: (l, 0, j)),
        ],
        out_specs=pl.BlockSpec((1, 8, tn), lambda l, j: (l, 0, j)),
        out_shape=jax.ShapeDtypeStruct((depth, 8, n6), F32),
        compiler_params=_params(ARB2, _nbytes((d + 2 * 8, tn), F32)),
        name="adaln",
    )(cond, ada_w, ada_b.reshape(depth, 1, n6))


NSUB = 3


def _mod_spec(chunk, lat_only=False):
    if lat_only:
        return pl.BlockSpec((1, 1, D_MODEL), lambda b, i: (2 * b + 1, 0, chunk))
    return pl.BlockSpec((2, 1, D_MODEL), lambda b, i: (b, 0, chunk))


def _pick_mod(ref, is_ctx):
    m = ref[...]
    return jnp.where(is_ctx, m[0], m[1])


def _is_ctx(k):
    return NSUB * pl.program_id(1) + k == 0


def _const_spec(shape):
    nd = len(shape)
    return pl.BlockSpec(shape, lambda *_: (0,) * nd, pipeline_mode=pl.Buffered(1))


QA_W = A_HEADS * HEAD_DIM
KVA_W = A_KV_HEADS * HEAD_DIM
QKVB_W = B_HEADS * HEAD_DIM
EVEN_ROPED = QA_W + KVA_W
EVEN_QB = EVEN_ROPED + KVA_W
EVEN_VB = EVEN_QB + 2 * QKVB_W
EVEN_COLS = EVEN_VB + QKVB_W
EVEN_NORMED = EVEN_ROPED + 2 * QKVB_W


def _joint_specs(d):
    lat = lambda k: pl.BlockSpec((1, TM, d), lambda bb, i: (bb, jnp.maximum(NSUB * i + k - 1, 0), 0))
    return [pl.BlockSpec((1, CTX_LEN, d), lambda bb, i: (bb, 0, 0))] + [lat(k) for k in range(NSUB)]


def _joint_tile(ctx_ref, x_refs, k):
    return jnp.where(_is_ctx(k), ctx_ref[0], x_refs[k][0])


def _even_proj_kernel(ctx_ref, *refs):
    x_refs = refs[:NSUB]
    (sh_ref, sc_ref, g_ref, w_ref, gains_ref, e_ref, cos_ref, sin_ref,
     qat_ref, ka_ref, qb_ref, kb_ref, vat_ref, vb_ref) = refs[NSUB:]
    for k in range(NSUB):
        rows = slice(k * TM, (k + 1) * TM)
        h = _norm_mod(_joint_tile(ctx_ref, x_refs, k), g_ref[...],
                      _pick_mod(sh_ref, _is_ctx(k)), _pick_mod(sc_ref, _is_ctx(k))).astype(BF16)
        y = _dot(h, w_ref[...])
        gains = gains_ref[...]
        ya = _group_rms(y[:, :EVEN_ROPED], e_ref[...], HEAD_DIM) * gains[:, :EVEN_ROPED]
        r = _rope(ya, cos_ref[rows, :], sin_ref[rows, :], HEAD_DIM // 4)
        yb = _group_rms(y[:, EVEN_QB:EVEN_VB], e_ref[...], HEAD_DIM) * gains[:, EVEN_ROPED:]
        qat_ref[0, :, rows] = r[:, 0:QA_W].T.astype(BF16)
        ka_ref[0, rows, :] = r[:, QA_W:].astype(BF16)
        qb_ref[0, :, rows] = yb[:, :QKVB_W].T.astype(BF16)
        kb_ref[0, rows, :] = yb[:, QKVB_W:].astype(BF16)
        vat_ref[0, :, rows] = y[:, EVEN_ROPED:EVEN_QB].T.astype(BF16)
        vb_ref[0, :, rows] = y[:, EVEN_VB:].T.astype(BF16)


def _even_proj(ctx, x, mods, g, w, gains, e64, cos, sin):
    b, s, d = x.shape
    n = CTX_LEN + s
    tmb = NSUB * TM
    assert n % tmb == 0
    tok = lambda wd: (pl.BlockSpec((1, tmb, wd), lambda bb, i: (bb, i, 0)),
                      jax.ShapeDtypeStruct((b, n, wd), BF16))
    tr = lambda wd: (pl.BlockSpec((1, wd, tmb), lambda bb, i: (bb, 0, i)),
                     jax.ShapeDtypeStruct((b, wd, n), BF16))
    outs = (tr(QA_W), tok(KVA_W), tr(QKVB_W), tok(QKVB_W), tr(KVA_W), tr(QKVB_W))
    return pl.pallas_call(
        _even_proj_kernel,
        grid=(b, n // tmb),
        in_specs=_joint_specs(d) + [
            _mod_spec(0), _mod_spec(1),
            _const_spec((1, d)),
            _const_spec((d, EVEN_COLS)),
            _const_spec((1, EVEN_NORMED)),
            _const_spec((MXU_DIM, MXU_DIM)),
            pl.BlockSpec((tmb, LANES), lambda bb, i: (i, 0)),
            pl.BlockSpec((tmb, LANES), lambda bb, i: (i, 0)),
        ],
        out_specs=[o[0] for o in outs],
        out_shape=[o[1] for o in outs],
        compiler_params=_params(ARB2, _nbytes((CTX_LEN + tmb, d), F32) + _nbytes((d, EVEN_COLS), BF16, 1)
                                + _nbytes((tmb, EVEN_COLS), BF16) + 2 * _nbytes((tmb, LANES), F32)),
        name="even_proj",
    )(ctx, *([x] * NSUB), mods, mods, g, w, gains, e64, cos, sin)


GQA_R = A_HEADS // A_KV_HEADS
WIN_BAND = 3 * A_BLOCK
WIN_LANES = GQA_R * A_BLOCK


PIPE_UNROLL = 2


def _window_kernel(qt_ref, k_ref, vt_ref, sink_ref, wm_ref, o_ref, s_ref, mx_ref, qs_ref, *, seq):
    g = pl.program_id(1)
    nb = seq // A_BLOCK
    nkeys = WIN_BAND + CTX_LEN
    assert nb % PIPE_UNROLL == 0 and PIPE_UNROLL % 2 == 0
    sink = sink_ref[0]
    ones_loc = jnp.ones((SUM_ROWS, nkeys), BF16)
    ones_ctx = jnp.ones((SUM_ROWS, CTX_LEN), BF16)

    def prep_q(col0, qslot):
        qt = qt_ref[0, :, pl.ds(col0, A_BLOCK)]
        zero = jnp.zeros((HEAD_DIM, A_BLOCK), BF16)
        for h in range(GQA_R):
            qh = qt[h * HEAD_DIM:(h + 1) * HEAD_DIM]
            lanes = slice(h * A_BLOCK, (h + 1) * A_BLOCK)
            qs_ref[qslot, 0:HEAD_DIM, lanes] = jnp.where(g == 0, qh, zero)
            qs_ref[qslot, HEAD_DIM:, lanes] = jnp.where(g == 0, zero, qh)

    def band_start(n):
        return pl.multiple_of(jnp.clip((n - 1) * A_BLOCK, 0, seq - WIN_BAND), A_BLOCK)

    def scores(qslot, n, sslot):
        start = band_start(n)
        keys = jnp.concatenate([k_ref[0, pl.ds(CTX_LEN + start, WIN_BAND), :], k_ref[0, 0:CTX_LEN, :]], axis=0)
        st = _dot(keys, qs_ref[qslot])
        mask = wm_ref[n - start // A_BLOCK]
        s_loc = st[:WIN_BAND] + jnp.concatenate([mask] * GQA_R, axis=1)
        s_ctx = st[WIN_BAND:]
        s_ref[sslot, 0:WIN_BAND] = s_loc
        s_ref[sslot, WIN_BAND:] = s_ctx
        mx_ref[sslot] = jnp.maximum(jnp.maximum(jnp.max(s_loc, axis=0, keepdims=True),
                                                jnp.max(s_ctx, axis=0, keepdims=True)), sink)

    def write(res, m, col0):
        o = res[0:HEAD_DIM] / (res[HEAD_DIM:HEAD_DIM + 1] + jnp.exp2(sink - m))
        for h in range(GQA_R):
            o_ref[0, h * HEAD_DIM:(h + 1) * HEAD_DIM, pl.ds(col0, A_BLOCK)] = (
                o[:, h * A_BLOCK:(h + 1) * A_BLOCK].astype(o_ref.dtype))

    def update(n, sslot):
        start = band_start(n)
        m = mx_ref[sslot]
        pt = jnp.exp2(s_ref[sslot] - m).astype(BF16)
        va = jnp.concatenate([vt_ref[0, :, pl.ds(CTX_LEN + start, WIN_BAND)], vt_ref[0, :, 0:CTX_LEN]], axis=1)
        res = _dot(jnp.concatenate([va, ones_loc], axis=0), pt)
        write(res, m, pl.multiple_of(CTX_LEN + n * A_BLOCK, A_BLOCK))

    for blk in range(CTX_LEN // A_BLOCK):
        prep_q(blk * A_BLOCK, 0)
        st = _dot(k_ref[0, 0:CTX_LEN, :], qs_ref[0])
        m = jnp.maximum(jnp.max(st, axis=0, keepdims=True), sink)
        pt = jnp.exp2(st - m).astype(BF16)
        res = _dot(jnp.concatenate([vt_ref[0, :, 0:CTX_LEN], ones_ctx], axis=0), pt)
        write(res, m, blk * A_BLOCK)

    def some_blocks(t, carry):
        for u in range(PIPE_UNROLL):
            nxt = jnp.minimum(PIPE_UNROLL * t + u + 1, nb - 1)
            prep_q(pl.multiple_of(CTX_LEN + nxt * A_BLOCK, A_BLOCK), (u + 1) % 2)
            scores((u + 1) % 2, nxt, (u + 1) % 2)
            update(PIPE_UNROLL * t + u, u % 2)
        return carry

    prep_q(CTX_LEN, 0)
    scores(0, 0, 0)
    lax.fori_loop(0, nb // PIPE_UNROLL, some_blocks, 0)


def _window_mask():
    o = jnp.arange(3)[:, None, None]
    key = jnp.arange(WIN_BAND)[None, :, None]
    qry = jnp.arange(A_BLOCK)[None, None, :]
    return jnp.where(jnp.abs(qry + o * A_BLOCK - key) <= A_WINDOW, 0.0, NEG_INF).astype(F32)


def _window_attn(qat, ka, vat, sink_row):
    b, _, n = qat.shape
    return pl.pallas_call(
        functools.partial(_window_kernel, seq=n - CTX_LEN),
        grid=(b, A_KV_HEADS),
        scratch_shapes=[pltpu.VMEM((2, WIN_BAND + CTX_LEN, WIN_LANES), F32),
                        pltpu.VMEM((2, 1, WIN_LANES), F32),
                        pltpu.VMEM((2, 2 * HEAD_DIM, WIN_LANES), BF16)],
        in_specs=[
            pl.BlockSpec((1, GQA_R * HEAD_DIM, n), lambda bb, g: (bb, g, 0)),
            pl.BlockSpec((1, n, KVA_W), lambda bb, g: (bb, 0, 0)),
            pl.BlockSpec((1, HEAD_DIM, n), lambda bb, g: (bb, g, 0)),
            pl.BlockSpec((1, 1, WIN_LANES), lambda bb, g: (g, 0, 0)),
            _const_spec((3, WIN_BAND, A_BLOCK)),
        ],
        out_specs=pl.BlockSpec((1, GQA_R * HEAD_DIM, n), lambda bb, g: (bb, g, 0)),
        out_shape=jax.ShapeDtypeStruct((b, QA_W, n), BF16),
        compiler_params=_params(ARB2, _nbytes((2 * GQA_R * HEAD_DIM + KVA_W + HEAD_DIM, n), BF16)
                                + _nbytes((WIN_BAND + CTX_LEN + 2 * HEAD_DIM, WIN_LANES), F32)
                                + _nbytes((3, WIN_BAND, A_BLOCK), F32, 1)),
        name="window_attn",
    )(qat, ka, vat, sink_row, _window_mask())


NA_ITEM_ROWS = LANES // GRID_W
NA_WIN_ROWS = NA_ROWS + NA_ITEM_ROWS
NA_HG = 4
NA_W = NA_HG * HEAD_DIM
NA_LANES = NA_HG * LANES
NA_BIAS_N = 2 * NA_ROWS


def _na_kernel(qt_ref, k_ref, vt_ref, ta_ref, tb_ref, o_ref, s_ref, mx_ref, qs_ref, *, rows_n):
    nitems = rows_n // NA_ITEM_ROWS
    nloc = NA_WIN_ROWS * GRID_W
    assert nitems % PIPE_UNROLL == 0 and (rows_n - NA_WIN_ROWS) % 2 == 0 and nloc % LANES == 0
    ones_loc = jnp.ones((SUM_ROWS, nloc + CTX_LEN), BF16)
    ones_ctx = jnp.ones((SUM_ROWS, CTX_LEN), BF16)

    def prep_q(col0, qslot):
        qt = qt_ref[0, :, pl.ds(col0, LANES)]
        for h in range(NA_HG):
            rows = slice(h * HEAD_DIM, (h + 1) * HEAD_DIM)
            qs_ref[qslot, rows, h * LANES:(h + 1) * LANES] = qt[rows]

    def win_start(j):
        return jnp.clip(NA_ITEM_ROWS * j - NA_ROWS // 2, 0, rows_n - NA_WIN_ROWS)

    def scores(qslot, j, sslot):
        ru = win_start(j)
        k0 = pl.multiple_of(CTX_LEN + ru * GRID_W, LANES)
        keys = jnp.concatenate([k_ref[0, pl.ds(k0, nloc), :], k_ref[0, 0:CTX_LEN, :]], axis=0)
        st = _dot(keys, qs_ref[qslot])
        mx = jnp.max(st[nloc:], axis=0, keepdims=True)
        s_ref[sslot, nloc:] = st[nloc:]
        for i in range(NA_WIN_ROWS):
            idx = []
            for e in range(NA_ITEM_ROWS):
                r = NA_ITEM_ROWS * j + e
                r0 = jnp.clip(r - NA_ROWS // 2, 0, rows_n - NA_ROWS)
                seen = (ru + i >= r0) & (ru + i < r0 + NA_ROWS)
                idx.append(jnp.where(seen, ru + i - r + NA_ROWS - 1, NA_BIAS_N - 1))
            rows = slice(i * GRID_W, (i + 1) * GRID_W)
            bias = ta_ref[0, idx[0]] + tb_ref[0, idx[1]]
            blk = st[rows] + jnp.concatenate([bias[h] for h in range(NA_HG)], axis=1)
            s_ref[sslot, rows] = blk
            mx = jnp.maximum(mx, jnp.max(blk, axis=0, keepdims=True))
        mx_ref[sslot] = mx

    def pv_write(va, ones, pt, col0):
        for pair in range(NA_HG // 2):
            rows = slice(pair * LANES, (pair + 1) * LANES)
            res = _dot(jnp.concatenate([va[rows], ones], axis=0),
                       pt[:, pair * 2 * LANES:(pair + 1) * 2 * LANES])
            for hh in range(2):
                lanes = slice(hh * LANES, (hh + 1) * LANES)
                o = res[hh * HEAD_DIM:(hh + 1) * HEAD_DIM, lanes] / res[LANES:LANES + 1, lanes]
                h = 2 * pair + hh
                o_ref[0, h * HEAD_DIM:(h + 1) * HEAD_DIM, pl.ds(col0, LANES)] = o.astype(o_ref.dtype)

    def update(j, sslot):
        k0 = pl.multiple_of(CTX_LEN + win_start(j) * GRID_W, LANES)
        pt = jnp.exp2(s_ref[sslot] - mx_ref[sslot]).astype(BF16)
        va = jnp.concatenate([vt_ref[0, :, pl.ds(k0, nloc)], vt_ref[0, :, 0:CTX_LEN]], axis=1)
        pv_write(va, ones_loc, pt, pl.multiple_of(CTX_LEN + j * LANES, LANES))

    qs_ref[...] = jnp.zeros(qs_ref.shape, BF16)
    for blk in range(CTX_LEN // LANES):
        prep_q(blk * LANES, 0)
        st = _dot(k_ref[0, 0:CTX_LEN, :], qs_ref[0])
        pt = jnp.exp2(st - jnp.max(st, axis=0, keepdims=True)).astype(BF16)
        pv_write(vt_ref[0, :, 0:CTX_LEN], ones_ctx, pt, blk * LANES)

    def some_items(t, carry):
        for u in range(PIPE_UNROLL):
            nxt = jnp.minimum(PIPE_UNROLL * t + u + 1, nitems - 1)
            prep_q(pl.multiple_of(CTX_LEN + nxt * LANES, LANES), (u + 1) % 2)
            scores((u + 1) % 2, nxt, (u + 1) % 2)
            update(PIPE_UNROLL * t + u, u % 2)
        return carry

    prep_q(CTX_LEN, 0)
    scores(0, 0, 0)
    lax.fori_loop(0, nitems // PIPE_UNROLL, some_items, 0)


def _na_attn(qbt, kb, vbt, ta, tb):
    b, _, n = qbt.shape
    rows_n = (n - CTX_LEN) // GRID_W
    assert rows_n >= NA_WIN_ROWS
    nkeys = NA_WIN_ROWS * GRID_W + CTX_LEN
    tspec = pl.BlockSpec((1, NA_BIAS_N, NA_HG, GRID_W, LANES), lambda bb, hg: (hg, 0, 0, 0, 0))
    return pl.pallas_call(
        functools.partial(_na_kernel, rows_n=rows_n),
        grid=(b, B_HEADS // NA_HG),
        scratch_shapes=[pltpu.VMEM((2, nkeys, NA_LANES), F32),
                        pltpu.VMEM((2, 1, NA_LANES), F32),
                        pltpu.VMEM((2, NA_W, NA_LANES), BF16)],
        in_specs=[
            pl.BlockSpec((1, NA_W, n), lambda bb, hg: (bb, hg, 0)),
            pl.BlockSpec((1, n, NA_W), lambda bb, hg: (bb, 0, hg)),
            pl.BlockSpec((1, NA_W, n), lambda bb, hg: (bb, hg, 0)),
            tspec, tspec,
        ],
        out_specs=pl.BlockSpec((1, NA_W, n), lambda bb, hg: (bb, hg, 0)),
        out_shape=jax.ShapeDtypeStruct((b, QKVB_W, n), BF16),
        compiler_params=_params(ARB2, _nbytes((4 * NA_W, n), BF16) + _nbytes((nkeys + NA_W, NA_LANES), F32)
                                + 2 * _nbytes((NA_BIAS_N, NA_HG, GRID_W, LANES), F32)),
        name="na_attn",
    )(qbt, kb, vbt, ta, tb)


def _mlp_tail(xs, y, mods, gn, w1_ref, w2_ref):
    x1s = [x + m[0] * y[k * TM:(k + 1) * TM] for k, (x, m) in enumerate(zip(xs, mods))]
    h = jnp.concatenate([_norm_mod(x1, gn, m[1], m[2]).astype(BF16) for x1, m in zip(x1s, mods)], axis=0)
    mlp = None
    for j in range(D_FF // D_MODEL):
        cols = slice(j * D_MODEL, (j + 1) * D_MODEL)
        u = jnp.maximum(_dot(h, w1_ref[:, cols]), 0.0)
        part = _dot((u * u).astype(BF16), w2_ref[cols, :])
        mlp = part if mlp is None else mlp + part
    return [x1 + m[3] * mlp[k * TM:(k + 1) * TM] for k, (x1, m) in enumerate(zip(x1s, mods))]


def _even_out_kernel(ctx_ref, *refs):
    x_refs = refs[:NSUB]
    (oat_ref, obt_ref, g1_ref, sh2_ref, sc2_ref, g2_ref, gn_ref,
     woa_ref, wob_ref, w1_ref, w2_ref, o_ref) = refs[NSUB:]
    oa = oat_ref[0].astype(F32).T.astype(BF16)
    ob = obt_ref[0].astype(F32).T.astype(BF16)
    y = _dot(oa, woa_ref[...]) + _dot(ob, wob_ref[...])
    xs = [_joint_tile(ctx_ref, x_refs, k) for k in range(NSUB)]
    mods = [[_pick_mod(r, _is_ctx(k)) for r in (g1_ref, sh2_ref, sc2_ref, g2_ref)] for k in range(NSUB)]
    for k, out in enumerate(_mlp_tail(xs, y, mods, gn_ref[...], w1_ref, w2_ref)):
        o_ref[0, k * TM:(k + 1) * TM, :] = out


def _even_out(ctx, x, oa, ob, mods, gn, woa, wob, w1, w2):
    b, s, d = x.shape
    n = CTX_LEN + s
    tmb = NSUB * TM
    return pl.pallas_call(
        _even_out_kernel,
        grid=(b, n // tmb),
        in_specs=_joint_specs(d) + [
            pl.BlockSpec((1, QA_W, tmb), lambda bb, i: (bb, 0, i)),
            pl.BlockSpec((1, QKVB_W, tmb), lambda bb, i: (bb, 0, i)),
            _mod_spec(2), _mod_spec(3), _mod_spec(4), _mod_spec(5),
            _const_spec((1, d)),
            _const_spec((QA_W, d)), _const_spec((QKVB_W, d)),
            _const_spec((d, D_FF)), _const_spec((D_FF, d)),
        ],
        out_specs=pl.BlockSpec((1, tmb, d), lambda bb, i: (bb, i, 0)),
        out_shape=jax.ShapeDtypeStruct((b, n, d), F32),
        compiler_params=_params(ARB2, _nbytes((CTX_LEN + 2 * tmb, d), F32) + _nbytes((QA_W + QKVB_W, tmb), BF16)
                                + _nbytes((d + 2 * D_FF, d), BF16, 1)),
        name="even_out_mlp",
    )(ctx, *([x] * NSUB), oa, ob, mods, mods, mods, mods, gn, woa, wob, w1, w2)


NSUB_ODD = 2


def _odd_out_kernel(*refs):
    x_refs = refs[:NSUB_ODD]
    ot_ref, g1_ref, sh2_ref, sc2_ref, g2_ref, gn_ref, wo_ref, w1_ref, w2_ref, o_ref = refs[NSUB_ODD:]
    o = ot_ref[0].astype(F32).T.astype(BF16)
    y = _dot(o, wo_ref[...])
    mods = [[g1_ref[0], sh2_ref[0], sc2_ref[0], g2_ref[0]]] * NSUB_ODD
    for k, out in enumerate(_mlp_tail([r[0] for r in x_refs], y, mods, gn_ref[...], w1_ref, w2_ref)):
        o_ref[0, k * TM:(k + 1) * TM, :] = out


def _odd_out(xa, ot, mods, gn, wo, w1, w2):
    b, n, d = xa.shape
    s = n - CTX_LEN
    tmb = NSUB_ODD * TM
    assert s % tmb == 0
    lat = lambda k: pl.BlockSpec((1, TM, d), lambda bb, i: (bb, NSUB_ODD * i + k + 1, 0))
    return pl.pallas_call(
        _odd_out_kernel,
        grid=(b, s // tmb),
        in_specs=[lat(k) for k in range(NSUB_ODD)] + [
            pl.BlockSpec((1, d, tmb), lambda bb, i: (bb, 0, i)),
            _mod_spec(2, True), _mod_spec(3, True), _mod_spec(4, True), _mod_spec(5, True),
            _const_spec((1, d)),
            _const_spec((d, d)),
            _const_spec((d, D_FF)), _const_spec((D_FF, d)),
        ],
        out_specs=pl.BlockSpec((1, tmb, d), lambda bb, i: (bb, i, 0)),
        out_shape=jax.ShapeDtypeStruct((b, s, d), F32),
        compiler_params=_params(ARB2, _nbytes((2 * tmb, d), F32) + _nbytes((d, tmb), BF16)
                                + _nbytes((d + 2 * D_FF, d), BF16, 1)),
        name="odd_out_mlp",
    )(*([xa] * NSUB_ODD), ot, mods, mods, mods, mods, gn, wo, w1, w2)


ODD_IN_COLS = C_Q_RANK + C_KV_RANK + LANES
Q_NOPE_W = C_HEADS * C_NOPE
Q_ROPE_W = C_HEADS * C_ROPE


def _odd_proj_kernel(x_ref, sh_ref, sc_ref, g_ref, win_ref, qag_ref, kvg_ref, wuq_ref, wukv_ref,
                     gq_ref, gk_ref, e64_ref, e32_ref, cos_ref, sin_ref,
                     qn_ref, qr_ref, kn_ref, kr_ref, vt_ref):
    def rms(t, g):
        return (t * lax.rsqrt(jnp.mean(t * t, axis=-1, keepdims=True) + NORM_EPS) * g).astype(BF16)

    gq = gq_ref[...]
    gk = gk_ref[...]
    for k in range(NSUB):
        rows = slice(k * TM, (k + 1) * TM)
        h = _norm_mod(x_ref[0, rows, :], g_ref[...],
                      _pick_mod(sh_ref, _is_ctx(k)), _pick_mod(sc_ref, _is_ctx(k))).astype(BF16)
        y = _dot(h, win_ref[...])
        cq = y[:, :C_Q_RANK]
        ckv = y[:, C_Q_RANK:C_Q_RANK + C_KV_RANK]
        kr = y[:, C_Q_RANK + C_KV_RANK:]
        q = _dot(rms(cq, qag_ref[...]), wuq_ref[...])
        kv = _dot(rms(ckv, kvg_ref[...]), wukv_ref[...])
        cos = cos_ref[rows, :]
        sin = sin_ref[rows, :]
        qn = _group_rms(q[:, :Q_NOPE_W], e64_ref[...], C_NOPE) * gq[:, :Q_NOPE_W]
        qn_ref[0, :, rows] = qn.T.astype(BF16)
        qr = _group_rms(q[:, Q_NOPE_W:], e32_ref[...], C_ROPE) * gq[:, Q_NOPE_W:]
        qr_ref[0, :, rows] = _rope(qr, cos, sin, C_ROPE // 4).T.astype(BF16)
        kn = _group_rms(kv[:, :Q_NOPE_W], e64_ref[...], C_NOPE) * gk[:, :Q_NOPE_W]
        kn_ref[0, rows, :] = kn.astype(BF16)
        krn = _group_rms(kr, e32_ref[...], C_ROPE) * gk[:, Q_NOPE_W:]
        kr_ref[0, rows, :] = _rope(krn, cos, sin, C_ROPE // 4).astype(BF16)
        vt_ref[0, :, rows] = kv[:, Q_NOPE_W:].T.astype(BF16)


def _odd_proj(xa, mods, g, win, qag, kvg, wuq, wukv, gq, gk, e64, e32, cos, sin):
    b, n, d = xa.shape
    tmb = NSUB * TM
    assert n % tmb == 0
    tok = lambda wd: pl.BlockSpec((1, tmb, wd), lambda bb, i: (bb, i, 0))
    return pl.pallas_call(
        _odd_proj_kernel,
        grid=(b, n // tmb),
        in_specs=[
            tok(d), _mod_spec(0), _mod_spec(1),
            _const_spec((1, d)),
            _const_spec((d, ODD_IN_COLS)),
            _const_spec((1, C_Q_RANK)), _const_spec((1, C_KV_RANK)),
            _const_spec((C_Q_RANK, Q_NOPE_W + Q_ROPE_W)),
            _const_spec((C_KV_RANK, 2 * Q_NOPE_W)),
            _const_spec((1, Q_NOPE_W + Q_ROPE_W)), _const_spec((1, Q_NOPE_W + LANES)),
            _const_spec((MXU_DIM, MXU_DIM)), _const_spec((MXU_DIM, MXU_DIM)),
            pl.BlockSpec((tmb, LANES), lambda bb, i: (i, 0)),
            pl.BlockSpec((tmb, LANES), lambda bb, i: (i, 0)),
        ],
        out_specs=[pl.BlockSpec((1, Q_NOPE_W, tmb), lambda bb, i: (bb, 0, i)),
                   pl.BlockSpec((1, Q_ROPE_W, tmb), lambda bb, i: (bb, 0, i)),
                   tok(Q_NOPE_W), tok(LANES),
                   pl.BlockSpec((1, Q_NOPE_W, tmb), lambda bb, i: (bb, 0, i))],
        out_shape=[jax.ShapeDtypeStruct((b, Q_NOPE_W, n), BF16),
                   jax.ShapeDtypeStruct((b, Q_ROPE_W, n), BF16),
                   jax.ShapeDtypeStruct((b, n, Q_NOPE_W), BF16),
                   jax.ShapeDtypeStruct((b, n, LANES), BF16),
                   jax.ShapeDtypeStruct((b, Q_NOPE_W, n), BF16)],
        compiler_params=_params(ARB2, _nbytes((tmb, d), F32) + 2 * _nbytes((tmb, LANES), F32)
                                + _nbytes((tmb, 3 * Q_NOPE_W + Q_ROPE_W + LANES), BF16)
                                + _nbytes((d + C_Q_RANK + C_KV_RANK, 2 * Q_NOPE_W), BF16, 1)),
        name="odd_proj",
    )(xa, mods, mods, g, win, qag, kvg, wuq, wukv, gq, gk, e64, e32, cos, sin)


SUM_ROWS = 16
FLASH_TILES_PER_TRIP = 8


def _kv_tile(n):
    for t in (768, 256):
        if n % t == 0 and (n // t) % 2 == 1:
            return t
    raise ValueError(f"joint sequence length {n} has no odd split into 256-multiples")


def _flash_kernel(qnt_ref, qrt_ref, kn_ref, kr_ref, vt_ref, o_ref,
                  s_ref, mx_ref, m_ref, acc_ref, qt_ref, *, tk, tq):
    p = pl.program_id(1)
    n = kn_ref.shape[1]
    nq = (n - CTX_LEN) // tq
    nch = n // tk
    assert nq % 2 == 0 and nch % 2 == 1
    ones = jnp.ones((SUM_ROWS, tk), BF16)
    row = lax.broadcasted_iota(jnp.int32, (LANES, tq), 0)

    def prep_q(qi, qslot):
        c0 = pl.multiple_of(CTX_LEN + qi * tq, tq)
        qn = qnt_ref[0, :, pl.ds(c0, tq)]
        qr = qrt_ref[0, :, pl.ds(c0, tq)]
        zero = jnp.zeros_like(qn)
        for hh in range(2):
            off = ((2 * p + hh) % 4) * C_ROPE
            keep_n = (row < C_NOPE) if hh == 0 else (row >= C_NOPE)
            keep_r = (row >= off) & (row < off + C_ROPE)
            qt_ref[qslot, 0:LANES, hh * tq:(hh + 1) * tq] = jnp.where(keep_n, qn, zero)
            qt_ref[qslot, LANES:, hh * tq:(hh + 1) * tq] = jnp.where(keep_r, qr, zero)

    def scores(qslot, j, sslot):
        k0 = j * tk
        kcat = jnp.concatenate([kn_ref[0, pl.ds(k0, tk), :], kr_ref[0, pl.ds(k0, tk), :]], axis=1)
        st = _dot(kcat, qt_ref[qslot])
        s_ref[sslot] = st
        mx_ref[sslot] = jnp.max(st, axis=0, keepdims=True)

    def update(j, sslot):
        k0 = j * tk
        m_old = m_ref[...]
        m_new = jnp.maximum(m_old, mx_ref[sslot])
        alpha = jnp.exp2(m_old - m_new)
        pt = jnp.exp2(s_ref[sslot] - m_new).astype(BF16)
        va = jnp.concatenate([vt_ref[0, :, pl.ds(k0, tk)], ones], axis=0)
        m_ref[...] = m_new
        acc_ref[...] = alpha * acc_ref[...] + _dot(va, pt)

    def reset():
        m_ref[...] = jnp.full(m_ref.shape, NEG_INF, F32)
        acc_ref[...] = jnp.zeros(acc_ref.shape, F32)

    def finish(qi):
        acc = acc_ref[...]
        c0 = pl.multiple_of(qi * tq, tq)
        o_ref[0, 0:C_NOPE, pl.ds(c0, tq)] = (acc[0:C_NOPE, :tq] / acc[LANES:LANES + 1, :tq]).astype(o_ref.dtype)
        o_ref[0, C_NOPE:, pl.ds(c0, tq)] = (acc[C_NOPE:LANES, tq:] / acc[LANES:LANES + 1, tq:]).astype(o_ref.dtype)
        reset()

    def run_tile(qi, slot, next_qi):
        other = 1 - slot
        for j in range(nch - 1):
            scores(slot, j + 1, other if j % 2 == 0 else slot)
            update(j, slot if j % 2 == 0 else other)
        prep_q(next_qi, other)
        scores(other, 0, other)
        update(nch - 1, slot)
        finish(qi)

    def some_tiles(t, carry):
        for u in range(FLASH_TILES_PER_TRIP):
            qi = FLASH_TILES_PER_TRIP * t + u
            run_tile(qi, u % 2, jnp.minimum(qi + 1, nq - 1))
        return carry

    assert nq % FLASH_TILES_PER_TRIP == 0 and FLASH_TILES_PER_TRIP % 2 == 0
    reset()
    prep_q(0, 0)
    scores(0, 0, 0)
    lax.fori_loop(0, nq // FLASH_TILES_PER_TRIP, some_tiles, 0)


def _flash_attn(qnt, qrt, kn, kr, vt):
    b, n, _ = kn.shape
    s = n - CTX_LEN
    tq = TM
    tk = _kv_tile(n)
    return pl.pallas_call(
        functools.partial(_flash_kernel, tk=tk, tq=tq),
        grid=(b, C_HEADS // 2),
        scratch_shapes=[pltpu.VMEM((2, tk, 2 * tq), F32),
                        pltpu.VMEM((2, 1, 2 * tq), F32),
                        pltpu.VMEM((1, 2 * tq), F32),
                        pltpu.VMEM((LANES + SUM_ROWS, 2 * tq), F32),
                        pltpu.VMEM((2, 2 * LANES, 2 * tq), BF16)],
        in_specs=[
            pl.BlockSpec((1, LANES, n), lambda bb, p: (bb, p, 0)),
            pl.BlockSpec((1, LANES, n), lambda bb, p: (bb, p // 2, 0)),
            pl.BlockSpec((1, n, LANES), lambda bb, p: (bb, 0, p)),
            pl.BlockSpec((1, n, LANES), lambda bb, p: (bb, 0, 0)),
            pl.BlockSpec((1, LANES, n), lambda bb, p: (bb, p, 0)),
        ],
        out_specs=pl.BlockSpec((1, LANES, s), lambda bb, p: (bb, p, 0)),
        out_shape=jax.ShapeDtypeStruct((b, C_HEADS * C_NOPE, s), BF16),
        compiler_params=_params(ARB2, _nbytes((6 * LANES, n), BF16)
                                + _nbytes((tk + LANES + SUM_ROWS + 2 * LANES, 2 * tq), F32)),
        name="mla_flash",
    )(qnt, qrt, kn, kr, vt)


def _rope_tables(seq, dim):
    t = np.arange(seq)
    pos = np.stack([t // GRID_W, t % GRID_W], axis=0).astype(np.float64)
    half = dim // 2
    q = half // 2
    inv = ROPE_THETA ** (-np.arange(q, dtype=np.float64) / q)
    j = np.arange(dim)
    ang = pos[j // half].T * inv[j % q][None, :]
    sign = np.where((j % half) < q, -1.0, 1.0)
    reps = LANES // dim
    cos = np.tile(np.cos(ang), (1, reps))
    sin = np.tile(np.sin(ang) * sign[None, :], (1, reps))
    cos = np.concatenate([np.ones((CTX_LEN, LANES)), cos], axis=0)
    sin = np.concatenate([np.zeros((CTX_LEN, LANES)), sin], axis=0)
    return jnp.asarray(cos, F32), jnp.asarray(sin, F32)


def _group_ones(group):
    r = jnp.arange(MXU_DIM) // group
    return (r[:, None] == r[None, :]).astype(BF16)


def _na_bias_tables(rpb, mult):
    cq = jnp.arange(GRID_W)
    c0 = jnp.clip(cq - NA_COLS // 2, 0, GRID_W - NA_COLS)
    col_ok = (cq[:, None] >= c0[None, :]) & (cq[:, None] < c0[None, :] + NA_COLS)
    dci = jnp.clip(cq[:, None] - cq[None, :], 1 - NA_COLS, NA_COLS - 1) + NA_COLS - 1
    pick = (dci[None] == jnp.arange(2 * NA_COLS - 1)[:, None, None]).astype(F32)
    tt = jnp.einsum("hdm,mkq->hdkq", rpb.astype(F32) * mult, pick, precision=lax.Precision.HIGHEST)
    tt = jnp.where(col_ok[None, None], tt, NEG_INF)
    tt = jnp.concatenate([tt, jnp.full_like(tt[:, :1], NEG_INF)], axis=1)
    zero = jnp.zeros_like(tt)

    def per_group(t):
        t = t.reshape(B_HEADS // NA_HG, NA_HG, NA_BIAS_N, GRID_W, LANES)
        return t.transpose(0, 2, 1, 3, 4)

    return per_group(jnp.concatenate([tt, zero], axis=-1)), per_group(jnp.concatenate([zero, tt], axis=-1))


def kernel(x, c, ctx, c_ctx, ada_w, ada_b, norm_mix, norm_mlp, mlp_w1, mlp_w2, e_w_in, e_w_out, a_q_norm, a_k_norm, a_sink, b_q_norm, b_k_norm, b_rpb, o_w_in, o_qa_norm, o_kva_norm, o_w_uq, o_w_ukv, o_qn_nope, o_qn_rope, o_kn_nope, o_kn_rope, o_w_out):
    bsz, seq, d = x.shape
    assert d == D_MODEL and ctx.shape[1] == CTX_LEN and seq % TM == 0 and ada_w.shape[0] == 2
    assert bsz + 1 <= 8

    cond = jnp.zeros((8, d), F32).at[:bsz].set(c).at[bsz].set(c_ctx)
    m = _adaln(cond, ada_w, ada_b)
    mods = [jnp.stack([jnp.broadcast_to(m[i, bsz], (bsz, 6 * d)), m[i, :bsz]], axis=1).reshape(2 * bsz, 1, 6 * d)
            for i in range(2)]

    e64 = _group_ones(HEAD_DIM)
    e32 = _group_ones(C_ROPE)

    w_ext = e_w_in[0].astype(BF16)
    scale = HEAD_DIM ** -0.5
    log2e = math.log2(math.e)
    gains = jnp.concatenate([jnp.tile(a_q_norm[0], A_HEADS) * (scale * log2e), jnp.tile(a_k_norm[0], A_KV_HEADS),
                             jnp.tile(b_q_norm[0], B_HEADS) * (scale * log2e), jnp.tile(b_k_norm[0], B_HEADS)])[None, :]
    cos64, sin64 = _rope_tables(seq, HEAD_DIM)
    qat, ka, qbt, kb, vat, vbt = _even_proj(ctx, x, mods[0], norm_mix[0][None, :], w_ext, gains, e64, cos64, sin64)
    sink_row = (jnp.repeat(a_sink[0].reshape(A_KV_HEADS, GQA_R), A_BLOCK, axis=1) * log2e)[:, None, :]
    oa = _window_attn(qat, ka, vat, sink_row)
    ob = _na_attn(qbt, kb, vbt, *_na_bias_tables(b_rpb[0], log2e))
    wo = e_w_out[0].astype(BF16)
    w1 = mlp_w1.astype(BF16)
    w2 = mlp_w2.astype(BF16)
    xa = _even_out(ctx, x, oa, ob, mods[0], norm_mlp[0][None, :], wo[:512], wo[512:], w1[0], w2[0])

    wi = o_w_in[0]
    win = jnp.concatenate([wi[:, :C_Q_RANK + C_KV_RANK]] + [wi[:, C_Q_RANK + C_KV_RANK:]] * 4, axis=1).astype(BF16)
    wuq = o_w_uq[0].reshape(C_Q_RANK, C_HEADS, C_NOPE + C_ROPE)
    wuq = jnp.concatenate([wuq[:, :, :C_NOPE].reshape(C_Q_RANK, -1), wuq[:, :, C_NOPE:].reshape(C_Q_RANK, -1)],
                          axis=1).astype(BF16)
    wukv = o_w_ukv[0].reshape(C_KV_RANK, C_HEADS, 2 * C_NOPE)
    wukv = jnp.concatenate([wukv[:, :, :C_NOPE].reshape(C_KV_RANK, -1), wukv[:, :, C_NOPE:].reshape(C_KV_RANK, -1)],
                           axis=1).astype(BF16)
    qscale = (C_NOPE + C_ROPE) ** -0.5 * math.log2(math.e)
    gq = (jnp.concatenate([jnp.tile(o_qn_nope[0], C_HEADS), jnp.tile(o_qn_rope[0], C_HEADS)]) * qscale)[None, :]
    gk = jnp.concatenate([jnp.tile(o_kn_nope[0], C_HEADS), jnp.tile(o_kn_rope[0], LANES // C_ROPE)])[None, :]
    cos32, sin32 = _rope_tables(seq, C_ROPE)
    qn, qr, kn, kr, vt = _odd_proj(xa, mods[1], norm_mix[1][None, :], win, o_qa_norm[0][None, :],
                                   o_kva_norm[0][None, :], wuq, wukv, gq, gk, e64, e32, cos32, sin32)
    ot = _flash_attn(qn, qr, kn, kr, vt)
    return _odd_out(xa, ot, mods[1], norm_mlp[1][None, :], o_w_out[0].astype(BF16), w1[1], w2[1])
```

```python
import functools
import math

import jax
import jax.numpy as jnp
import numpy as np
from jax import lax
from jax.experimental import pallas as pl
from jax.experimental.pallas import tpu as pltpu

F32 = jnp.float32
BF16 = jnp.bfloat16

D_MODEL = 1024
CTX_LEN = 256
GRID_W = 64
HEAD_DIM = 64
A_HEADS = 8
A_KV_HEADS = 2
A_WINDOW = 128
A_BLOCK = 128
B_HEADS = 8
NA_ROWS = 8
NA_COLS = 16
C_HEADS = 16
C_Q_RANK = 384
C_KV_RANK = 256
C_NOPE = 64
C_ROPE = 32
D_FF = 4 * D_MODEL
ROPE_THETA = 10000.0
NORM_EPS = 1e-6
NEG_INF = -1e30

LANES = 128
MXU_DIM = 256
TM = CTX_LEN
V7X_VMEM_BYTES = 64 * 1024 * 1024
VMEM_VALUE_BYTES = 32 * 1024 * 1024


ARB2 = ("arbitrary", "arbitrary")


def _dot(a, b):
    return jnp.dot(a, b, preferred_element_type=F32)


def _nbytes(shape, dtype, buffers=2):
    return math.prod(shape) * jnp.dtype(dtype).itemsize * buffers


def _params(sem, declared_bytes, value_bytes=VMEM_VALUE_BYTES):
    limit = min(declared_bytes + value_bytes, V7X_VMEM_BYTES * 7 // 8)
    return pltpu.CompilerParams(dimension_semantics=sem, vmem_limit_bytes=limit)


def _norm_mod(x, g, sh, sc):
    ms = jnp.mean(x * x, axis=-1, keepdims=True)
    return (x * lax.rsqrt(ms + NORM_EPS) * g) * (1.0 + sc) + sh


def _group_rms(y, e, group):
    out = []
    for c0 in range(0, y.shape[1], MXU_DIM):
        cw = min(MXU_DIM, y.shape[1] - c0)
        yc = y[:, c0:c0 + cw]
        ss = _dot((yc * yc).astype(BF16), e[:cw, :cw])
        out.append(yc * lax.rsqrt(ss * (1.0 / group) + NORM_EPS))
    return out[0] if len(out) == 1 else jnp.concatenate(out, axis=1)


def _rope(r, cos, sin, half):
    w = r.shape[1]
    reps = w // LANES
    if reps > 1:
        cos = jnp.concatenate([cos] * reps, axis=1)
        sin = jnp.concatenate([sin] * reps, axis=1)
    lane = lax.broadcasted_iota(jnp.int32, r.shape, 1)
    up = pltpu.roll(r, w - half, axis=1)
    dn = pltpu.roll(r, half, axis=1)
    sw = jnp.where((lane & half) == 0, up, dn)
    return r * cos + sw * sin


def _ada_kernel(cond_ref, w_ref, b_ref, o_ref):
    c = cond_ref[...]
    s = (c * jax.nn.sigmoid(c)).astype(BF16)
    o_ref[0] = _dot(s, w_ref[0].astype(BF16)) + b_ref[0]


def _adaln(cond, ada_w, ada_b):
    depth, d, n6 = ada_w.shape
    tn = 1536
    return pl.pallas_call(
        _ada_kernel,
        grid=(depth, n6 // tn),
        in_specs=[
            pl.BlockSpec((8, d), lambda l, j: (0, 0)),
            pl.BlockSpec((1, d, tn), lambda l, j: (l, 0, j)),
            pl.BlockSpec((1, 1, tn), lambda l, j: (l, 0, j)),
        ],
        out_specs=pl.BlockSpec((1, 8, tn), lambda l, j: (l, 0, j)),
        out_shape=jax.ShapeDtypeStruct((depth, 8, n6), F32),
        compiler_params=_params(ARB2, _nbytes((d + 2 * 8, tn), F32)),
        name="adaln",
    )(cond, ada_w, ada_b.reshape(depth, 1, n6))


NSUB = 3


def _mod_spec(chunk, lat_only=False):
    if lat_only:
        return pl.BlockSpec((1, 1, D_MODEL), lambda b, i: (2 * b + 1, 0, chunk))
    return pl.BlockSpec((2, 1, D_MODEL), lambda b, i: (b, 0, chunk))


def _pick_mod(ref, is_ctx):
    m = ref[...]
    return jnp.where(is_ctx, m[0], m[1])


def _is_ctx(k):
    return NSUB * pl.program_id(1) + k == 0


def _const_spec(shape):
    nd = len(shape)
    return pl.BlockSpec(shape, lambda *_: (0,) * nd, pipeline_mode=pl.Buffered(1))


QA_W = A_HEADS * HEAD_DIM
KVA_W = A_KV_HEADS * HEAD_DIM
QKVB_W = B_HEADS * HEAD_DIM
EVEN_ROPED = QA_W + KVA_W
EVEN_QB = EVEN_ROPED + KVA_W
EVEN_VB = EVEN_QB + 2 * QKVB_W
EVEN_COLS = EVEN_VB + QKVB_W
EVEN_NORMED = EVEN_ROPED + 2 * QKVB_W


def _joint_specs(d):
    lat = lambda k: pl.BlockSpec((1, TM, d), lambda bb, i: (bb, jnp.maximum(NSUB * i + k - 1, 0), 0))
    return [pl.BlockSpec((1, CTX_LEN, d), lambda bb, i: (bb, 0, 0))] + [lat(k) for k in range(NSUB)]


def _joint_tile(ctx_ref, x_refs, k):
    return jnp.where(_is_ctx(k), ctx_ref[0], x_refs[k][0])


def _even_proj_kernel(ctx_ref, *refs):
    x_refs = refs[:NSUB]
    (sh_ref, sc_ref, g_ref, w_ref, gains_ref, e_ref, cos_ref, sin_ref,
     qat_ref, ka_ref, qb_ref, kb_ref, vat_ref, vb_ref) = refs[NSUB:]
    for k in range(NSUB):
        rows = slice(k * TM, (k + 1) * TM)
        h = _norm_mod(_joint_tile(ctx_ref, x_refs, k), g_ref[...],
                      _pick_mod(sh_ref, _is_ctx(k)), _pick_mod(sc_ref, _is_ctx(k))).astype(BF16)
        y = _dot(h, w_ref[...])
        gains = gains_ref[...]
        ya = _group_rms(y[:, :EVEN_ROPED], e_ref[...], HEAD_DIM) * gains[:, :EVEN_ROPED]
        r = _rope(ya, cos_ref[rows, :], sin_ref[rows, :], HEAD_DIM // 4)
        yb = _group_rms(y[:, EVEN_QB:EVEN_VB], e_ref[...], HEAD_DIM) * gains[:, EVEN_ROPED:]
        qat_ref[0, :, rows] = r[:, 0:QA_W].T.astype(BF16)
        ka_ref[0, rows, :] = r[:, QA_W:].astype(BF16)
        qb_ref[0, :, rows] = yb[:, :QKVB_W].T.astype(BF16)
        kb_ref[0, rows, :] = yb[:, QKVB_W:].astype(BF16)
        vat_ref[0, :, rows] = y[:, EVEN_ROPED:EVEN_QB].T.astype(BF16)
        vb_ref[0, :, rows] = y[:, EVEN_VB:].T.astype(BF16)


def _even_proj(ctx, x, mods, g, w, gains, e64, cos, sin):
    b, s, d = x.shape
    n = CTX_LEN + s
    tmb = NSUB * TM
    assert n % tmb == 0
    tok = lambda wd: (pl.BlockSpec((1, tmb, wd), lambda bb, i: (bb, i, 0)),
                      jax.ShapeDtypeStruct((b, n, wd), BF16))
    tr = lambda wd: (pl.BlockSpec((1, wd, tmb), lambda bb, i: (bb, 0, i)),
                     jax.ShapeDtypeStruct((b, wd, n), BF16))
    outs = (tr(QA_W), tok(KVA_W), tr(QKVB_W), tok(QKVB_W), tr(KVA_W), tr(QKVB_W))
    return pl.pallas_call(
        _even_proj_kernel,
        grid=(b, n // tmb),
        in_specs=_joint_specs(d) + [
            _mod_spec(0), _mod_spec(1),
            _const_spec((1, d)),
            _const_spec((d, EVEN_COLS)),
            _const_spec((1, EVEN_NORMED)),
            _const_spec((MXU_DIM, MXU_DIM)),
            pl.BlockSpec((tmb, LANES), lambda bb, i: (i, 0)),
            pl.BlockSpec((tmb, LANES), lambda bb, i: (i, 0)),
        ],
        out_specs=[o[0] for o in outs],
        out_shape=[o[1] for o in outs],
        compiler_params=_params(ARB2, _nbytes((CTX_LEN + tmb, d), F32) + _nbytes((d, EVEN_COLS), BF16, 1)
                                + _nbytes((tmb, EVEN_COLS), BF16) + 2 * _nbytes((tmb, LANES), F32)),
        name="even_proj",
    )(ctx, *([x] * NSUB), mods, mods, g, w, gains, e64, cos, sin)


GQA_R = A_HEADS // A_KV_HEADS
WIN_BAND = 3 * A_BLOCK
WIN_LANES = GQA_R * A_BLOCK


PIPE_UNROLL = 2


def _window_kernel(qt_ref, k_ref, vt_ref, sink_ref, wm_ref, o_ref, s_ref, mx_ref, qs_ref, *, seq):
    g = pl.program_id(1)
    nb = seq // A_BLOCK
    nkeys = WIN_BAND + CTX_LEN
    assert nb % PIPE_UNROLL == 0 and PIPE_UNROLL % 2 == 0
    sink = sink_ref[0]
    ones_loc = jnp.ones((SUM_ROWS, nkeys), BF16)
    ones_ctx = jnp.ones((SUM_ROWS, CTX_LEN), BF16)

    def prep_q(col0, qslot):
        qt = qt_ref[0, :, pl.ds(col0, A_BLOCK)]
        zero = jnp.zeros((HEAD_DIM, A_BLOCK), BF16)
        for h in range(GQA_R):
            qh = qt[h * HEAD_DIM:(h + 1) * HEAD_DIM]
            lanes = slice(h * A_BLOCK, (h + 1) * A_BLOCK)
            qs_ref[qslot, 0:HEAD_DIM, lanes] = jnp.where(g == 0, qh, zero)
            qs_ref[qslot, HEAD_DIM:, lanes] = jnp.where(g == 0, zero, qh)

    def band_start(n):
        return pl.multiple_of(jnp.clip((n - 1) * A_BLOCK, 0, seq - WIN_BAND), A_BLOCK)

    def scores(qslot, n, sslot):
        start = band_start(n)
        keys = jnp.concatenate([k_ref[0, pl.ds(CTX_LEN + start, WIN_BAND), :], k_ref[0, 0:CTX_LEN, :]], axis=0)
        st = _dot(keys, qs_ref[qslot])
        mask = wm_ref[n - start // A_BLOCK]
        s_loc = st[:WIN_BAND] + jnp.concatenate([mask] * GQA_R, axis=1)
        s_ctx = st[WIN_BAND:]
        s_ref[sslot, 0:WIN_BAND] = s_loc
        s_ref[sslot, WIN_BAND:] = s_ctx
        mx_ref[sslot] = jnp.maximum(jnp.maximum(jnp.max(s_loc, axis=0, keepdims=True),
                                                jnp.max(s_ctx, axis=0, keepdims=True)), sink)

    def write(res, m, col0):
        o = res[0:HEAD_DIM] / (res[HEAD_DIM:HEAD_DIM + 1] + jnp.exp2(sink - m))
        for h in range(GQA_R):
            o_ref[0, h * HEAD_DIM:(h + 1) * HEAD_DIM, pl.ds(col0, A_BLOCK)] = (
                o[:, h * A_BLOCK:(h + 1) * A_BLOCK].astype(o_ref.dtype))

    def update(n, sslot):
        start = band_start(n)
        m = mx_ref[sslot]
        pt = jnp.exp2(s_ref[sslot] - m).astype(BF16)
        va = jnp.concatenate([vt_ref[0, :, pl.ds(CTX_LEN + start, WIN_BAND)], vt_ref[0, :, 0:CTX_LEN]], axis=1)
        res = _dot(jnp.concatenate([va, ones_loc], axis=0), pt)
        write(res, m, pl.multiple_of(CTX_LEN + n * A_BLOCK, A_BLOCK))

    for blk in range(CTX_LEN // A_BLOCK):
        prep_q(blk * A_BLOCK, 0)
        st = _dot(k_ref[0, 0:CTX_LEN, :], qs_ref[0])
        m = jnp.maximum(jnp.max(st, axis=0, keepdims=True), sink)
        pt = jnp.exp2(st - m).astype(BF16)
        res = _dot(jnp.concatenate([vt_ref[0, :, 0:CTX_LEN], ones_ctx], axis=0), pt)
        write(res, m, blk * A_BLOCK)

    def some_blocks(t, carry):
        for u in range(PIPE_UNROLL):
            nxt = jnp.minimum(PIPE_UNROLL * t + u + 1, nb - 1)
            prep_q(pl.multiple_of(CTX_LEN + nxt * A_BLOCK, A_BLOCK), (u + 1) % 2)
            scores((u + 1) % 2, nxt, (u + 1) % 2)
            update(PIPE_UNROLL * t + u, u % 2)
        return carry

    prep_q(CTX_LEN, 0)
    scores(0, 0, 0)
    lax.fori_loop(0, nb // PIPE_UNROLL, some_blocks, 0)


def _window_mask():
    o = np.arange(3)[:, None, None]
    key = np.arange(WIN_BAND)[None, :, None]
    qry = np.arange(A_BLOCK)[None, None, :]
    return jnp.asarray(np.where(np.abs(qry + o * A_BLOCK - key) <= A_WINDOW, 0.0, NEG_INF), F32)


def _window_attn(qat, ka, vat, sink_row):
    b, _, n = qat.shape
    return pl.pallas_call(
        functools.partial(_window_kernel, seq=n - CTX_LEN),
        grid=(b, A_KV_HEADS),
        scratch_shapes=[pltpu.VMEM((2, WIN_BAND + CTX_LEN, WIN_LANES), F32),
                        pltpu.VMEM((2, 1, WIN_LANES), F32),
                        pltpu.VMEM((2, 2 * HEAD_DIM, WIN_LANES), BF16)],
        in_specs=[
            pl.BlockSpec((1, GQA_R * HEAD_DIM, n), lambda bb, g: (bb, g, 0)),
            pl.BlockSpec((1, n, KVA_W), lambda bb, g: (bb, 0, 0)),
            pl.BlockSpec((1, HEAD_DIM, n), lambda bb, g: (bb, g, 0)),
            pl.BlockSpec((1, 1, WIN_LANES), lambda bb, g: (g, 0, 0)),
            _const_spec((3, WIN_BAND, A_BLOCK)),
        ],
        out_specs=pl.BlockSpec((1, GQA_R * HEAD_DIM, n), lambda bb, g: (bb, g, 0)),
        out_shape=jax.ShapeDtypeStruct((b, QA_W, n), BF16),
        compiler_params=_params(ARB2, _nbytes((2 * GQA_R * HEAD_DIM + KVA_W + HEAD_DIM, n), BF16)
                                + _nbytes((WIN_BAND + CTX_LEN + 2 * HEAD_DIM, WIN_LANES), F32)
                                + _nbytes((3, WIN_BAND, A_BLOCK), F32, 1),
                                VMEM_VALUE_BYTES // 2),
        name="window_attn",
    )(qat, ka, vat, sink_row, _window_mask())


NA_ITEM_ROWS = LANES // GRID_W
NA_WIN_ROWS = NA_ROWS + NA_ITEM_ROWS
NA_HG = 4
NA_W = NA_HG * HEAD_DIM
NA_LANES = NA_HG * LANES
NA_BIAS_N = 2 * NA_ROWS


def _na_kernel(qt_ref, k_ref, vt_ref, ta_ref, tb_ref, o_ref, s_ref, mx_ref, qs_ref, *, rows_n):
    nitems = rows_n // NA_ITEM_ROWS
    nloc = NA_WIN_ROWS * GRID_W
    assert nitems % PIPE_UNROLL == 0 and (rows_n - NA_WIN_ROWS) % 2 == 0 and nloc % LANES == 0
    ones_loc = jnp.ones((SUM_ROWS, nloc + CTX_LEN), BF16)
    ones_ctx = jnp.ones((SUM_ROWS, CTX_LEN), BF16)

    def prep_q(col0, qslot):
        qt = qt_ref[0, :, pl.ds(col0, LANES)]
        for h in range(NA_HG):
            rows = slice(h * HEAD_DIM, (h + 1) * HEAD_DIM)
            qs_ref[qslot, rows, h * LANES:(h + 1) * LANES] = qt[rows]

    def win_start(j):
        return jnp.clip(NA_ITEM_ROWS * j - NA_ROWS // 2, 0, rows_n - NA_WIN_ROWS)

    def scores(qslot, j, sslot):
        ru = win_start(j)
        k0 = pl.multiple_of(CTX_LEN + ru * GRID_W, LANES)
        keys = jnp.concatenate([k_ref[0, pl.ds(k0, nloc), :], k_ref[0, 0:CTX_LEN, :]], axis=0)
        st = _dot(keys, qs_ref[qslot])
        mx = jnp.max(st[nloc:], axis=0, keepdims=True)
        s_ref[sslot, nloc:] = st[nloc:]
        for i in range(NA_WIN_ROWS):
            idx = []
            for e in range(NA_ITEM_ROWS):
                r = NA_ITEM_ROWS * j + e
                r0 = jnp.clip(r - NA_ROWS // 2, 0, rows_n - NA_ROWS)
                seen = (ru + i >= r0) & (ru + i < r0 + NA_ROWS)
                idx.append(jnp.where(seen, ru + i - r + NA_ROWS - 1, NA_BIAS_N - 1))
            rows = slice(i * GRID_W, (i + 1) * GRID_W)
            bias = ta_ref[0, idx[0]] + tb_ref[0, idx[1]]
            blk = st[rows] + jnp.concatenate([bias[h] for h in range(NA_HG)], axis=1)
            s_ref[sslot, rows] = blk
            mx = jnp.maximum(mx, jnp.max(blk, axis=0, keepdims=True))
        mx_ref[sslot] = mx

    def pv_write(va, ones, pt, col0):
        for pair in range(NA_HG // 2):
            rows = slice(pair * LANES, (pair + 1) * LANES)
            res = _dot(jnp.concatenate([va[rows], ones], axis=0),
                       pt[:, pair * 2 * LANES:(pair + 1) * 2 * LANES])
            for hh in range(2):
                lanes = slice(hh * LANES, (hh + 1) * LANES)
                o = res[hh * HEAD_DIM:(hh + 1) * HEAD_DIM, lanes] / res[LANES:LANES + 1, lanes]
                h = 2 * pair + hh
                o_ref[0, h * HEAD_DIM:(h + 1) * HEAD_DIM, pl.ds(col0, LANES)] = o.astype(o_ref.dtype)

    def update(j, sslot):
        k0 = pl.multiple_of(CTX_LEN + win_start(j) * GRID_W, LANES)
        pt = jnp.exp2(s_ref[sslot] - mx_ref[sslot]).astype(BF16)
        va = jnp.concatenate([vt_ref[0, :, pl.ds(k0, nloc)], vt_ref[0, :, 0:CTX_LEN]], axis=1)
        pv_write(va, ones_loc, pt, pl.multiple_of(CTX_LEN + j * LANES, LANES))

    qs_ref[...] = jnp.zeros(qs_ref.shape, BF16)
    for blk in range(CTX_LEN // LANES):
        prep_q(blk * LANES, 0)
        st = _dot(k_ref[0, 0:CTX_LEN, :], qs_ref[0])
        pt = jnp.exp2(st - jnp.max(st, axis=0, keepdims=True)).astype(BF16)
        pv_write(vt_ref[0, :, 0:CTX_LEN], ones_ctx, pt, blk * LANES)

    def some_items(t, carry):
        for u in range(PIPE_UNROLL):
            nxt = jnp.minimum(PIPE_UNROLL * t + u + 1, nitems - 1)
            prep_q(pl.multiple_of(CTX_LEN + nxt * LANES, LANES), (u + 1) % 2)
            scores((u + 1) % 2, nxt, (u + 1) % 2)
            update(PIPE_UNROLL * t + u, u % 2)
        return carry

    prep_q(CTX_LEN, 0)
    scores(0, 0, 0)
    lax.fori_loop(0, nitems // PIPE_UNROLL, some_items, 0)


def _na_attn(qbt, kb, vbt, ta, tb):
    b, _, n = qbt.shape
    rows_n = (n - CTX_LEN) // GRID_W
    assert rows_n >= NA_WIN_ROWS
    nkeys = NA_WIN_ROWS * GRID_W + CTX_LEN
    tspec = pl.BlockSpec((1, NA_BIAS_N, NA_HG, GRID_W, LANES), lambda bb, hg: (hg, 0, 0, 0, 0))
    return pl.pallas_call(
        functools.partial(_na_kernel, rows_n=rows_n),
        grid=(b, B_HEADS // NA_HG),
        scratch_shapes=[pltpu.VMEM((2, nkeys, NA_LANES), F32),
                        pltpu.VMEM((2, 1, NA_LANES), F32),
                        pltpu.VMEM((2, NA_W, NA_LANES), BF16)],
        in_specs=[
            pl.BlockSpec((1, NA_W, n), lambda bb, hg: (bb, hg, 0)),
            pl.BlockSpec((1, n, NA_W), lambda bb, hg: (bb, 0, hg)),
            pl.BlockSpec((1, NA_W, n), lambda bb, hg: (bb, hg, 0)),
            tspec, tspec,
        ],
        out_specs=pl.BlockSpec((1, NA_W, n), lambda bb, hg: (bb, hg, 0)),
        out_shape=jax.ShapeDtypeStruct((b, QKVB_W, n), BF16),
        compiler_params=_params(ARB2, _nbytes((4 * NA_W, n), BF16) + _nbytes((nkeys + NA_W, NA_LANES), F32)
                                + 2 * _nbytes((NA_BIAS_N, NA_HG, GRID_W, LANES), F32)),
        name="na_attn",
    )(qbt, kb, vbt, ta, tb)


def _mlp_tail(xs, y, mods, gn, w1_ref, w2_ref):
    x1s = [x + m[0] * y[k * TM:(k + 1) * TM] for k, (x, m) in enumerate(zip(xs, mods))]
    h = jnp.concatenate([_norm_mod(x1, gn, m[1], m[2]).astype(BF16) for x1, m in zip(x1s, mods)], axis=0)
    mlp = None
    for j in range(D_FF // D_MODEL):
        cols = slice(j * D_MODEL, (j + 1) * D_MODEL)
        u = jnp.maximum(_dot(h, w1_ref[:, cols]), 0.0)
        part = _dot((u * u).astype(BF16), w2_ref[cols, :])
        mlp = part if mlp is None else mlp + part
    return [x1 + m[3] * mlp[k * TM:(k + 1) * TM] for k, (x1, m) in enumerate(zip(x1s, mods))]


def _even_out_kernel(ctx_ref, *refs):
    x_refs = refs[:NSUB]
    (oat_ref, obt_ref, g1_ref, sh2_ref, sc2_ref, g2_ref, gn_ref,
     woa_ref, wob_ref, w1_ref, w2_ref, o_ref) = refs[NSUB:]
    oa = oat_ref[0].astype(F32).T.astype(BF16)
    ob = obt_ref[0].astype(F32).T.astype(BF16)
    y = _dot(oa, woa_ref[...]) + _dot(ob, wob_ref[...])
    xs = [_joint_tile(ctx_ref, x_refs, k) for k in range(NSUB)]
    mods = [[_pick_mod(r, _is_ctx(k)) for r in (g1_ref, sh2_ref, sc2_ref, g2_ref)] for k in range(NSUB)]
    for k, out in enumerate(_mlp_tail(xs, y, mods, gn_ref[...], w1_ref, w2_ref)):
        o_ref[0, k * TM:(k + 1) * TM, :] = out


def _even_out(ctx, x, oa, ob, mods, gn, woa, wob, w1, w2):
    b, s, d = x.shape
    n = CTX_LEN + s
    tmb = NSUB * TM
    return pl.pallas_call(
        _even_out_kernel,
        grid=(b, n // tmb),
        in_specs=_joint_specs(d) + [
            pl.BlockSpec((1, QA_W, tmb), lambda bb, i: (bb, 0, i)),
            pl.BlockSpec((1, QKVB_W, tmb), lambda bb, i: (bb, 0, i)),
            _mod_spec(2), _mod_spec(3), _mod_spec(4), _mod_spec(5),
            _const_spec((1, d)),
            _const_spec((QA_W, d)), _const_spec((QKVB_W, d)),
            _const_spec((d, D_FF)), _const_spec((D_FF, d)),
        ],
        out_specs=pl.BlockSpec((1, tmb, d), lambda bb, i: (bb, i, 0)),
        out_shape=jax.ShapeDtypeStruct((b, n, d), F32),
        compiler_params=_params(ARB2, _nbytes((CTX_LEN + 2 * tmb, d), F32) + _nbytes((QA_W + QKVB_W, tmb), BF16)
                                + _nbytes((d + 2 * D_FF, d), BF16, 1)),
        name="even_out_mlp",
    )(ctx, *([x] * NSUB), oa, ob, mods, mods, mods, mods, gn, woa, wob, w1, w2)


NSUB_ODD = 2


def _odd_out_kernel(*refs):
    x_refs = refs[:NSUB_ODD]
    ot_ref, g1_ref, sh2_ref, sc2_ref, g2_ref, gn_ref, wo_ref, w1_ref, w2_ref, o_ref = refs[NSUB_ODD:]
    o = ot_ref[0].astype(F32).T.astype(BF16)
    y = _dot(o, wo_ref[...])
    mods = [[g1_ref[0], sh2_ref[0], sc2_ref[0], g2_ref[0]]] * NSUB_ODD
    for k, out in enumerate(_mlp_tail([r[0] for r in x_refs], y, mods, gn_ref[...], w1_ref, w2_ref)):
        o_ref[0, k * TM:(k + 1) * TM, :] = out


def _odd_out(xa, ot, mods, gn, wo, w1, w2):
    b, n, d = xa.shape
    s = n - CTX_LEN
    tmb = NSUB_ODD * TM
    assert s % tmb == 0
    lat = lambda k: pl.BlockSpec((1, TM, d), lambda bb, i: (bb, NSUB_ODD * i + k + 1, 0))
    return pl.pallas_call(
        _odd_out_kernel,
        grid=(b, s // tmb),
        in_specs=[lat(k) for k in range(NSUB_ODD)] + [
            pl.BlockSpec((1, d, tmb), lambda bb, i: (bb, 0, i)),
            _mod_spec(2, True), _mod_spec(3, True), _mod_spec(4, True), _mod_spec(5, True),
            _const_spec((1, d)),
            _const_spec((d, d)),
            _const_spec((d, D_FF)), _const_spec((D_FF, d)),
        ],
        out_specs=pl.BlockSpec((1, tmb, d), lambda bb, i: (bb, i, 0)),
        out_shape=jax.ShapeDtypeStruct((b, s, d), F32),
        compiler_params=_params(ARB2, _nbytes((2 * tmb, d), F32) + _nbytes((d, tmb), BF16)
                                + _nbytes((d + 2 * D_FF, d), BF16, 1)),
        name="odd_out_mlp",
    )(*([xa] * NSUB_ODD), ot, mods, mods, mods, mods, gn, wo, w1, w2)


ODD_IN_COLS = C_Q_RANK + C_KV_RANK + LANES
Q_NOPE_W = C_HEADS * C_NOPE
Q_ROPE_W = C_HEADS * C_ROPE


def _odd_proj_kernel(x_ref, sh_ref, sc_ref, g_ref, win_ref, qag_ref, kvg_ref, wuq_ref, wukv_ref,
                     gq_ref, gk_ref, e64_ref, e32_ref, cos_ref, sin_ref,
                     qn_ref, qr_ref, kn_ref, kr_ref, vt_ref):
    def rms(t, g):
        return (t * lax.rsqrt(jnp.mean(t * t, axis=-1, keepdims=True) + NORM_EPS) * g).astype(BF16)

    gq = gq_ref[...]
    gk = gk_ref[...]
    for k in range(NSUB):
        rows = slice(k * TM, (k + 1) * TM)
        h = _norm_mod(x_ref[0, rows, :], g_ref[...],
                      _pick_mod(sh_ref, _is_ctx(k)), _pick_mod(sc_ref, _is_ctx(k))).astype(BF16)
        y = _dot(h, win_ref[...])
        cq = y[:, :C_Q_RANK]
        ckv = y[:, C_Q_RANK:C_Q_RANK + C_KV_RANK]
        kr = y[:, C_Q_RANK + C_KV_RANK:]
        q = _dot(rms(cq, qag_ref[...]), wuq_ref[...])
        kv = _dot(rms(ckv, kvg_ref[...]), wukv_ref[...])
        cos = cos_ref[rows, :]
        sin = sin_ref[rows, :]
        qn = _group_rms(q[:, :Q_NOPE_W], e64_ref[...], C_NOPE) * gq[:, :Q_NOPE_W]
        qn_ref[0, :, rows] = qn.T.astype(BF16)
        qr = _group_rms(q[:, Q_NOPE_W:], e32_ref[...], C_ROPE) * gq[:, Q_NOPE_W:]
        qr_ref[0, :, rows] = _rope(qr, cos, sin, C_ROPE // 4).T.astype(BF16)
        kn = _group_rms(kv[:, :Q_NOPE_W], e64_ref[...], C_NOPE) * gk[:, :Q_NOPE_W]
        kn_ref[0, rows, :] = kn.astype(BF16)
        krn = _group_rms(kr, e32_ref[...], C_ROPE) * gk[:, Q_NOPE_W:]
        kr_ref[0, rows, :] = _rope(krn, cos, sin, C_ROPE // 4).astype(BF16)
        vt_ref[0, :, rows] = kv[:, Q_NOPE_W:].T.astype(BF16)


def _odd_proj(xa, mods, g, win, qag, kvg, wuq, wukv, gq, gk, e64, e32, cos, sin):
    b, n, d = xa.shape
    tmb = NSUB * TM
    assert n % tmb == 0
    tok = lambda wd: pl.BlockSpec((1, tmb, wd), lambda bb, i: (bb, i, 0))
    return pl.pallas_call(
        _odd_proj_kernel,
        grid=(b, n // tmb),
        in_specs=[
            tok(d), _mod_spec(0), _mod_spec(1),
            _const_spec((1, d)),
            _const_spec((d, ODD_IN_COLS)),
            _const_spec((1, C_Q_RANK)), _const_spec((1, C_KV_RANK)),
            _const_spec((C_Q_RANK, Q_NOPE_W + Q_ROPE_W)),
            _const_spec((C_KV_RANK, 2 * Q_NOPE_W)),
            _const_spec((1, Q_NOPE_W + Q_ROPE_W)), _const_spec((1, Q_NOPE_W + LANES)),
            _const_spec((MXU_DIM, MXU_DIM)), _const_spec((MXU_DIM, MXU_DIM)),
            pl.BlockSpec((tmb, LANES), lambda bb, i: (i, 0)),
            pl.BlockSpec((tmb, LANES), lambda bb, i: (i, 0)),
        ],
        out_specs=[pl.BlockSpec((1, Q_NOPE_W, tmb), lambda bb, i: (bb, 0, i)),
                   pl.BlockSpec((1, Q_ROPE_W, tmb), lambda bb, i: (bb, 0, i)),
                   tok(Q_NOPE_W), tok(LANES),
                   pl.BlockSpec((1, Q_NOPE_W, tmb), lambda bb, i: (bb, 0, i))],
        out_shape=[jax.ShapeDtypeStruct((b, Q_NOPE_W, n), BF16),
                   jax.ShapeDtypeStruct((b, Q_ROPE_W, n), BF16),
                   jax.ShapeDtypeStruct((b, n, Q_NOPE_W), BF16),
                   jax.ShapeDtypeStruct((b, n, LANES), BF16),
                   jax.ShapeDtypeStruct((b, Q_NOPE_W, n), BF16)],
        compiler_params=_params(ARB2, _nbytes((tmb, d), F32) + 2 * _nbytes((tmb, LANES), F32)
                                + _nbytes((tmb, 3 * Q_NOPE_W + Q_ROPE_W + LANES), BF16)
                                + _nbytes((d + C_Q_RANK + C_KV_RANK, 2 * Q_NOPE_W), BF16, 1)),
        name="odd_proj",
    )(xa, mods, mods, g, win, qag, kvg, wuq, wukv, gq, gk, e64, e32, cos, sin)


SUM_ROWS = 16
FLASH_TILES_PER_TRIP = 4


def _kv_tile(n):
    for t in (768, 256):
        if n % t == 0 and (n // t) % 2 == 1:
            return t
    raise ValueError(f"joint sequence length {n} has no odd split into 256-multiples")


def _flash_kernel(qnt_ref, qrt_ref, kn_ref, kr_ref, vt_ref, o_ref,
                  s_ref, mx_ref, m_ref, acc_ref, qt_ref, *, tk, tq):
    p = pl.program_id(1)
    n = kn_ref.shape[1]
    nq = (n - CTX_LEN) // tq
    nch = n // tk
    assert nq % 2 == 0 and nch % 2 == 1
    ones = jnp.ones((SUM_ROWS, tk), BF16)
    row = lax.broadcasted_iota(jnp.int32, (LANES, tq), 0)

    def prep_q(qi, qslot):
        c0 = pl.multiple_of(CTX_LEN + qi * tq, tq)
        qn = qnt_ref[0, :, pl.ds(c0, tq)]
        qr = qrt_ref[0, :, pl.ds(c0, tq)]
        zero = jnp.zeros_like(qn)
        for hh in range(2):
            off = ((2 * p + hh) % 4) * C_ROPE
            keep_n = (row < C_NOPE) if hh == 0 else (row >= C_NOPE)
            keep_r = (row >= off) & (row < off + C_ROPE)
            qt_ref[qslot, 0:LANES, hh * tq:(hh + 1) * tq] = jnp.where(keep_n, qn, zero)
            qt_ref[qslot, LANES:, hh * tq:(hh + 1) * tq] = jnp.where(keep_r, qr, zero)

    def scores(qslot, j, sslot):
        k0 = j * tk
        kcat = jnp.concatenate([kn_ref[0, pl.ds(k0, tk), :], kr_ref[0, pl.ds(k0, tk), :]], axis=1)
        st = _dot(kcat, qt_ref[qslot])
        s_ref[sslot] = st
        mx_ref[sslot] = jnp.max(st, axis=0, keepdims=True)

    def update(j, sslot):
        k0 = j * tk
        m_old = m_ref[...]
        m_new = jnp.maximum(m_old, mx_ref[sslot])
        alpha = jnp.exp2(m_old - m_new)
        pt = jnp.exp2(s_ref[sslot] - m_new).astype(BF16)
        va = jnp.concatenate([vt_ref[0, :, pl.ds(k0, tk)], ones], axis=0)
        m_ref[...] = m_new
        acc_ref[...] = alpha * acc_ref[...] + _dot(va, pt)

    def reset():
        m_ref[...] = jnp.full(m_ref.shape, NEG_INF, F32)
        acc_ref[...] = jnp.zeros(acc_ref.shape, F32)

    def finish(qi):
        acc = acc_ref[...]
        c0 = pl.multiple_of(qi * tq, tq)
        o_ref[0, 0:C_NOPE, pl.ds(c0, tq)] = (acc[0:C_NOPE, :tq] / acc[LANES:LANES + 1, :tq]).astype(o_ref.dtype)
        o_ref[0, C_NOPE:, pl.ds(c0, tq)] = (acc[C_NOPE:LANES, tq:] / acc[LANES:LANES + 1, tq:]).astype(o_ref.dtype)
        reset()

    def run_tile(qi, slot, next_qi):
        other = 1 - slot
        for j in range(nch - 1):
            scores(slot, j + 1, other if j % 2 == 0 else slot)
            update(j, slot if j % 2 == 0 else other)
        prep_q(next_qi, other)
        scores(other, 0, other)
        update(nch - 1, slot)
        finish(qi)

    def some_tiles(t, carry):
        for u in range(FLASH_TILES_PER_TRIP):
            qi = FLASH_TILES_PER_TRIP * t + u
            run_tile(qi, u % 2, jnp.minimum(qi + 1, nq - 1))
        return carry

    assert nq % FLASH_TILES_PER_TRIP == 0 and FLASH_TILES_PER_TRIP % 2 == 0
    reset()
    prep_q(0, 0)
    scores(0, 0, 0)
    lax.fori_loop(0, nq // FLASH_TILES_PER_TRIP, some_tiles, 0)


def _flash_attn(qnt, qrt, kn, kr, vt):
    b, n, _ = kn.shape
    s = n - CTX_LEN
    tq = TM
    tk = _kv_tile(n)
    return pl.pallas_call(
        functools.partial(_flash_kernel, tk=tk, tq=tq),
        grid=(b, C_HEADS // 2),
        scratch_shapes=[pltpu.VMEM((2, tk, 2 * tq), F32),
                        pltpu.VMEM((2, 1, 2 * tq), F32),
                        pltpu.VMEM((1, 2 * tq), F32),
                        pltpu.VMEM((LANES + SUM_ROWS, 2 * tq), F32),
                        pltpu.VMEM((2, 2 * LANES, 2 * tq), BF16)],
        in_specs=[
            pl.BlockSpec((1, LANES, n), lambda bb, p: (bb, p, 0)),
            pl.BlockSpec((1, LANES, n), lambda bb, p: (bb, p // 2, 0)),
            pl.BlockSpec((1, n, LANES), lambda bb, p: (bb, 0, p)),
            pl.BlockSpec((1, n, LANES), lambda bb, p: (bb, 0, 0)),
            pl.BlockSpec((1, LANES, n), lambda bb, p: (bb, p, 0)),
        ],
        out_specs=pl.BlockSpec((1, LANES, s), lambda bb, p: (bb, p, 0)),
        out_shape=jax.ShapeDtypeStruct((b, C_HEADS * C_NOPE, s), BF16),
        compiler_params=_params(ARB2, _nbytes((6 * LANES, n), BF16)
                                + _nbytes((tk + LANES + SUM_ROWS + 2 * LANES, 2 * tq), F32)),
        name="mla_flash",
    )(qnt, qrt, kn, kr, vt)


def _rope_tables(seq, dim):
    t = np.arange(seq)
    pos = np.stack([t // GRID_W, t % GRID_W], axis=0).astype(np.float64)
    half = dim // 2
    q = half // 2
    inv = ROPE_THETA ** (-np.arange(q, dtype=np.float64) / q)
    j = np.arange(dim)
    ang = pos[j // half].T * inv[j % q][None, :]
    sign = np.where((j % half) < q, -1.0, 1.0)
    reps = LANES // dim
    cos = np.tile(np.cos(ang), (1, reps))
    sin = np.tile(np.sin(ang) * sign[None, :], (1, reps))
    cos = np.concatenate([np.ones((CTX_LEN, LANES)), cos], axis=0)
    sin = np.concatenate([np.zeros((CTX_LEN, LANES)), sin], axis=0)
    return jnp.asarray(cos, F32), jnp.asarray(sin, F32)


def _group_ones(group):
    r = np.arange(MXU_DIM) // group
    return jnp.asarray(r[:, None] == r[None, :], BF16)


def _na_bias_tables(rpb, mult):
    cq = jnp.arange(GRID_W)
    c0 = jnp.clip(cq - NA_COLS // 2, 0, GRID_W - NA_COLS)
    col_ok = (cq[:, None] >= c0[None, :]) & (cq[:, None] < c0[None, :] + NA_COLS)
    dci = jnp.clip(cq[:, None] - cq[None, :], 1 - NA_COLS, NA_COLS - 1) + NA_COLS - 1
    pick = (dci[None] == jnp.arange(2 * NA_COLS - 1)[:, None, None]).astype(F32)
    tt = jnp.einsum("hdm,mkq->hdkq", rpb.astype(F32) * mult, pick, precision=lax.Precision.HIGHEST)
    tt = jnp.where(col_ok[None, None], tt, NEG_INF)
    tt = jnp.concatenate([tt, jnp.full_like(tt[:, :1], NEG_INF)], axis=1)
    zero = jnp.zeros_like(tt)

    def per_group(t):
        t = t.reshape(B_HEADS // NA_HG, NA_HG, NA_BIAS_N, GRID_W, LANES)
        return t.transpose(0, 2, 1, 3, 4)

    return per_group(jnp.concatenate([tt, zero], axis=-1)), per_group(jnp.concatenate([zero, tt], axis=-1))


def kernel(x, c, ctx, c_ctx, ada_w, ada_b, norm_mix, norm_mlp, mlp_w1, mlp_w2, e_w_in, e_w_out, a_q_norm, a_k_norm, a_sink, b_q_norm, b_k_norm, b_rpb, o_w_in, o_qa_norm, o_kva_norm, o_w_uq, o_w_ukv, o_qn_nope, o_qn_rope, o_kn_nope, o_kn_rope, o_w_out):
    bsz, seq, d = x.shape
    assert d == D_MODEL and ctx.shape[1] == CTX_LEN and seq % TM == 0 and ada_w.shape[0] == 2
    assert bsz + 1 <= 8

    cond = jnp.zeros((8, d), F32).at[:bsz].set(c).at[bsz].set(c_ctx)
    m = _adaln(cond, ada_w, ada_b)
    mods = [jnp.stack([jnp.broadcast_to(m[i, bsz], (bsz, 6 * d)), m[i, :bsz]], axis=1).reshape(2 * bsz, 1, 6 * d)
            for i in range(2)]

    e64 = _group_ones(HEAD_DIM)
    e32 = _group_ones(C_ROPE)

    w_ext = e_w_in[0].astype(BF16)
    scale = HEAD_DIM ** -0.5
    log2e = math.log2(math.e)
    gains = jnp.concatenate([jnp.tile(a_q_norm[0], A_HEADS) * (scale * log2e), jnp.tile(a_k_norm[0], A_KV_HEADS),
                             jnp.tile(b_q_norm[0], B_HEADS) * (scale * log2e), jnp.tile(b_k_norm[0], B_HEADS)])[None, :]
    cos64, sin64 = _rope_tables(seq, HEAD_DIM)
    qat, ka, qbt, kb, vat, vbt = _even_proj(ctx, x, mods[0], norm_mix[0][None, :], w_ext, gains, e64, cos64, sin64)
    sink_row = (jnp.repeat(a_sink[0].reshape(A_KV_HEADS, GQA_R), A_BLOCK, axis=1) * log2e)[:, None, :]
    oa = _window_attn(qat, ka, vat, sink_row)
    ob = _na_attn(qbt, kb, vbt, *_na_bias_tables(b_rpb[0], log2e))
    wo = e_w_out[0].astype(BF16)
    xa = _even_out(ctx, x, oa, ob, mods[0], norm_mlp[0][None, :], wo[:512], wo[512:],
                   mlp_w1[0].astype(BF16), mlp_w2[0].astype(BF16))

    wi = o_w_in[0]
    win = jnp.concatenate([wi[:, :C_Q_RANK + C_KV_RANK]] + [wi[:, C_Q_RANK + C_KV_RANK:]] * 4, axis=1).astype(BF16)
    wuq = o_w_uq[0].reshape(C_Q_RANK, C_HEADS, C_NOPE + C_ROPE)
    wuq = jnp.concatenate([wuq[:, :, :C_NOPE].reshape(C_Q_RANK, -1), wuq[:, :, C_NOPE:].reshape(C_Q_RANK, -1)],
                          axis=1).astype(BF16)
    wukv = o_w_ukv[0].reshape(C_KV_RANK, C_HEADS, 2 * C_NOPE)
    wukv = jnp.concatenate([wukv[:, :, :C_NOPE].reshape(C_KV_RANK, -1), wukv[:, :, C_NOPE:].reshape(C_KV_RANK, -1)],
                           axis=1).astype(BF16)
    qscale = (C_NOPE + C_ROPE) ** -0.5 * math.log2(math.e)
    gq = (jnp.concatenate([jnp.tile(o_qn_nope[0], C_HEADS), jnp.tile(o_qn_rope[0], C_HEADS)]) * qscale)[None, :]
    gk = jnp.concatenate([jnp.tile(o_kn_nope[0], C_HEADS), jnp.tile(o_kn_rope[0], LANES // C_ROPE)])[None, :]
    cos32, sin32 = _rope_tables(seq, C_ROPE)
    qn, qr, kn, kr, vt = _odd_proj(xa, mods[1], norm_mix[1][None, :], win, o_qa_norm[0][None, :],
                                   o_kva_norm[0][None, :], wuq, wukv, gq, gk, e64, e32, cos32, sin32)
    ot = _flash_attn(qn, qr, kn, kr, vt)
    return _odd_out(xa, ot, mods[1], norm_mlp[1][None, :], o_w_out[0].astype(BF16),
                    mlp_w1[1].astype(BF16), mlp_w2[1].astype(BF16))
```

```python
import functools
import math

import jax
import jax.numpy as jnp
import numpy as np
from jax import lax
from jax.experimental import pallas as pl
from jax.experimental.pallas import tpu as pltpu

F32 = jnp.float32
BF16 = jnp.bfloat16

D_MODEL = 1024
CTX_LEN = 256
GRID_W = 64
HEAD_DIM = 64
A_HEADS = 8
A_KV_HEADS = 2
A_WINDOW = 128
A_BLOCK = 128
B_HEADS = 8
NA_ROWS = 8
NA_COLS = 16
C_HEADS = 16
C_Q_RANK = 384
C_KV_RANK = 256
C_NOPE = 64
C_ROPE = 32
D_FF = 4 * D_MODEL
ROPE_THETA = 10000.0
NORM_EPS = 1e-6
NEG_INF = -1e30

LANES = 128
MXU_DIM = 256
TM = CTX_LEN
V7X_VMEM_BYTES = 64 * 1024 * 1024
VMEM_VALUE_BYTES = 32 * 1024 * 1024


ARB2 = ("arbitrary", "arbitrary")


def _dot(a, b):
    return jnp.dot(a, b, preferred_element_type=F32)


def _nbytes(shape, dtype, buffers=2):
    return math.prod(shape) * jnp.dtype(dtype).itemsize * buffers


def _params(sem, declared_bytes, value_bytes=VMEM_VALUE_BYTES):
    limit = min(declared_bytes + value_bytes, V7X_VMEM_BYTES * 7 // 8)
    return pltpu.CompilerParams(dimension_semantics=sem, vmem_limit_bytes=limit)


def _norm_mod(x, g, sh, sc):
    ms = jnp.mean(x * x, axis=-1, keepdims=True)
    return (x * lax.rsqrt(ms + NORM_EPS) * g) * (1.0 + sc) + sh


def _group_rms(y, e, group):
    out = []
    for c0 in range(0, y.shape[1], MXU_DIM):
        cw = min(MXU_DIM, y.shape[1] - c0)
        yc = y[:, c0:c0 + cw]
        ss = _dot((yc * yc).astype(BF16), e[:cw, :cw])
        out.append(yc * lax.rsqrt(ss * (1.0 / group) + NORM_EPS))
    return out[0] if len(out) == 1 else jnp.concatenate(out, axis=1)


def _rope(r, cos, sin, half):
    w = r.shape[1]
    reps = w // LANES
    if reps > 1:
        cos = jnp.concatenate([cos] * reps, axis=1)
        sin = jnp.concatenate([sin] * reps, axis=1)
    lane = lax.broadcasted_iota(jnp.int32, r.shape, 1)
    up = pltpu.roll(r, w - half, axis=1)
    dn = pltpu.roll(r, half, axis=1)
    sw = jnp.where((lane & half) == 0, up, dn)
    return r * cos + sw * sin


def _ada_kernel(cond_ref, w_ref, b_ref, o_ref):
    c = cond_ref[...]
    s = (c * jax.nn.sigmoid(c)).astype(BF16)
    o_ref[0] = _dot(s, w_ref[0].astype(BF16)) + b_ref[0]


def _adaln(cond, ada_w, ada_b):
    depth, d, n6 = ada_w.shape
    tn = 1536
    return pl.pallas_call(
        _ada_kernel,
        grid=(depth, n6 // tn),
        in_specs=[
            pl.BlockSpec((8, d), lambda l, j: (0, 0)),
            pl.BlockSpec((1, d, tn), lambda l, j: (l, 0, j)),
            pl.BlockSpec((1, 1, tn), lambda l, j: (l, 0, j)),
        ],
        out_specs=pl.BlockSpec((1, 8, tn), lambda l, j: (l, 0, j)),
        out_shape=jax.ShapeDtypeStruct((depth, 8, n6), F32),
        compiler_params=_params(ARB2, _nbytes((d + 2 * 8, tn), F32)),
        name="adaln",
    )(cond, ada_w, ada_b.reshape(depth, 1, n6))


NSUB = 3


def _mod_spec(chunk, lat_only=False):
    if lat_only:
        return pl.BlockSpec((1, 1, D_MODEL), lambda b, i: (2 * b + 1, 0, chunk))
    return pl.BlockSpec((2, 1, D_MODEL), lambda b, i: (b, 0, chunk))


def _pick_mod(ref, is_ctx):
    m = ref[...]
    return jnp.where(is_ctx, m[0], m[1])


def _is_ctx(k):
    return NSUB * pl.program_id(1) + k == 0


def _const_spec(shape):
    nd = len(shape)
    return pl.BlockSpec(shape, lambda *_: (0,) * nd, pipeline_mode=pl.Buffered(1))


QA_W = A_HEADS * HEAD_DIM
KVA_W = A_KV_HEADS * HEAD_DIM
QKVB_W = B_HEADS * HEAD_DIM
EVEN_ROPED = QA_W + KVA_W
EVEN_QB = EVEN_ROPED + KVA_W
EVEN_VB = EVEN_QB + 2 * QKVB_W
EVEN_COLS = EVEN_VB + QKVB_W
EVEN_NORMED = EVEN_ROPED + 2 * QKVB_W


def _joint_specs(d):
    lat = lambda k: pl.BlockSpec((1, TM, d), lambda bb, i: (bb, jnp.maximum(NSUB * i + k - 1, 0), 0))
    return [pl.BlockSpec((1, CTX_LEN, d), lambda bb, i: (bb, 0, 0))] + [lat(k) for k in range(NSUB)]


def _joint_tile(ctx_ref, x_refs, k):
    return jnp.where(_is_ctx(k), ctx_ref[0], x_refs[k][0])


def _even_proj_kernel(ctx_ref, *refs):
    x_refs = refs[:NSUB]
    (sh_ref, sc_ref, g_ref, w_ref, gains_ref, e_ref, cos_ref, sin_ref,
     qat_ref, ka_ref, qb_ref, kb_ref, vat_ref, vb_ref) = refs[NSUB:]
    for k in range(NSUB):
        rows = slice(k * TM, (k + 1) * TM)
        h = _norm_mod(_joint_tile(ctx_ref, x_refs, k), g_ref[...],
                      _pick_mod(sh_ref, _is_ctx(k)), _pick_mod(sc_ref, _is_ctx(k))).astype(BF16)
        y = _dot(h, w_ref[...])
        gains = gains_ref[...]
        ya = _group_rms(y[:, :EVEN_ROPED], e_ref[...], HEAD_DIM) * gains[:, :EVEN_ROPED]
        r = _rope(ya, cos_ref[rows, :], sin_ref[rows, :], HEAD_DIM // 4)
        yb = _group_rms(y[:, EVEN_QB:EVEN_VB], e_ref[...], HEAD_DIM) * gains[:, EVEN_ROPED:]
        qat_ref[0, :, rows] = r[:, 0:QA_W].T.astype(BF16)
        ka_ref[0, rows, :] = r[:, QA_W:].astype(BF16)
        qb_ref[0, :, rows] = yb[:, :QKVB_W].T.astype(BF16)
        kb_ref[0, rows, :] = yb[:, QKVB_W:].astype(BF16)
        vat_ref[0, :, rows] = y[:, EVEN_ROPED:EVEN_QB].T.astype(BF16)
        vb_ref[0, :, rows] = y[:, EVEN_VB:].T.astype(BF16)


def _even_proj(ctx, x, mods, g, w, gains, e64, cos, sin):
    b, s, d = x.shape
    n = CTX_LEN + s
    tmb = NSUB * TM
    assert n % tmb == 0
    tok = lambda wd: (pl.BlockSpec((1, tmb, wd), lambda bb, i: (bb, i, 0)),
                      jax.ShapeDtypeStruct((b, n, wd), BF16))
    tr = lambda wd: (pl.BlockSpec((1, wd, tmb), lambda bb, i: (bb, 0, i)),
                     jax.ShapeDtypeStruct((b, wd, n), BF16))
    outs = (tr(QA_W), tok(KVA_W), tr(QKVB_W), tok(QKVB_W), tr(KVA_W), tr(QKVB_W))
    return pl.pallas_call(
        _even_proj_kernel,
        grid=(b, n // tmb),
        in_specs=_joint_specs(d) + [
            _mod_spec(0), _mod_spec(1),
            _const_spec((1, d)),
            _const_spec((d, EVEN_COLS)),
            _const_spec((1, EVEN_NORMED)),
            _const_spec((MXU_DIM, MXU_DIM)),
            pl.BlockSpec((tmb, LANES), lambda bb, i: (i, 0)),
            pl.BlockSpec((tmb, LANES), lambda bb, i: (i, 0)),
        ],
        out_specs=[o[0] for o in outs],
        out_shape=[o[1] for o in outs],
        compiler_params=_params(ARB2, _nbytes((CTX_LEN + tmb, d), F32) + _nbytes((d, EVEN_COLS), BF16, 1)
                                + _nbytes((tmb, EVEN_COLS), BF16) + 2 * _nbytes((tmb, LANES), F32)),
        name="even_proj",
    )(ctx, *([x] * NSUB), mods, mods, g, w, gains, e64, cos, sin)


GQA_R = A_HEADS // A_KV_HEADS
WIN_BAND = 3 * A_BLOCK
WIN_LANES = GQA_R * A_BLOCK


PIPE_UNROLL = 2


def _window_kernel(qt_ref, k_ref, vt_ref, sink_ref, wm_ref, o_ref, s_ref, mx_ref, qs_ref, *, seq):
    g = pl.program_id(1)
    nb = seq // A_BLOCK
    nkeys = WIN_BAND + CTX_LEN
    assert nb % PIPE_UNROLL == 0 and PIPE_UNROLL % 2 == 0
    sink = sink_ref[0]
    ones_loc = jnp.ones((SUM_ROWS, nkeys), BF16)
    ones_ctx = jnp.ones((SUM_ROWS, CTX_LEN), BF16)

    def prep_q(col0, qslot):
        qt = qt_ref[0, :, pl.ds(col0, A_BLOCK)]
        zero = jnp.zeros((HEAD_DIM, A_BLOCK), BF16)
        cols = []
        for h in range(GQA_R):
            qh = qt[h * HEAD_DIM:(h + 1) * HEAD_DIM]
            cols.append(jnp.concatenate([jnp.where(g == 0, qh, zero), jnp.where(g == 0, zero, qh)], axis=0))
        qs_ref[qslot] = jnp.concatenate(cols, axis=1)

    def band_start(n):
        return pl.multiple_of(jnp.clip((n - 1) * A_BLOCK, 0, seq - WIN_BAND), A_BLOCK)

    def scores(qslot, n, sslot):
        start = band_start(n)
        keys = jnp.concatenate([k_ref[0, pl.ds(CTX_LEN + start, WIN_BAND), :], k_ref[0, 0:CTX_LEN, :]], axis=0)
        st = _dot(keys, qs_ref[qslot])
        mask = wm_ref[n - start // A_BLOCK]
        s_loc = st[:WIN_BAND] + jnp.concatenate([mask] * GQA_R, axis=1)
        s_ctx = st[WIN_BAND:]
        s_ref[sslot] = jnp.concatenate([s_loc, s_ctx], axis=0)
        mx_ref[sslot] = jnp.maximum(jnp.maximum(jnp.max(s_loc, axis=0, keepdims=True),
                                                jnp.max(s_ctx, axis=0, keepdims=True)), sink)

    def write(res, m, col0):
        o = res[0:HEAD_DIM] / (res[HEAD_DIM:HEAD_DIM + 1] + jnp.exp2(sink - m))
        for h in range(GQA_R):
            o_ref[0, h * HEAD_DIM:(h + 1) * HEAD_DIM, pl.ds(col0, A_BLOCK)] = (
                o[:, h * A_BLOCK:(h + 1) * A_BLOCK].astype(o_ref.dtype))

    def update(n, sslot):
        start = band_start(n)
        m = mx_ref[sslot]
        pt = jnp.exp2(s_ref[sslot] - m).astype(BF16)
        va = jnp.concatenate([vt_ref[0, :, pl.ds(CTX_LEN + start, WIN_BAND)], vt_ref[0, :, 0:CTX_LEN]], axis=1)
        res = _dot(jnp.concatenate([va, ones_loc], axis=0), pt)
        write(res, m, pl.multiple_of(CTX_LEN + n * A_BLOCK, A_BLOCK))

    for blk in range(CTX_LEN // A_BLOCK):
        prep_q(blk * A_BLOCK, 0)
        st = _dot(k_ref[0, 0:CTX_LEN, :], qs_ref[0])
        m = jnp.maximum(jnp.max(st, axis=0, keepdims=True), sink)
        pt = jnp.exp2(st - m).astype(BF16)
        res = _dot(jnp.concatenate([vt_ref[0, :, 0:CTX_LEN], ones_ctx], axis=0), pt)
        write(res, m, blk * A_BLOCK)

    def some_blocks(t, carry):
        for u in range(4):
            nxt = jnp.minimum(4 * t + u + 1, nb - 1)
            prep_q(pl.multiple_of(CTX_LEN + nxt * A_BLOCK, A_BLOCK), (u + 1) % 2)
            scores((u + 1) % 2, nxt, (u + 1) % 2)
            update(4 * t + u, u % 2)
        return carry

    prep_q(CTX_LEN, 0)
    scores(0, 0, 0)
    lax.fori_loop(0, nb // 4, some_blocks, 0)


def _window_mask():
    o = np.arange(3)[:, None, None]
    key = np.arange(WIN_BAND)[None, :, None]
    qry = np.arange(A_BLOCK)[None, None, :]
    return jnp.asarray(np.where(np.abs(qry + o * A_BLOCK - key) <= A_WINDOW, 0.0, NEG_INF), F32)


def _window_attn(qat, ka, vat, sink_row):
    b, _, n = qat.shape
    return pl.pallas_call(
        functools.partial(_window_kernel, seq=n - CTX_LEN),
        grid=(b, A_KV_HEADS),
        scratch_shapes=[pltpu.VMEM((2, WIN_BAND + CTX_LEN, WIN_LANES), F32),
                        pltpu.VMEM((2, 1, WIN_LANES), F32),
                        pltpu.VMEM((2, 2 * HEAD_DIM, WIN_LANES), BF16)],
        in_specs=[
            pl.BlockSpec((1, GQA_R * HEAD_DIM, n), lambda bb, g: (bb, g, 0)),
            pl.BlockSpec((1, n, KVA_W), lambda bb, g: (bb, 0, 0)),
            pl.BlockSpec((1, HEAD_DIM, n), lambda bb, g: (bb, g, 0)),
            pl.BlockSpec((1, 1, WIN_LANES), lambda bb, g: (g, 0, 0)),
            _const_spec((3, WIN_BAND, A_BLOCK)),
        ],
        out_specs=pl.BlockSpec((1, GQA_R * HEAD_DIM, n), lambda bb, g: (bb, g, 0)),
        out_shape=jax.ShapeDtypeStruct((b, QA_W, n), BF16),
        compiler_params=_params(ARB2, _nbytes((2 * GQA_R * HEAD_DIM + KVA_W + HEAD_DIM, n), BF16)
                                + _nbytes((WIN_BAND + CTX_LEN + 2 * HEAD_DIM, WIN_LANES), F32)
                                + _nbytes((3, WIN_BAND, A_BLOCK), F32, 1),
                                VMEM_VALUE_BYTES // 2),
        name="window_attn",
    )(qat, ka, vat, sink_row, _window_mask())


NA_ITEM_ROWS = LANES // GRID_W
NA_WIN_ROWS = NA_ROWS + NA_ITEM_ROWS
NA_HG = 4
NA_W = NA_HG * HEAD_DIM
NA_LANES = NA_HG * LANES
NA_BIAS_N = 2 * NA_ROWS


def _na_kernel(qt_ref, k_ref, vt_ref, ta_ref, tb_ref, o_ref, s_ref, mx_ref, qs_ref, *, rows_n):
    nitems = rows_n // NA_ITEM_ROWS
    nloc = NA_WIN_ROWS * GRID_W
    assert nitems % PIPE_UNROLL == 0 and (rows_n - NA_WIN_ROWS) % 2 == 0 and nloc % LANES == 0
    ones_loc = jnp.ones((SUM_ROWS, nloc + CTX_LEN), BF16)
    ones_ctx = jnp.ones((SUM_ROWS, CTX_LEN), BF16)

    def prep_q(col0, qslot):
        qt = qt_ref[0, :, pl.ds(col0, LANES)]
        for h in range(NA_HG):
            rows = slice(h * HEAD_DIM, (h + 1) * HEAD_DIM)
            qs_ref[qslot, rows, h * LANES:(h + 1) * LANES] = qt[rows]

    def win_start(j):
        return jnp.clip(NA_ITEM_ROWS * j - NA_ROWS // 2, 0, rows_n - NA_WIN_ROWS)

    def scores(qslot, j, sslot):
        ru = win_start(j)
        k0 = pl.multiple_of(CTX_LEN + ru * GRID_W, LANES)
        keys = jnp.concatenate([k_ref[0, pl.ds(k0, nloc), :], k_ref[0, 0:CTX_LEN, :]], axis=0)
        st = _dot(keys, qs_ref[qslot])
        mx = jnp.max(st[nloc:], axis=0, keepdims=True)
        s_ref[sslot, nloc:] = st[nloc:]
        for i in range(NA_WIN_ROWS):
            idx = []
            for e in range(NA_ITEM_ROWS):
                r = NA_ITEM_ROWS * j + e
                r0 = jnp.clip(r - NA_ROWS // 2, 0, rows_n - NA_ROWS)
                seen = (ru + i >= r0) & (ru + i < r0 + NA_ROWS)
                idx.append(jnp.where(seen, ru + i - r + NA_ROWS - 1, NA_BIAS_N - 1))
            rows = slice(i * GRID_W, (i + 1) * GRID_W)
            bias = ta_ref[0, idx[0]] + tb_ref[0, idx[1]]
            blk = st[rows] + jnp.concatenate([bias[h] for h in range(NA_HG)], axis=1)
            s_ref[sslot, rows] = blk
            mx = jnp.maximum(mx, jnp.max(blk, axis=0, keepdims=True))
        mx_ref[sslot] = mx

    def pv_write(va, ones, pt, col0):
        for pair in range(NA_HG // 2):
            rows = slice(pair * LANES, (pair + 1) * LANES)
            res = _dot(jnp.concatenate([va[rows], ones], axis=0),
                       pt[:, pair * 2 * LANES:(pair + 1) * 2 * LANES])
            for hh in range(2):
                lanes = slice(hh * LANES, (hh + 1) * LANES)
                o = res[hh * HEAD_DIM:(hh + 1) * HEAD_DIM, lanes] / res[LANES:LANES + 1, lanes]
                h = 2 * pair + hh
                o_ref[0, h * HEAD_DIM:(h + 1) * HEAD_DIM, pl.ds(col0, LANES)] = o.astype(o_ref.dtype)

    def update(j, sslot):
        k0 = pl.multiple_of(CTX_LEN + win_start(j) * GRID_W, LANES)
        pt = jnp.exp2(s_ref[sslot] - mx_ref[sslot]).astype(BF16)
        va = jnp.concatenate([vt_ref[0, :, pl.ds(k0, nloc)], vt_ref[0, :, 0:CTX_LEN]], axis=1)
        pv_write(va, ones_loc, pt, pl.multiple_of(CTX_LEN + j * LANES, LANES))

    qs_ref[...] = jnp.zeros(qs_ref.shape, BF16)
    for blk in range(CTX_LEN // LANES):
        prep_q(blk * LANES, 0)
        st = _dot(k_ref[0, 0:CTX_LEN, :], qs_ref[0])
        pt = jnp.exp2(st - jnp.max(st, axis=0, keepdims=True)).astype(BF16)
        pv_write(vt_ref[0, :, 0:CTX_LEN], ones_ctx, pt, blk * LANES)

    def some_items(t, carry):
        for u in range(PIPE_UNROLL):
            nxt = jnp.minimum(PIPE_UNROLL * t + u + 1, nitems - 1)
            prep_q(pl.multiple_of(CTX_LEN + nxt * LANES, LANES), (u + 1) % 2)
            scores((u + 1) % 2, nxt, (u + 1) % 2)
            update(PIPE_UNROLL * t + u, u % 2)
        return carry

    prep_q(CTX_LEN, 0)
    scores(0, 0, 0)
    lax.fori_loop(0, nitems // PIPE_UNROLL, some_items, 0)


def _na_attn(qbt, kb, vbt, ta, tb):
    b, _, n = qbt.shape
    rows_n = (n - CTX_LEN) // GRID_W
    assert rows_n >= NA_WIN_ROWS
    nkeys = NA_WIN_ROWS * GRID_W + CTX_LEN
    tspec = pl.BlockSpec((1, NA_BIAS_N, NA_HG, GRID_W, LANES), lambda bb, hg: (hg, 0, 0, 0, 0))
    return pl.pallas_call(
        functools.partial(_na_kernel, rows_n=rows_n),
        grid=(b, B_HEADS // NA_HG),
        scratch_shapes=[pltpu.VMEM((2, nkeys, NA_LANES), F32),
                        pltpu.VMEM((2, 1, NA_LANES), F32),
                        pltpu.VMEM((2, NA_W, NA_LANES), BF16)],
        in_specs=[
            pl.BlockSpec((1, NA_W, n), lambda bb, hg: (bb, hg, 0)),
            pl.BlockSpec((1, n, NA_W), lambda bb, hg: (bb, 0, hg)),
            pl.BlockSpec((1, NA_W, n), lambda bb, hg: (bb, hg, 0)),
            tspec, tspec,
        ],
        out_specs=pl.BlockSpec((1, NA_W, n), lambda bb, hg: (bb, hg, 0)),
        out_shape=jax.ShapeDtypeStruct((b, QKVB_W, n), BF16),
        compiler_params=_params(ARB2, _nbytes((4 * NA_W, n), BF16) + _nbytes((nkeys + NA_W, NA_LANES), F32)
                                + 2 * _nbytes((NA_BIAS_N, NA_HG, GRID_W, LANES), F32)),
        name="na_attn",
    )(qbt, kb, vbt, ta, tb)


def _mlp_tail(xs, y, mods, gn, w1_ref, w2_ref):
    x1s = [x + m[0] * y[k * TM:(k + 1) * TM] for k, (x, m) in enumerate(zip(xs, mods))]
    h = jnp.concatenate([_norm_mod(x1, gn, m[1], m[2]).astype(BF16) for x1, m in zip(x1s, mods)], axis=0)
    mlp = None
    for j in range(D_FF // D_MODEL):
        cols = slice(j * D_MODEL, (j + 1) * D_MODEL)
        u = jnp.maximum(_dot(h, w1_ref[:, cols]), 0.0)
        part = _dot((u * u).astype(BF16), w2_ref[cols, :])
        mlp = part if mlp is None else mlp + part
    return [x1 + m[3] * mlp[k * TM:(k + 1) * TM] for k, (x1, m) in enumerate(zip(x1s, mods))]


def _even_out_kernel(ctx_ref, *refs):
    x_refs = refs[:NSUB]
    (oat_ref, obt_ref, g1_ref, sh2_ref, sc2_ref, g2_ref, gn_ref,
     woa_ref, wob_ref, w1_ref, w2_ref, o_ref) = refs[NSUB:]
    oa = oat_ref[0].astype(F32).T.astype(BF16)
    ob = obt_ref[0].astype(F32).T.astype(BF16)
    y = _dot(oa, woa_ref[...]) + _dot(ob, wob_ref[...])
    xs = [_joint_tile(ctx_ref, x_refs, k) for k in range(NSUB)]
    mods = [[_pick_mod(r, _is_ctx(k)) for r in (g1_ref, sh2_ref, sc2_ref, g2_ref)] for k in range(NSUB)]
    for k, out in enumerate(_mlp_tail(xs, y, mods, gn_ref[...], w1_ref, w2_ref)):
        o_ref[0, k * TM:(k + 1) * TM, :] = out


def _even_out(ctx, x, oa, ob, mods, gn, woa, wob, w1, w2):
    b, s, d = x.shape
    n = CTX_LEN + s
    tmb = NSUB * TM
    return pl.pallas_call(
        _even_out_kernel,
        grid=(b, n // tmb),
        in_specs=_joint_specs(d) + [
            pl.BlockSpec((1, QA_W, tmb), lambda bb, i: (bb, 0, i)),
            pl.BlockSpec((1, QKVB_W, tmb), lambda bb, i: (bb, 0, i)),
            _mod_spec(2), _mod_spec(3), _mod_spec(4), _mod_spec(5),
            _const_spec((1, d)),
            _const_spec((QA_W, d)), _const_spec((QKVB_W, d)),
            _const_spec((d, D_FF)), _const_spec((D_FF, d)),
        ],
        out_specs=pl.BlockSpec((1, tmb, d), lambda bb, i: (bb, i, 0)),
        out_shape=jax.ShapeDtypeStruct((b, n, d), F32),
        compiler_params=_params(ARB2, _nbytes((CTX_LEN + 2 * tmb, d), F32) + _nbytes((QA_W + QKVB_W, tmb), BF16)
                                + _nbytes((d + 2 * D_FF, d), BF16, 1)),
        name="even_out_mlp",
    )(ctx, *([x] * NSUB), oa, ob, mods, mods, mods, mods, gn, woa, wob, w1, w2)


NSUB_ODD = 2


def _odd_out_kernel(*refs):
    x_refs = refs[:NSUB_ODD]
    ot_ref, g1_ref, sh2_ref, sc2_ref, g2_ref, gn_ref, wo_ref, w1_ref, w2_ref, o_ref = refs[NSUB_ODD:]
    o = ot_ref[0].astype(F32).T.astype(BF16)
    y = _dot(o, wo_ref[...])
    mods = [[g1_ref[0], sh2_ref[0], sc2_ref[0], g2_ref[0]]] * NSUB_ODD
    for k, out in enumerate(_mlp_tail([r[0] for r in x_refs], y, mods, gn_ref[...], w1_ref, w2_ref)):
        o_ref[0, k * TM:(k + 1) * TM, :] = out


def _odd_out(xa, ot, mods, gn, wo, w1, w2):
    b, n, d = xa.shape
    s = n - CTX_LEN
    tmb = NSUB_ODD * TM
    assert s % tmb == 0
    lat = lambda k: pl.BlockSpec((1, TM, d), lambda bb, i: (bb, NSUB_ODD * i + k + 1, 0))
    return pl.pallas_call(
        _odd_out_kernel,
        grid=(b, s // tmb),
        in_specs=[lat(k) for k in range(NSUB_ODD)] + [
            pl.BlockSpec((1, d, tmb), lambda bb, i: (bb, 0, i)),
            _mod_spec(2, True), _mod_spec(3, True), _mod_spec(4, True), _mod_spec(5, True),
            _const_spec((1, d)),
            _const_spec((d, d)),
            _const_spec((d, D_FF)), _const_spec((D_FF, d)),
        ],
        out_specs=pl.BlockSpec((1, tmb, d), lambda bb, i: (bb, i, 0)),
        out_shape=jax.ShapeDtypeStruct((b, s, d), F32),
        compiler_params=_params(ARB2, _nbytes((2 * tmb, d), F32) + _nbytes((d, tmb), BF16)
                                + _nbytes((d + 2 * D_FF, d), BF16, 1)),
        name="odd_out_mlp",
    )(*([xa] * NSUB_ODD), ot, mods, mods, mods, mods, gn, wo, w1, w2)


ODD_IN_COLS = C_Q_RANK + C_KV_RANK + LANES
Q_NOPE_W = C_HEADS * C_NOPE
Q_ROPE_W = C_HEADS * C_ROPE


def _odd_proj_kernel(x_ref, sh_ref, sc_ref, g_ref, win_ref, qag_ref, kvg_ref, wuq_ref, wukv_ref,
                     gq_ref, gk_ref, e64_ref, e32_ref, cos_ref, sin_ref,
                     qn_ref, qr_ref, kn_ref, kr_ref, vt_ref):
    def rms(t, g):
        return (t * lax.rsqrt(jnp.mean(t * t, axis=-1, keepdims=True) + NORM_EPS) * g).astype(BF16)

    gq = gq_ref[...]
    gk = gk_ref[...]
    for k in range(NSUB):
        rows = slice(k * TM, (k + 1) * TM)
        h = _norm_mod(x_ref[0, rows, :], g_ref[...],
                      _pick_mod(sh_ref, _is_ctx(k)), _pick_mod(sc_ref, _is_ctx(k))).astype(BF16)
        y = _dot(h, win_ref[...])
        cq = y[:, :C_Q_RANK]
        ckv = y[:, C_Q_RANK:C_Q_RANK + C_KV_RANK]
        kr = y[:, C_Q_RANK + C_KV_RANK:]
        q = _dot(rms(cq, qag_ref[...]), wuq_ref[...])
        kv = _dot(rms(ckv, kvg_ref[...]), wukv_ref[...])
        cos = cos_ref[rows, :]
        sin = sin_ref[rows, :]
        qn = _group_rms(q[:, :Q_NOPE_W], e64_ref[...], C_NOPE) * gq[:, :Q_NOPE_W]
        qn_ref[0, :, rows] = qn.T.astype(BF16)
        qr = _group_rms(q[:, Q_NOPE_W:], e32_ref[...], C_ROPE) * gq[:, Q_NOPE_W:]
        qr_ref[0, :, rows] = _rope(qr, cos, sin, C_ROPE // 4).T.astype(BF16)
        kn = _group_rms(kv[:, :Q_NOPE_W], e64_ref[...], C_NOPE) * gk[:, :Q_NOPE_W]
        kn_ref[0, rows, :] = kn.astype(BF16)
        krn = _group_rms(kr, e32_ref[...], C_ROPE) * gk[:, Q_NOPE_W:]
        kr_ref[0, rows, :] = _rope(krn, cos, sin, C_ROPE // 4).astype(BF16)
        vt_ref[0, :, rows] = kv[:, Q_NOPE_W:].T.astype(BF16)


def _odd_proj(xa, mods, g, win, qag, kvg, wuq, wukv, gq, gk, e64, e32, cos, sin):
    b, n, d = xa.shape
    tmb = NSUB * TM
    assert n % tmb == 0
    tok = lambda wd: pl.BlockSpec((1, tmb, wd), lambda bb, i: (bb, i, 0))
    return pl.pallas_call(
        _odd_proj_kernel,
        grid=(b, n // tmb),
        in_specs=[
            tok(d), _mod_spec(0), _mod_spec(1),
            _const_spec((1, d)),
            _const_spec((d, ODD_IN_COLS)),
            _const_spec((1, C_Q_RANK)), _const_spec((1, C_KV_RANK)),
            _const_spec((C_Q_RANK, Q_NOPE_W + Q_ROPE_W)),
            _const_spec((C_KV_RANK, 2 * Q_NOPE_W)),
            _const_spec((1, Q_NOPE_W + Q_ROPE_W)), _const_spec((1, Q_NOPE_W + LANES)),
            _const_spec((MXU_DIM, MXU_DIM)), _const_spec((MXU_DIM, MXU_DIM)),
            pl.BlockSpec((tmb, LANES), lambda bb, i: (i, 0)),
            pl.BlockSpec((tmb, LANES), lambda bb, i: (i, 0)),
        ],
        out_specs=[pl.BlockSpec((1, Q_NOPE_W, tmb), lambda bb, i: (bb, 0, i)),
                   pl.BlockSpec((1, Q_ROPE_W, tmb), lambda bb, i: (bb, 0, i)),
                   tok(Q_NOPE_W), tok(LANES),
                   pl.BlockSpec((1, Q_NOPE_W, tmb), lambda bb, i: (bb, 0, i))],
        out_shape=[jax.ShapeDtypeStruct((b, Q_NOPE_W, n), BF16),
                   jax.ShapeDtypeStruct((b, Q_ROPE_W, n), BF16),
                   jax.ShapeDtypeStruct((b, n, Q_NOPE_W), BF16),
                   jax.ShapeDtypeStruct((b, n, LANES), BF16),
                   jax.ShapeDtypeStruct((b, Q_NOPE_W, n), BF16)],
        compiler_params=_params(ARB2, _nbytes((tmb, d), F32) + 2 * _nbytes((tmb, LANES), F32)
                                + _nbytes((tmb, 3 * Q_NOPE_W + Q_ROPE_W + LANES), BF16)
                                + _nbytes((d + C_Q_RANK + C_KV_RANK, 2 * Q_NOPE_W), BF16, 1)),
        name="odd_proj",
    )(xa, mods, mods, g, win, qag, kvg, wuq, wukv, gq, gk, e64, e32, cos, sin)


SUM_ROWS = 16
FLASH_TILES_PER_TRIP = 4


def _kv_tile(n):
    for t in (768, 256):
        if n % t == 0 and (n // t) % 2 == 1:
            return t
    raise ValueError(f"joint sequence length {n} has no odd split into 256-multiples")


def _flash_kernel(qnt_ref, qrt_ref, kn_ref, kr_ref, vt_ref, o_ref,
                  s_ref, mx_ref, m_ref, acc_ref, qt_ref, *, tk, tq):
    p = pl.program_id(1)
    n = kn_ref.shape[1]
    nq = (n - CTX_LEN) // tq
    nch = n // tk
    assert nq % 2 == 0 and nch % 2 == 1
    ones = jnp.ones((SUM_ROWS, tk), BF16)
    row = lax.broadcasted_iota(jnp.int32, (LANES, tq), 0)

    def prep_q(qi, qslot):
        c0 = pl.multiple_of(CTX_LEN + qi * tq, tq)
        qn = qnt_ref[0, :, pl.ds(c0, tq)]
        qr = qrt_ref[0, :, pl.ds(c0, tq)]
        zero = jnp.zeros_like(qn)
        for hh in range(2):
            off = ((2 * p + hh) % 4) * C_ROPE
            keep_n = (row < C_NOPE) if hh == 0 else (row >= C_NOPE)
            keep_r = (row >= off) & (row < off + C_ROPE)
            qt_ref[qslot, 0:LANES, hh * tq:(hh + 1) * tq] = jnp.where(keep_n, qn, zero)
            qt_ref[qslot, LANES:, hh * tq:(hh + 1) * tq] = jnp.where(keep_r, qr, zero)

    def scores(qslot, j, sslot):
        k0 = j * tk
        kcat = jnp.concatenate([kn_ref[0, pl.ds(k0, tk), :], kr_ref[0, pl.ds(k0, tk), :]], axis=1)
        st = _dot(kcat, qt_ref[qslot])
        s_ref[sslot] = st
        mx_ref[sslot] = jnp.max(st, axis=0, keepdims=True)

    def update(j, sslot):
        k0 = j * tk
        m_old = m_ref[...]
        m_new = jnp.maximum(m_old, mx_ref[sslot])
        alpha = jnp.exp2(m_old - m_new)
        pt = jnp.exp2(s_ref[sslot] - m_new).astype(BF16)
        va = jnp.concatenate([vt_ref[0, :, pl.ds(k0, tk)], ones], axis=0)
        m_ref[...] = m_new
        acc_ref[...] = alpha * acc_ref[...] + _dot(va, pt)

    def reset():
        m_ref[...] = jnp.full(m_ref.shape, NEG_INF, F32)
        acc_ref[...] = jnp.zeros(acc_ref.shape, F32)

    def finish(qi):
        acc = acc_ref[...]
        c0 = pl.multiple_of(qi * tq, tq)
        o_ref[0, 0:C_NOPE, pl.ds(c0, tq)] = (acc[0:C_NOPE, :tq] / acc[LANES:LANES + 1, :tq]).astype(o_ref.dtype)
        o_ref[0, C_NOPE:, pl.ds(c0, tq)] = (acc[C_NOPE:LANES, tq:] / acc[LANES:LANES + 1, tq:]).astype(o_ref.dtype)
        reset()

    def run_tile(qi, slot, next_qi):
        other = 1 - slot
        for j in range(nch - 1):
            scores(slot, j + 1, other if j % 2 == 0 else slot)
            update(j, slot if j % 2 == 0 else other)
        prep_q(next_qi, other)
        scores(other, 0, other)
        update(nch - 1, slot)
        finish(qi)

    def some_tiles(t, carry):
        for u in range(FLASH_TILES_PER_TRIP):
            qi = FLASH_TILES_PER_TRIP * t + u
            run_tile(qi, u % 2, jnp.minimum(qi + 1, nq - 1))
        return carry

    assert nq % FLASH_TILES_PER_TRIP == 0 and FLASH_TILES_PER_TRIP % 2 == 0
    reset()
    prep_q(0, 0)
    scores(0, 0, 0)
    lax.fori_loop(0, nq // FLASH_TILES_PER_TRIP, some_tiles, 0)


def _flash_attn(qnt, qrt, kn, kr, vt):
    b, n, _ = kn.shape
    s = n - CTX_LEN
    tq = TM
    tk = _kv_tile(n)
    return pl.pallas_call(
        functools.partial(_flash_kernel, tk=tk, tq=tq),
        grid=(b, C_HEADS // 2),
        scratch_shapes=[pltpu.VMEM((2, tk, 2 * tq), F32),
                        pltpu.VMEM((2, 1, 2 * tq), F32),
                        pltpu.VMEM((1, 2 * tq), F32),
                        pltpu.VMEM((LANES + SUM_ROWS, 2 * tq), F32),
                        pltpu.VMEM((2, 2 * LANES, 2 * tq), BF16)],
        in_specs=[
            pl.BlockSpec((1, LANES, n), lambda bb, p: (bb, p, 0)),
            pl.BlockSpec((1, LANES, n), lambda bb, p: (bb, p // 2, 0)),
            pl.BlockSpec((1, n, LANES), lambda bb, p: (bb, 0, p)),
            pl.BlockSpec((1, n, LANES), lambda bb, p: (bb, 0, 0)),
            pl.BlockSpec((1, LANES, n), lambda bb, p: (bb, p, 0)),
        ],
        out_specs=pl.BlockSpec((1, LANES, s), lambda bb, p: (bb, p, 0)),
        out_shape=jax.ShapeDtypeStruct((b, C_HEADS * C_NOPE, s), BF16),
        compiler_params=_params(ARB2, _nbytes((6 * LANES, n), BF16)
                                + _nbytes((tk + LANES + SUM_ROWS + 2 * LANES, 2 * tq), F32)),
        name="mla_flash",
    )(qnt, qrt, kn, kr, vt)


def _rope_tables(seq, dim):
    t = np.arange(seq)
    pos = np.stack([t // GRID_W, t % GRID_W], axis=0).astype(np.float64)
    half = dim // 2
    q = half // 2
    inv = ROPE_THETA ** (-np.arange(q, dtype=np.float64) / q)
    j = np.arange(dim)
    ang = pos[j // half].T * inv[j % q][None, :]
    sign = np.where((j % half) < q, -1.0, 1.0)
    reps = LANES // dim
    cos = np.tile(np.cos(ang), (1, reps))
    sin = np.tile(np.sin(ang) * sign[None, :], (1, reps))
    cos = np.concatenate([np.ones((CTX_LEN, LANES)), cos], axis=0)
    sin = np.concatenate([np.zeros((CTX_LEN, LANES)), sin], axis=0)
    return jnp.asarray(cos, F32), jnp.asarray(sin, F32)


def _group_ones(group):
    r = np.arange(MXU_DIM) // group
    return jnp.asarray(r[:, None] == r[None, :], BF16)


def _na_bias_tables(rpb, mult):
    cq = jnp.arange(GRID_W)
    c0 = jnp.clip(cq - NA_COLS // 2, 0, GRID_W - NA_COLS)
    col_ok = (cq[:, None] >= c0[None, :]) & (cq[:, None] < c0[None, :] + NA_COLS)
    dci = jnp.clip(cq[:, None] - cq[None, :], 1 - NA_COLS, NA_COLS - 1) + NA_COLS - 1
    pick = (dci[None] == jnp.arange(2 * NA_COLS - 1)[:, None, None]).astype(F32)
    tt = jnp.einsum("hdm,mkq->hdkq", rpb.astype(F32) * mult, pick, precision=lax.Precision.HIGHEST)
    tt = jnp.where(col_ok[None, None], tt, NEG_INF)
    tt = jnp.concatenate([tt, jnp.full_like(tt[:, :1], NEG_INF)], axis=1)
    zero = jnp.zeros_like(tt)

    def per_group(t):
        t = t.reshape(B_HEADS // NA_HG, NA_HG, NA_BIAS_N, GRID_W, LANES)
        return t.transpose(0, 2, 1, 3, 4)

    return per_group(jnp.concatenate([tt, zero], axis=-1)), per_group(jnp.concatenate([zero, tt], axis=-1))


def kernel(x, c, ctx, c_ctx, ada_w, ada_b, norm_mix, norm_mlp, mlp_w1, mlp_w2, e_w_in, e_w_out, a_q_norm, a_k_norm, a_sink, b_q_norm, b_k_norm, b_rpb, o_w_in, o_qa_norm, o_kva_norm, o_w_uq, o_w_ukv, o_qn_nope, o_qn_rope, o_kn_nope, o_kn_rope, o_w_out):
    bsz, seq, d = x.shape
    assert d == D_MODEL and ctx.shape[1] == CTX_LEN and seq % TM == 0 and ada_w.shape[0] == 2
    assert bsz + 1 <= 8

    cond = jnp.zeros((8, d), F32).at[:bsz].set(c).at[bsz].set(c_ctx)
    m = _adaln(cond, ada_w, ada_b)
    mods = [jnp.stack([jnp.broadcast_to(m[i, bsz], (bsz, 6 * d)), m[i, :bsz]], axis=1).reshape(2 * bsz, 1, 6 * d)
            for i in range(2)]

    e64 = _group_ones(HEAD_DIM)
    e32 = _group_ones(C_ROPE)

    w_ext = e_w_in[0].astype(BF16)
    scale = HEAD_DIM ** -0.5
    log2e = math.log2(math.e)
    gains = jnp.concatenate([jnp.tile(a_q_norm[0], A_HEADS) * (scale * log2e), jnp.tile(a_k_norm[0], A_KV_HEADS),
                             jnp.tile(b_q_norm[0], B_HEADS) * (scale * log2e), jnp.tile(b_k_norm[0], B_HEADS)])[None, :]
    cos64, sin64 = _rope_tables(seq, HEAD_DIM)
    qat, ka, qbt, kb, vat, vbt = _even_proj(ctx, x, mods[0], norm_mix[0][None, :], w_ext, gains, e64, cos64, sin64)
    sink_row = (jnp.repeat(a_sink[0].reshape(A_KV_HEADS, GQA_R), A_BLOCK, axis=1) * log2e)[:, None, :]
    oa = _window_attn(qat, ka, vat, sink_row)
    ob = _na_attn(qbt, kb, vbt, *_na_bias_tables(b_rpb[0], log2e))
    wo = e_w_out[0].astype(BF16)
    xa = _even_out(ctx, x, oa, ob, mods[0], norm_mlp[0][None, :], wo[:512], wo[512:],
                   mlp_w1[0].astype(BF16), mlp_w2[0].astype(BF16))

    wi = o_w_in[0]
    win = jnp.concatenate([wi[:, :C_Q_RANK + C_KV_RANK]] + [wi[:, C_Q_RANK + C_KV_RANK:]] * 4, axis=1).astype(BF16)
    wuq = o_w_uq[0].reshape(C_Q_RANK, C_HEADS, C_NOPE + C_ROPE)
    wuq = jnp.concatenate([wuq[:, :, :C_NOPE].reshape(C_Q_RANK, -1), wuq[:, :, C_NOPE:].reshape(C_Q_RANK, -1)],
                          axis=1).astype(BF16)
    wukv = o_w_ukv[0].reshape(C_KV_RANK, C_HEADS, 2 * C_NOPE)
    wukv = jnp.concatenate([wukv[:, :, :C_NOPE].reshape(C_KV_RANK, -1), wukv[:, :, C_NOPE:].reshape(C_KV_RANK, -1)],
                           axis=1).astype(BF16)
    qscale = (C_NOPE + C_ROPE) ** -0.5 * math.log2(math.e)
    gq = (jnp.concatenate([jnp.tile(o_qn_nope[0], C_HEADS), jnp.tile(o_qn_rope[0], C_HEADS)]) * qscale)[None, :]
    gk = jnp.concatenate([jnp.tile(o_kn_nope[0], C_HEADS), jnp.tile(o_kn_rope[0], LANES // C_ROPE)])[None, :]
    cos32, sin32 = _rope_tables(seq, C_ROPE)
    qn, qr, kn, kr, vt = _odd_proj(xa, mods[1], norm_mix[1][None, :], win, o_qa_norm[0][None, :],
                                   o_kva_norm[0][None, :], wuq, wukv, gq, gk, e64, e32, cos32, sin32)
    ot = _flash_attn(qn, qr, kn, kr, vt)
    return _odd_out(xa, ot, mods[1], norm_mlp[1][None, :], o_w_out[0].astype(BF16),
                    mlp_w1[1].astype(BF16), mlp_w2[1].astype(BF16))
```
